```python
import math
import jax, jax.numpy as jnp
from jax import lax
import numpy as np

D_MODEL = 1024
BATCH = 8
SEQ = 4096
DEPTH = 4

SSD_D_INNER = 1024
SSD_HEAD_DIM = 64
SSD_HEADS = SSD_D_INNER // SSD_HEAD_DIM
SSD_GROUPS = 2
SSD_STATE = 128
SSD_CONV = 4
SSD_CHUNK = 128
SSD_CONV_DIM = SSD_D_INNER + 2 * SSD_GROUPS * SSD_STATE

ATTN_HEAD_DIM = 128
ATTN_HEADS_PER_GROUP = 4
DILATED_GROUPS = ((128, 1), (512, 4), (2048, 16))
ATTN_HEADS = ATTN_HEADS_PER_GROUP * len(DILATED_GROUPS)
ATTN_WIDTH = ATTN_HEADS * ATTN_HEAD_DIM
ATTN_OUT_WIDTH = ATTN_HEADS_PER_GROUP * ATTN_HEAD_DIM

POOL_WINDOWS = (2, 4, 8, 16)
POOL_WIDTH = 1024
POOL_GROUP_WIDTH = POOL_WIDTH // len(POOL_WINDOWS)

N_BRANCHES = 3
D_FF = 4 * D_MODEL
EPS = 1e-6
IN_SIZES = (SSD_D_INNER, SSD_CONV_DIM, SSD_HEADS, ATTN_WIDTH, ATTN_WIDTH, ATTN_WIDTH, POOL_WIDTH, N_BRANCHES * D_MODEL)
IN_WIDTH = SSD_D_INNER + SSD_CONV_DIM + SSD_HEADS + 3 * ATTN_WIDTH + POOL_WIDTH + N_BRANCHES * D_MODEL

kernel_name = "hybrid_ssd_dilattn_pool_gated"


def _rms(x, w):
    xf = x.astype(jnp.float32)
    y = xf * lax.rsqrt(jnp.mean(xf * xf, axis=-1, keepdims=True) + EPS)
    return (y * w.astype(jnp.float32)).astype(x.dtype)


def _alibi_slopes(n):
    def pow2(k):
        start = 2.0 ** (-8.0 / k)
        return [start ** (i + 1) for i in range(k)]
    if math.log2(n).is_integer():
        s = pow2(n)
    else:
        c = 2 ** math.floor(math.log2(n))
        s = pow2(c) + pow2(2 * c)[0::2][: n - c]
    return np.sort(np.asarray(s, np.float32))[::-1].copy()


def _causal_dwconv(x, w, bias):
    ch = x.shape[-1]
    y = lax.conv_general_dilated(x, w[:, None, :].astype(x.dtype), window_strides=(1,),
                                 padding=[(SSD_CONV - 1, 0)],
                                 dimension_numbers=('NWC', 'WIO', 'NWC'),
                                 feature_group_count=ch)
    return y + bias


def _ssd(xs, dt, A, B, C, D):
    b, S, H, P = xs.shape
    G, N, L = SSD_GROUPS, SSD_STATE, SSD_CHUNK
    J = H // G
    nc = S // L
    xc = xs.reshape(b, nc, L, G, J, P)
    X = xc * dt.reshape(b, nc, L, G, J)[..., None]
    a_cum = jnp.cumsum((dt * A).reshape(b, nc, L, G, J), axis=2)
    Bc = B.reshape(b, nc, L, G, N)
    Cc = C.reshape(b, nc, L, G, N)
    seg = a_cum[:, :, :, None] - a_cum[:, :, None, :]
    causal = jnp.tril(jnp.ones((L, L), bool))[:, :, None, None]
    decay = jnp.exp(jnp.where(causal, seg, -jnp.inf))
    cb = jnp.einsum('bclgn,bcsgn->bclsg', Cc, Bc)
    y_diag = jnp.einsum('bclsgj,bcsgjp->bclgjp', cb[..., None] * decay, X)
    Xd = X * jnp.exp(a_cum[:, :, -1:] - a_cum)[..., None]
    states = jnp.einsum('bclgn,bclgjp->bcgjpn', Bc, Xd)
    chunk_decay = jnp.exp(a_cum[:, :, -1])

    def step(h, inp):
        st, dec = inp
        return dec[..., None, None] * h + st, h

    h0 = jnp.zeros((b, G, J, P, N), X.dtype)
    _, prev = lax.scan(step, h0, (jnp.moveaxis(states, 1, 0), jnp.moveaxis(chunk_decay, 1, 0)))
    prev = jnp.moveaxis(prev, 0, 1)
    y_off = jnp.einsum('bclgn,bcgjpn->bclgjp', Cc, prev) * jnp.exp(a_cum)[..., None]
    y = y_diag + y_off + D.reshape(G, J)[:, :, None] * xc
    return y.reshape(b, S, H * P)


def _dilated_window_attn(q, k, v, steps, dilation, slopes):
    b, S, h, e = q.shape
    n = S // dilation
    nb = -(-n // steps)
    n_pad = nb * steps

    def to_sub(t):
        t = t.reshape(b, n, dilation, h, e).transpose(0, 3, 2, 1, 4)
        return jnp.pad(t, ((0, 0), (0, 0), (0, 0), (0, n_pad - n), (0, 0)))

    qs, ks, vs = to_sub(q), to_sub(k), to_sub(v)
    qb = qs.reshape(b, h, dilation, nb, steps, e)

    def band(t):
        tp = jnp.pad(t, ((0, 0), (0, 0), (0, 0), (steps, 0), (0, 0)))
        prev = tp[:, :, :, :n_pad].reshape(b, h, dilation, nb, steps, e)
        cur = t.reshape(b, h, dilation, nb, steps, e)
        return jnp.concatenate([prev, cur], axis=4)

    kb, vb = band(ks), band(vs)
    s = jnp.einsum('bhdnqe,bhdnke->bhdnqk', qb, kb) * (e ** -0.5)
    qi = jnp.arange(steps)[:, None]
    kj = jnp.arange(2 * steps)[None, :]
    rel = qi + steps - kj
    blk = jnp.arange(nb)[:, None, None]
    valid = (rel >= 0) & (rel <= steps) & (blk * steps + kj - steps >= 0)
    alibi = -slopes[:, None, None] * (rel * dilation).astype(jnp.float32)
    s = jnp.where(valid, s + alibi[:, None, None], -jnp.inf)
    lse = jax.nn.logsumexp(s, axis=-1)
    p = jnp.exp(s - lse[..., None])
    o = jnp.einsum('bhdnqk,bhdnke->bhdnqe', p, vb)
    o = o.reshape(b, h, dilation, n_pad, e)[:, :, :, :n].transpose(0, 3, 2, 1, 4).reshape(b, S, h, e)
    lse = lse.reshape(b, h, dilation, n_pad)[:, :, :, :n].transpose(0, 3, 2, 1).reshape(b, S, h)
    return o, lse


def _multi_scale_pool(u, w_mix):
    b, S, _ = u.shape
    ug = u.reshape(b, S, len(POOL_WINDOWS), POOL_GROUP_WIDTH)
    cs = jnp.cumsum(ug, axis=1)
    t = jnp.arange(S)
    outs = []
    for gi, w in enumerate(POOL_WINDOWS):
        csg = cs[:, :, gi]
        shifted = jnp.pad(csg, ((0, 0), (w, 0), (0, 0)))[:, :S]
        count = jnp.minimum(t + 1, w).astype(jnp.float32)[None, :, None]
        outs.append((csg - shifted) / count - ug[:, :, gi])
    pooled = jnp.stack(outs, axis=2)
    return jnp.einsum('bsgi,gio->bsgo', pooled, w_mix).reshape(b, S, POOL_WIDTH)


def _hybrid_mixer(h, w_in, conv_w, conv_b, dt_bias, a_log, d_skip, ssd_norm_w, w_ssd_out,
                  q_norm_w, k_norm_w, w_attn_out, w_pool_mix, pool_scale, w_pool_out, w_out, slopes):
    f32 = jnp.float32
    dtype = h.dtype
    b, S, _ = h.shape
    splits = np.cumsum(IN_SIZES)[:-1].tolist()
    z, xbc, dt_raw, q, k, v, u, gates = jnp.split(h @ w_in, splits, axis=-1)

    xbc = jax.nn.silu(_causal_dwconv(xbc, conv_w, conv_b)).astype(f32)
    xs, Bm, Cm = jnp.split(xbc, [SSD_D_INNER, SSD_D_INNER + SSD_GROUPS * SSD_STATE], axis=-1)
    dt = jax.nn.softplus(dt_raw.astype(f32) + dt_bias.astype(f32))
    A = -jnp.exp(a_log.astype(f32))
    y = _ssd(xs.reshape(b, S, SSD_HEADS, SSD_HEAD_DIM), dt, A,
             Bm.reshape(b, S, SSD_GROUPS, SSD_STATE), Cm.reshape(b, S, SSD_GROUPS, SSD_STATE),
             d_skip.astype(f32))
    y = (y * jax.nn.silu(z.astype(f32))).reshape(b, S, SSD_GROUPS, SSD_D_INNER // SSD_GROUPS)
    y = (y * lax.rsqrt(jnp.mean(y * y, axis=-1, keepdims=True) + EPS)).reshape(b, S, SSD_D_INNER)
    y_ssd = (y * ssd_norm_w.astype(f32)).astype(dtype) @ w_ssd_out

    qh = _rms(q.reshape(b, S, ATTN_HEADS, ATTN_HEAD_DIM).astype(f32), q_norm_w)
    kh = _rms(k.reshape(b, S, ATTN_HEADS, ATTN_HEAD_DIM).astype(f32), k_norm_w)
    vh = v.reshape(b, S, ATTN_HEADS, ATTN_HEAD_DIM).astype(f32)
    outs, lses = [], []
    for gi, (win, dil) in enumerate(DILATED_GROUPS):
        sl = slice(gi * ATTN_HEADS_PER_GROUP, (gi + 1) * ATTN_HEADS_PER_GROUP)
        o, lse = _dilated_window_attn(qh[:, :, sl], kh[:, :, sl], vh[:, :, sl], win // dil, dil, slopes[gi])
        outs.append(o)
        lses.append(lse)
    wts = jax.nn.softmax(jnp.stack(lses, axis=0), axis=0)
    o = jnp.sum(wts[..., None] * jnp.stack(outs, axis=0), axis=0).reshape(b, S, ATTN_OUT_WIDTH)
    y_attn = o.astype(dtype) @ w_attn_out

    y_pool = (_multi_scale_pool(u.astype(f32), w_pool_mix) * pool_scale).astype(dtype) @ w_pool_out

    g_ssd, g_attn, g_pool = jnp.split(jax.nn.sigmoid(gates.astype(f32)).astype(dtype), N_BRANCHES, axis=-1)
    return (g_ssd * y_ssd + g_attn * y_attn + g_pool * y_pool) @ w_out


def _fwd_setup_inputs(seed: int = 0) -> dict:
    key = jax.random.key(seed)
    ks = jax.random.split(key, 24)
    n = jax.random.normal
    L, D = DEPTH, D_MODEL
    u01 = jax.random.uniform(ks[9], (L, SSD_HEADS))
    dt0 = jnp.exp(u01 * (math.log(0.1) - math.log(0.001)) + math.log(0.001))
    return {
        "x": n(ks[0], (BATCH, SEQ, D), jnp.float32),
        "c": n(ks[1], (BATCH, D), jnp.float32),
        "w_ada": n(ks[2], (L, D, 6 * D)) * D ** -0.5,
        "b_ada": n(ks[3], (L, 6 * D)) * 0.01,
        "norm1_w": 1.0 + 0.05 * n(ks[4], (L, D)),
        "norm2_w": 1.0 + 0.05 * n(ks[5], (L, D)),
        "w_in": n(ks[6], (L, D, IN_WIDTH)) * D ** -0.5,
        "conv_w": n(ks[7], (L, SSD_CONV, SSD_CONV_DIM)) * SSD_CONV ** -0.5,
        "conv_b": n(ks[8], (L, SSD_CONV_DIM)) * 0.01,
        "dt_bias": dt0 + jnp.log(-jnp.expm1(-dt0)),
        "a_log": jnp.log(jax.random.uniform(ks[10], (L, SSD_HEADS), minval=1.0, maxval=16.0)),
        "d_skip": 1.0 + 0.05 * n(ks[11], (L, SSD_HEADS)),
        "ssd_norm_w": 1.0 + 0.05 * n(ks[12], (L, SSD_D_INNER)),
        "w_ssd_out": n(ks[13], (L, SSD_D_INNER, D)) * SSD_D_INNER ** -0.5,
        "q_norm_w": 1.0 + 0.05 * n(ks[14], (L, ATTN_HEAD_DIM)),
        "k_norm_w": 1.0 + 0.05 * n(ks[15], (L, ATTN_HEAD_DIM)),
        "w_attn_out": n(ks[16], (L, ATTN_OUT_WIDTH, D)) * ATTN_OUT_WIDTH ** -0.5,
        "w_pool_mix": n(ks[17], (L, len(POOL_WINDOWS), POOL_GROUP_WIDTH, POOL_GROUP_WIDTH)) * POOL_GROUP_WIDTH ** -0.5,
        "pool_scale": 1.0 + 0.05 * n(ks[18], (L, POOL_WIDTH)),
        "w_pool_out": n(ks[19], (L, POOL_WIDTH, D)) * POOL_WIDTH ** -0.5,
        "w_out": n(ks[20], (L, D, D)) * D ** -0.5,
        "w_ff1": n(ks[21], (L, D, D_FF)) * D ** -0.5,
        "w_ff2": n(ks[22], (L, D_FF, D)) * D_FF ** -0.5,
    }


def _fwd_reference(x, c, w_ada, b_ada, norm1_w, norm2_w, w_in, conv_w, conv_b, dt_bias, a_log, d_skip,
              ssd_norm_w, w_ssd_out, q_norm_w, k_norm_w, w_attn_out, w_pool_mix, pool_scale,
              w_pool_out, w_out, w_ff1, w_ff2):
    cond = jax.nn.silu(c)
    slopes = jnp.asarray(_alibi_slopes(ATTN_HEADS)).reshape(len(DILATED_GROUPS), ATTN_HEADS_PER_GROUP)
    for l in range(DEPTH):
        mod = (cond @ w_ada[l] + b_ada[l])[:, None, :]
        sh1, sc1, g1, sh2, sc2, g2 = jnp.split(mod, 6, axis=-1)
        h = _rms(x, norm1_w[l]) * (1 + sc1) + sh1
        x = x + g1 * _hybrid_mixer(h, w_in[l], conv_w[l], conv_b[l], dt_bias[l], a_log[l], d_skip[l],
                                   ssd_norm_w[l], w_ssd_out[l], q_norm_w[l], k_norm_w[l], w_attn_out[l],
                                   w_pool_mix[l], pool_scale[l], w_pool_out[l], w_out[l], slopes)
        h = _rms(x, norm2_w[l]) * (1 + sc2) + sh2
        x = x + g2 * (jnp.square(jax.nn.relu(h @ w_ff1[l])) @ w_ff2[l])
    return x


import jax as _jax
import jax.numpy as _jnp

TWIN_FORMAT = 'train_step'
FWD_PARAMS = ['x', 'c', 'w_ada', 'b_ada', 'norm1_w', 'norm2_w', 'w_in', 'conv_w', 'conv_b', 'dt_bias', 'a_log', 'd_skip', 'ssd_norm_w', 'w_ssd_out', 'q_norm_w', 'k_norm_w', 'w_attn_out', 'w_pool_mix', 'pool_scale', 'w_pool_out', 'w_out', 'w_ff1', 'w_ff2']
TWIN_WEIGHTS = ['w_ada', 'b_ada', 'norm1_w', 'norm2_w', 'w_in', 'conv_w', 'conv_b', 'dt_bias', 'a_log', 'd_skip', 'ssd_norm_w', 'w_ssd_out', 'q_norm_w', 'k_norm_w', 'w_attn_out', 'w_pool_mix', 'pool_scale', 'w_pool_out', 'w_out', 'w_ff1', 'w_ff2']
TWIN_DIFF_INPUT = 'x'
TWIN_INPUTS = ['x', 'c', 'w_ada', 'b_ada', 'norm1_w', 'norm2_w', 'w_in', 'conv_w', 'conv_b', 'dt_bias', 'a_log', 'd_skip', 'ssd_norm_w', 'w_ssd_out', 'q_norm_w', 'k_norm_w', 'w_attn_out', 'w_pool_mix', 'pool_scale', 'w_pool_out', 'w_out', 'w_ff1', 'w_ff2', 'loss_target', 'm_w_ada', 'm_b_ada', 'm_norm1_w', 'm_norm2_w', 'm_w_in', 'm_conv_w', 'm_conv_b', 'm_dt_bias', 'm_a_log', 'm_d_skip', 'm_ssd_norm_w', 'm_w_ssd_out', 'm_q_norm_w', 'm_k_norm_w', 'm_w_attn_out', 'm_w_pool_mix', 'm_pool_scale', 'm_w_pool_out', 'm_w_out', 'm_w_ff1', 'm_w_ff2', 'v_w_ada', 'v_b_ada', 'v_norm1_w', 'v_norm2_w', 'v_w_in', 'v_conv_w', 'v_conv_b', 'v_dt_bias', 'v_a_log', 'v_d_skip', 'v_ssd_norm_w', 'v_w_ssd_out', 'v_q_norm_w', 'v_k_norm_w', 'v_w_attn_out', 'v_w_pool_mix', 'v_pool_scale', 'v_w_pool_out', 'v_w_out', 'v_w_ff1', 'v_w_ff2']
TWIN_OUTPUTS = ['loss', 'grad_x', 'grad_w_ada', 'grad_b_ada', 'grad_norm1_w', 'grad_norm2_w', 'grad_w_in', 'grad_conv_w', 'grad_conv_b', 'grad_dt_bias', 'grad_a_log', 'grad_d_skip', 'grad_ssd_norm_w', 'grad_w_ssd_out', 'grad_q_norm_w', 'grad_k_norm_w', 'grad_w_attn_out', 'grad_w_pool_mix', 'grad_pool_scale', 'grad_w_pool_out', 'grad_w_out', 'grad_w_ff1', 'grad_w_ff2', 'delta_w_ada', 'delta_b_ada', 'delta_norm1_w', 'delta_norm2_w', 'delta_w_in', 'delta_conv_w', 'delta_conv_b', 'delta_dt_bias', 'delta_a_log', 'delta_d_skip', 'delta_ssd_norm_w', 'delta_w_ssd_out', 'delta_q_norm_w', 'delta_k_norm_w', 'delta_w_attn_out', 'delta_w_pool_mix', 'delta_pool_scale', 'delta_w_pool_out', 'delta_w_out', 'delta_w_ff1', 'delta_w_ff2', 'new_m_w_ada', 'new_m_b_ada', 'new_m_norm1_w', 'new_m_norm2_w', 'new_m_w_in', 'new_m_conv_w', 'new_m_conv_b', 'new_m_dt_bias', 'new_m_a_log', 'new_m_d_skip', 'new_m_ssd_norm_w', 'new_m_w_ssd_out', 'new_m_q_norm_w', 'new_m_k_norm_w', 'new_m_w_attn_out', 'new_m_w_pool_mix', 'new_m_pool_scale', 'new_m_w_pool_out', 'new_m_w_out', 'new_m_w_ff1', 'new_m_w_ff2', 'new_v_w_ada', 'new_v_b_ada', 'new_v_norm1_w', 'new_v_norm2_w', 'new_v_w_in', 'new_v_conv_w', 'new_v_conv_b', 'new_v_dt_bias', 'new_v_a_log', 'new_v_d_skip', 'new_v_ssd_norm_w', 'new_v_w_ssd_out', 'new_v_q_norm_w', 'new_v_k_norm_w', 'new_v_w_attn_out', 'new_v_w_pool_mix', 'new_v_pool_scale', 'new_v_w_pool_out', 'new_v_w_out', 'new_v_w_ff1', 'new_v_w_ff2']
TWIN_LEAF_KINDS = {'loss': 'loss', 'grad_x': 'grad_x', 'grad_w_ada': 'grad_w', 'grad_b_ada': 'grad_w', 'grad_norm1_w': 'grad_w', 'grad_norm2_w': 'grad_w', 'grad_w_in': 'grad_w', 'grad_conv_w': 'grad_w', 'grad_conv_b': 'grad_w', 'grad_dt_bias': 'grad_w', 'grad_a_log': 'grad_w', 'grad_d_skip': 'grad_w', 'grad_ssd_norm_w': 'grad_w', 'grad_w_ssd_out': 'grad_w', 'grad_q_norm_w': 'grad_w', 'grad_k_norm_w': 'grad_w', 'grad_w_attn_out': 'grad_w', 'grad_w_pool_mix': 'grad_w', 'grad_pool_scale': 'grad_w', 'grad_w_pool_out': 'grad_w', 'grad_w_out': 'grad_w', 'grad_w_ff1': 'grad_w', 'grad_w_ff2': 'grad_w', 'delta_w_ada': 'delta_w', 'delta_b_ada': 'delta_w', 'delta_norm1_w': 'delta_w', 'delta_norm2_w': 'delta_w', 'delta_w_in': 'delta_w', 'delta_conv_w': 'delta_w', 'delta_conv_b': 'delta_w', 'delta_dt_bias': 'delta_w', 'delta_a_log': 'delta_w', 'delta_d_skip': 'delta_w', 'delta_ssd_norm_w': 'delta_w', 'delta_w_ssd_out': 'delta_w', 'delta_q_norm_w': 'delta_w', 'delta_k_norm_w': 'delta_w', 'delta_w_attn_out': 'delta_w', 'delta_w_pool_mix': 'delta_w', 'delta_pool_scale': 'delta_w', 'delta_w_pool_out': 'delta_w', 'delta_w_out': 'delta_w', 'delta_w_ff1': 'delta_w', 'delta_w_ff2': 'delta_w', 'new_m_w_ada': 'new_m', 'new_m_b_ada': 'new_m', 'new_m_norm1_w': 'new_m', 'new_m_norm2_w': 'new_m', 'new_m_w_in': 'new_m', 'new_m_conv_w': 'new_m', 'new_m_conv_b': 'new_m', 'new_m_dt_bias': 'new_m', 'new_m_a_log': 'new_m', 'new_m_d_skip': 'new_m', 'new_m_ssd_norm_w': 'new_m', 'new_m_w_ssd_out': 'new_m', 'new_m_q_norm_w': 'new_m', 'new_m_k_norm_w': 'new_m', 'new_m_w_attn_out': 'new_m', 'new_m_w_pool_mix': 'new_m', 'new_m_pool_scale': 'new_m', 'new_m_w_pool_out': 'new_m', 'new_m_w_out': 'new_m', 'new_m_w_ff1': 'new_m', 'new_m_w_ff2': 'new_m', 'new_v_w_ada': 'new_v', 'new_v_b_ada': 'new_v', 'new_v_norm1_w': 'new_v', 'new_v_norm2_w': 'new_v', 'new_v_w_in': 'new_v', 'new_v_conv_w': 'new_v', 'new_v_conv_b': 'new_v', 'new_v_dt_bias': 'new_v', 'new_v_a_log': 'new_v', 'new_v_d_skip': 'new_v', 'new_v_ssd_norm_w': 'new_v', 'new_v_w_ssd_out': 'new_v', 'new_v_q_norm_w': 'new_v', 'new_v_k_norm_w': 'new_v', 'new_v_w_attn_out': 'new_v', 'new_v_w_pool_mix': 'new_v', 'new_v_pool_scale': 'new_v', 'new_v_w_pool_out': 'new_v', 'new_v_w_out': 'new_v', 'new_v_w_ff1': 'new_v', 'new_v_w_ff2': 'new_v'}


def _forward(args):
    return _fwd_reference(*[args[k] for k in FWD_PARAMS])


def _output_shape():
    def fwd():
        inp = _fwd_setup_inputs(0)
        return _fwd_reference(*[inp[k] for k in FWD_PARAMS])
    out = _jax.eval_shape(fwd)
    return out.shape, out.dtype

N_MICROBATCH = 1
ADAM_LR = 0.001
ADAM_B1 = 0.9
ADAM_B2 = 0.999
ADAM_EPS = 1e-08
ADAM_WD = 0.01
ADAM_STEP = 10
PER_EXAMPLE_BATCH_AXIS = {'x': 0, 'c': 0, 'loss_target': 0}
SHARED_INPUTS = []
_WEIGHT_DTYPES = {'w_ada': _jnp.float32, 'b_ada': _jnp.float32, 'norm1_w': _jnp.float32, 'norm2_w': _jnp.float32, 'w_in': _jnp.float32, 'conv_w': _jnp.float32, 'conv_b': _jnp.float32, 'dt_bias': _jnp.float32, 'a_log': _jnp.float32, 'd_skip': _jnp.float32, 'ssd_norm_w': _jnp.float32, 'w_ssd_out': _jnp.float32, 'q_norm_w': _jnp.float32, 'k_norm_w': _jnp.float32, 'w_attn_out': _jnp.float32, 'w_pool_mix': _jnp.float32, 'pool_scale': _jnp.float32, 'w_pool_out': _jnp.float32, 'w_out': _jnp.float32, 'w_ff1': _jnp.float32, 'w_ff2': _jnp.float32}
MOMENT_SCALE = {'w_ada': 1.946449e+01, 'b_ada': 4.331421e+01, 'norm1_w': 6.917887e+00, 'norm2_w': 9.919659e+01, 'w_in': 2.607050e+00, 'conv_w': 4.211280e+00, 'conv_b': 5.017273e+00, 'dt_bias': 3.609884e+00, 'a_log': 1.991892e+01, 'd_skip': 1.036131e+01, 'ssd_norm_w': 1.062174e+01, 'w_ssd_out': 5.852020e+00, 'q_norm_w': 3.960487e+00, 'k_norm_w': 3.957546e+00, 'w_attn_out': 5.161759e+00, 'w_pool_mix': 6.936027e-01, 'pool_scale': 3.220255e+00, 'w_pool_out': 6.899365e-01, 'w_out': 7.796024e+00, 'w_ff1': 1.142832e+01, 'w_ff2': 2.609519e+01}


def _to_microbatches(a, axis):
    t = _jnp.moveaxis(a, axis, 0)
    t = t.reshape((N_MICROBATCH, t.shape[0] // N_MICROBATCH) + t.shape[1:])
    return _jnp.moveaxis(t, 1, axis + 1)


def setup_inputs(seed: int = 0) -> dict:
    inp = _fwd_setup_inputs(seed)
    key = _jax.random.fold_in(_jax.random.key(seed), 7919)
    shape, _ = _output_shape()
    out = dict(inp)
    out["loss_target"] = _jax.random.normal(_jax.random.fold_in(key, 0), shape, _jnp.float32)
    for i, name in enumerate(TWIN_WEIGHTS):
        w = inp[name].astype(_jnp.float32)
        if MOMENT_SCALE is None:
            s = _jnp.sqrt(_jnp.mean(_jnp.square(w)) + 1e-30)
        else:
            s = MOMENT_SCALE[name]
        km, kv = _jax.random.split(_jax.random.fold_in(key, i + 1))
        out[name] = w
        out["m_" + name] = s * _jax.random.normal(km, w.shape, _jnp.float32)
        out["v_" + name] = (s * s) * _jax.random.uniform(kv, w.shape, _jnp.float32, 0.5, 1.5)
    if N_MICROBATCH > 1:
        for name, axis in PER_EXAMPLE_BATCH_AXIS.items():
            out[name] = _to_microbatches(out[name], axis)
    return {'x': out['x'], 'c': out['c'], 'w_ada': out['w_ada'], 'b_ada': out['b_ada'], 'norm1_w': out['norm1_w'], 'norm2_w': out['norm2_w'], 'w_in': out['w_in'], 'conv_w': out['conv_w'], 'conv_b': out['conv_b'], 'dt_bias': out['dt_bias'], 'a_log': out['a_log'], 'd_skip': out['d_skip'], 'ssd_norm_w': out['ssd_norm_w'], 'w_ssd_out': out['w_ssd_out'], 'q_norm_w': out['q_norm_w'], 'k_norm_w': out['k_norm_w'], 'w_attn_out': out['w_attn_out'], 'w_pool_mix': out['w_pool_mix'], 'pool_scale': out['pool_scale'], 'w_pool_out': out['w_pool_out'], 'w_out': out['w_out'], 'w_ff1': out['w_ff1'], 'w_ff2': out['w_ff2'], 'loss_target': out['loss_target'], 'm_w_ada': out['m_w_ada'], 'm_b_ada': out['m_b_ada'], 'm_norm1_w': out['m_norm1_w'], 'm_norm2_w': out['m_norm2_w'], 'm_w_in': out['m_w_in'], 'm_conv_w': out['m_conv_w'], 'm_conv_b': out['m_conv_b'], 'm_dt_bias': out['m_dt_bias'], 'm_a_log': out['m_a_log'], 'm_d_skip': out['m_d_skip'], 'm_ssd_norm_w': out['m_ssd_norm_w'], 'm_w_ssd_out': out['m_w_ssd_out'], 'm_q_norm_w': out['m_q_norm_w'], 'm_k_norm_w': out['m_k_norm_w'], 'm_w_attn_out': out['m_w_attn_out'], 'm_w_pool_mix': out['m_w_pool_mix'], 'm_pool_scale': out['m_pool_scale'], 'm_w_pool_out': out['m_w_pool_out'], 'm_w_out': out['m_w_out'], 'm_w_ff1': out['m_w_ff1'], 'm_w_ff2': out['m_w_ff2'], 'v_w_ada': out['v_w_ada'], 'v_b_ada': out['v_b_ada'], 'v_norm1_w': out['v_norm1_w'], 'v_norm2_w': out['v_norm2_w'], 'v_w_in': out['v_w_in'], 'v_conv_w': out['v_conv_w'], 'v_conv_b': out['v_conv_b'], 'v_dt_bias': out['v_dt_bias'], 'v_a_log': out['v_a_log'], 'v_d_skip': out['v_d_skip'], 'v_ssd_norm_w': out['v_ssd_norm_w'], 'v_w_ssd_out': out['v_w_ssd_out'], 'v_q_norm_w': out['v_q_norm_w'], 'v_k_norm_w': out['v_k_norm_w'], 'v_w_attn_out': out['v_w_attn_out'], 'v_w_pool_mix': out['v_w_pool_mix'], 'v_pool_scale': out['v_pool_scale'], 'v_w_pool_out': out['v_w_pool_out'], 'v_w_out': out['v_w_out'], 'v_w_ff1': out['v_w_ff1'], 'v_w_ff2': out['v_w_ff2']}


def _loss(weights, diff, rest, loss_target):
    with _jax.named_scope("forward"):
        args = {**rest, TWIN_DIFF_INPUT: diff, **{k: w.astype(_WEIGHT_DTYPES[k]) for k, w in weights.items()}}
        y = _forward(args)
    with _jax.named_scope("loss_head"):
        err = _jnp.square(y.astype(_jnp.float32) - loss_target)
        return 0.5 * _jnp.sum(_jnp.mean(err, axis=-1)) if err.ndim else 0.5 * err


def _adamw(w, g, m, v):
    m = ADAM_B1 * m + (1.0 - ADAM_B1) * g
    v = ADAM_B2 * v + (1.0 - ADAM_B2) * _jnp.square(g)
    m_hat = m / (1.0 - ADAM_B1 ** ADAM_STEP)
    v_hat = v / (1.0 - ADAM_B2 ** ADAM_STEP)
    delta = -ADAM_LR * (m_hat / (_jnp.sqrt(v_hat) + ADAM_EPS) + ADAM_WD * w)
    return delta, m, v


def reference(x, c, w_ada, b_ada, norm1_w, norm2_w, w_in, conv_w, conv_b, dt_bias, a_log, d_skip, ssd_norm_w, w_ssd_out, q_norm_w, k_norm_w, w_attn_out, w_pool_mix, pool_scale, w_pool_out, w_out, w_ff1, w_ff2, loss_target, m_w_ada, m_b_ada, m_norm1_w, m_norm2_w, m_w_in, m_conv_w, m_conv_b, m_dt_bias, m_a_log, m_d_skip, m_ssd_norm_w, m_w_ssd_out, m_q_norm_w, m_k_norm_w, m_w_attn_out, m_w_pool_mix, m_pool_scale, m_w_pool_out, m_w_out, m_w_ff1, m_w_ff2, v_w_ada, v_b_ada, v_norm1_w, v_norm2_w, v_w_in, v_conv_w, v_conv_b, v_dt_bias, v_a_log, v_d_skip, v_ssd_norm_w, v_w_ssd_out, v_q_norm_w, v_k_norm_w, v_w_attn_out, v_w_pool_mix, v_pool_scale, v_w_pool_out, v_w_out, v_w_ff1, v_w_ff2):
    given = dict(x=x, c=c, w_ada=w_ada, b_ada=b_ada, norm1_w=norm1_w, norm2_w=norm2_w, w_in=w_in, conv_w=conv_w, conv_b=conv_b, dt_bias=dt_bias, a_log=a_log, d_skip=d_skip, ssd_norm_w=ssd_norm_w, w_ssd_out=w_ssd_out, q_norm_w=q_norm_w, k_norm_w=k_norm_w, w_attn_out=w_attn_out, w_pool_mix=w_pool_mix, pool_scale=pool_scale, w_pool_out=w_pool_out, w_out=w_out, w_ff1=w_ff1, w_ff2=w_ff2, loss_target=loss_target, m_w_ada=m_w_ada, m_b_ada=m_b_ada, m_norm1_w=m_norm1_w, m_norm2_w=m_norm2_w, m_w_in=m_w_in, m_conv_w=m_conv_w, m_conv_b=m_conv_b, m_dt_bias=m_dt_bias, m_a_log=m_a_log, m_d_skip=m_d_skip, m_ssd_norm_w=m_ssd_norm_w, m_w_ssd_out=m_w_ssd_out, m_q_norm_w=m_q_norm_w, m_k_norm_w=m_k_norm_w, m_w_attn_out=m_w_attn_out, m_w_pool_mix=m_w_pool_mix, m_pool_scale=m_pool_scale, m_w_pool_out=m_w_pool_out, m_w_out=m_w_out, m_w_ff1=m_w_ff1, m_w_ff2=m_w_ff2, v_w_ada=v_w_ada, v_b_ada=v_b_ada, v_norm1_w=v_norm1_w, v_norm2_w=v_norm2_w, v_w_in=v_w_in, v_conv_w=v_conv_w, v_conv_b=v_conv_b, v_dt_bias=v_dt_bias, v_a_log=v_a_log, v_d_skip=v_d_skip, v_ssd_norm_w=v_ssd_norm_w, v_w_ssd_out=v_w_ssd_out, v_q_norm_w=v_q_norm_w, v_k_norm_w=v_k_norm_w, v_w_attn_out=v_w_attn_out, v_w_pool_mix=v_w_pool_mix, v_pool_scale=v_pool_scale, v_w_pool_out=v_w_pool_out, v_w_out=v_w_out, v_w_ff1=v_w_ff1, v_w_ff2=v_w_ff2)
    weights = {n: given[n] for n in TWIN_WEIGHTS}
    shared = {n: given[n] for n in SHARED_INPUTS}
    per_example = {n: given[n] for n in ['x', 'c']}
    grad_fn = _jax.value_and_grad(_loss, argnums=(0, 1))

    def one_microbatch(ex, loss_target):
        ex = dict(ex)
        diff = ex.pop(TWIN_DIFF_INPUT)
        return grad_fn(weights, diff, {**shared, **ex}, loss_target)

    if N_MICROBATCH == 1:
        loss, (grad_w, grad_x) = one_microbatch(per_example, given["loss_target"])
    else:
        def body(carry, xs):
            loss_sum, grad_sum = carry
            l_k, (gw_k, gx_k) = one_microbatch(xs[0], xs[1])
            with _jax.named_scope("update"):
                return (loss_sum + l_k, _jax.tree.map(_jnp.add, grad_sum, gw_k)), gx_k

        init = (_jnp.zeros((), _jnp.float32), _jax.tree.map(_jnp.zeros_like, weights))
        (loss, grad_w), grad_x = _jax.lax.scan(body, init, (per_example, given["loss_target"]))
    with _jax.named_scope("update"):
        delta_w, new_m, new_v = {}, {}, {}
        for n in TWIN_WEIGHTS:
            delta_w[n], new_m[n], new_v[n] = _adamw(weights[n], grad_w[n], given["m_" + n], given["v_" + n])
    return (loss, grad_x, *[grad_w[n] for n in TWIN_WEIGHTS], *[delta_w[n] for n in TWIN_WEIGHTS],
            *[new_m[n] for n in TWIN_WEIGHTS], *[new_v[n] for n in TWIN_WEIGHTS])
```

```python
import functools
import math

import numpy as np
import jax
import jax.numpy as jnp
from jax import lax
from jax.experimental import pallas as pl
from jax.experimental.pallas import tpu as pltpu

F32 = jnp.float32
BF16 = jnp.bfloat16
MESH = pl.DeviceIdType.MESH
HIGHEST = lax.Precision.HIGHEST

N_DEV = 8
V7X_VMEM_LIMIT_BYTES = 56 * 1024 * 1024
LANES = 128
ROW_TILE = 256

EPS = 1e-6
D_MODEL = 1024
SSD_HEADS = 16
SSD_HEAD_DIM = 64
SSD_CHUNK = 128
SSD_STATE = 128
ATTN_HEADS = 12
ATTN_HEAD_DIM = 128
ATTN_STEPS = 128
DILATIONS = (1, 4, 16)
POOL_WINDOWS = (2, 4, 8, 16)
POOL_HALO = 16
CONV_HALO = 8
NEG = -1e30

ADAM_LR = 0.001
ADAM_B1 = 0.9
ADAM_B2 = 0.999
ADAM_EPS = 1e-08
ADAM_WD = 0.01
ADAM_STEP = 10

QKV_OFF, QKV_W = 0, 4608
XBC_OFF, XBC_W = 4608, 1536
GATE_OFF, GATE_W = 6144, 3072
Z_OFF, Z_W = 9216, 1024
U_OFF, U_W = 10240, 1024
DT_OFF, DT_W = 11264, 256
NP = 11520
IN_WIDTH = 11280
O_Z, O_XBC, O_DT, O_Q, O_K, O_V, O_U, O_G = 0, 1024, 2560, 2576, 4112, 5648, 7184, 8208


def _alibi_slopes(n):
    def pow2(k):
        start = 2.0 ** (-8.0 / k)
        return [start ** (i + 1) for i in range(k)]
    if math.log2(n).is_integer():
        s = pow2(n)
    else:
        c = 2 ** math.floor(math.log2(n))
        s = pow2(c) + pow2(2 * c)[0::2][: n - c]
    return np.sort(np.asarray(s, np.float32))[::-1].copy()


SLOPES = _alibi_slopes(ATTN_HEADS).reshape(3, 4)


def _cp(sem=None):
    return pltpu.CompilerParams(dimension_semantics=sem, vmem_limit_bytes=V7X_VMEM_LIMIT_BYTES)


def _pick(dim, cap):
    if dim <= cap:
        return dim
    t = cap - cap % LANES
    while t >= LANES:
        if dim % t == 0:
            return t
        t -= LANES
    raise ValueError((dim, cap))


def _sigmoid(x):
    return 1.0 / (1.0 + jnp.exp(-x))


def _silu(x):
    return x * _sigmoid(x)


def _dsilu(x):
    s = _sigmoid(x)
    return s * (1.0 + x * (1.0 - s))


def _softplus(x):
    return jnp.maximum(x, 0.0) + jnp.log(1.0 + jnp.exp(-jnp.abs(x)))


def _dot(a, b, dims):
    return lax.dot_general(a, b, (dims, ((), ())), preferred_element_type=F32)


def _nn(a, b):
    return _dot(a, b, ((1,), (0,)))


def _nt(a, b):
    return _dot(a, b, ((1,), (1,)))


def _tn(a, b):
    return _dot(a, b, ((0,), (0,)))


def _iota(shape, axis):
    return lax.broadcasted_iota(jnp.int32, shape, axis)


def mm(a, b, mode, out_dtype, name, add=None, tm_cap=512, tn_cap=1280, tk_cap=1024):
    if mode == "nn":
        (M, K), (K2, N) = a.shape, b.shape
    elif mode == "nt":
        (M, K), (N, K2) = a.shape, b.shape
    else:
        (K, M), (K2, N) = a.shape, b.shape
    assert K == K2, (a.shape, b.shape, mode)
    tm, tn, tk = _pick(M, tm_cap), _pick(N, tn_cap), _pick(K, tk_cap)
    nk = K // tk
    if mode == "tn":
        a_spec = pl.BlockSpec((tk, tm), lambda i, j, k: (k, i))
    else:
        a_spec = pl.BlockSpec((tm, tk), lambda i, j, k: (i, k))
    if mode == "nt":
        b_spec = pl.BlockSpec((tn, tk), lambda i, j, k: (j, k))
    else:
        b_spec = pl.BlockSpec((tk, tn), lambda i, j, k: (k, j))
    dims = {"nn": ((1,), (0,)), "nt": ((1,), (1,)), "tn": ((0,), (0,))}[mode]
    o_spec = pl.BlockSpec((tm, tn), lambda i, j, k: (i, j))
    has_add = add is not None

    def body(*refs):
        if has_add:
            a_ref, b_ref, add_ref, o_ref, acc = refs
        else:
            a_ref, b_ref, o_ref, acc = refs
        k = pl.program_id(2)

        @pl.when(k == 0)
        def _():
            acc[...] = jnp.zeros_like(acc)

        acc[...] += _dot(a_ref[...].astype(BF16), b_ref[...].astype(BF16), dims)

        @pl.when(k == nk - 1)
        def _():
            r = acc[...]
            if has_add:
                r = r + add_ref[...]
            o_ref[...] = r.astype(out_dtype)

    ins = [a, b] + ([add] if has_add else [])
    in_specs = [a_spec, b_spec] + ([o_spec] if has_add else [])
    return pl.pallas_call(
        body, grid=(M // tm, N // tn, nk), in_specs=in_specs, out_specs=o_spec,
        out_shape=jax.ShapeDtypeStruct((M, N), out_dtype),
        scratch_shapes=[pltpu.VMEM((tm, tn), F32)], name=name,
        compiler_params=_cp(("parallel", "parallel", "arbitrary")),
    )(*ins)


def _rows(tile, w, cb=0):
    return pl.BlockSpec((tile, w), lambda i: (i, cb))


def _vec(shape):
    nd = len(shape)
    return pl.BlockSpec(shape, lambda i: (0,) * nd)


def _any():
    return pl.BlockSpec(memory_space=pl.ANY)


def _sds(shape, dtype):
    return jax.ShapeDtypeStruct(shape, dtype)


def _rowcall(body, n_rows, tile, in_specs, out_specs, out_shape, name, aliases=None):
    return pl.pallas_call(
        body, grid=(n_rows // tile,), in_specs=in_specs, out_specs=out_specs, out_shape=out_shape,
        name=name, input_output_aliases=aliases or {}, compiler_params=_cp(("arbitrary",)))


def _zero_at_first(step, *refs):
    @pl.when(step == 0)
    def _():
        for r in refs:
            r[...] = jnp.zeros_like(r)


def rms_mod_fwd(x, nw, sc, sh, name):
    S, D = x.shape
    T = ROW_TILE

    def body(x_ref, nw_ref, sc_ref, sh_ref, h_ref):
        xv = x_ref[...]
        r = lax.rsqrt(jnp.mean(xv * xv, axis=-1, keepdims=True) + EPS)
        y = xv * r * nw_ref[...]
        h_ref[...] = (y * (1.0 + sc_ref[...]) + sh_ref[...]).astype(BF16)

    return _rowcall(body, S, T, [_rows(T, D), _vec((1, D)), _vec((1, D)), _vec((1, D))], _rows(T, D),
                    _sds((S, D), BF16), name)(x, nw, sc, sh)


def rms_mod_bwd(x, dh, dres, nw, sc, name):
    S, D = x.shape
    T = ROW_TILE

    def body(x_ref, dh_ref, dres_ref, nw_ref, sc_ref, dx_ref, dnw_ref, dsc_ref, dsh_ref):
        _zero_at_first(pl.program_id(0), dnw_ref, dsc_ref, dsh_ref)
        xv = x_ref[...]
        d = dh_ref[...]
        r = lax.rsqrt(jnp.mean(xv * xv, axis=-1, keepdims=True) + EPS)
        xh = xv * r
        nwv = nw_ref[...]
        dsh_ref[...] += jnp.sum(d, axis=0, keepdims=True)
        dsc_ref[...] += jnp.sum(d * (xh * nwv), axis=0, keepdims=True)
        dy = d * (1.0 + sc_ref[...])
        dnw_ref[...] += jnp.sum(dy * xh, axis=0, keepdims=True)
        dxh = dy * nwv
        dx_ref[...] = dres_ref[...] + r * (dxh - xh * jnp.mean(dxh * xh, axis=-1, keepdims=True))

    v = _vec((1, D))
    return _rowcall(body, S, T, [_rows(T, D), _rows(T, D), _rows(T, D), v, v],
                    [_rows(T, D), v, v, v],
                    [_sds((S, D), F32), _sds((1, D), F32), _sds((1, D), F32), _sds((1, D), F32)], name)(x, dh, dres, nw, sc)


def _shift_down(tile, halo, k):
    if k == 0:
        return tile
    hr = halo.shape[0]
    rolled = pltpu.roll(tile, k, 0)
    rh = pltpu.roll(halo, k, 0)
    rid = _iota(halo.shape, 0)
    first = jnp.where(rid < k, rh, rolled[:hr])
    return jnp.concatenate([first, rolled[hr:]], axis=0)


def _shift_up(tile, nxt, k):
    if k == 0:
        return tile
    T = tile.shape[0]
    hr = nxt.shape[0]
    rolled = pltpu.roll(tile, T - k, 0)
    rn = pltpu.roll(nxt, hr - k, 0)
    rid = _iota(nxt.shape, 0)
    last = jnp.where(rid >= hr - k, rn, rolled[T - hr:])
    return jnp.concatenate([rolled[:T - hr], last], axis=0)


CONV_CB = 512


def conv_fwd(P, conv_w, conv_b, name):
    S = P.shape[0]
    T = ROW_TILE
    nj = XBC_W // CONV_CB
    cb0 = XBC_OFF // CONV_CB
    hb = T // CONV_HALO

    def body(x_ref, h_ref, w_ref, b_ref, pre_ref, xc_ref):
        i = pl.program_id(0)
        xb = x_ref[...]
        xh = jnp.where(i > 0, h_ref[...], 0.0)
        acc = b_ref[...] + w_ref[3:4, :] * xb
        for idx in range(3):
            acc = acc + w_ref[idx:idx + 1, :] * _shift_down(xb, xh, 3 - idx)
        pre_ref[...] = acc
        xc_ref[...] = _silu(acc)

    blk = pl.BlockSpec((T, CONV_CB), lambda i, j: (i, j))
    return pl.pallas_call(
        body, grid=(S // T, nj),
        in_specs=[pl.BlockSpec((T, CONV_CB), lambda i, j: (i, cb0 + j)),
                  pl.BlockSpec((CONV_HALO, CONV_CB), lambda i, j: (jnp.maximum(i * hb - 1, 0), cb0 + j)),
                  pl.BlockSpec((4, CONV_CB), lambda i, j: (0, j)),
                  pl.BlockSpec((1, CONV_CB), lambda i, j: (0, j))],
        out_specs=[blk, blk], out_shape=[_sds((S, XBC_W), F32), _sds((S, XBC_W), F32)], name=name,
        compiler_params=_cp(("arbitrary", "arbitrary")))(P, P, conv_w, conv_b)


def conv_bwd(dxc, pre, P, dP, conv_w, name):
    S = P.shape[0]
    T = ROW_TILE
    nt = S // T
    nj = XBC_W // CONV_CB
    cb0 = XBC_OFF // CONV_CB
    hb = T // CONV_HALO

    def body(d_ref, dn_ref, p_ref, pn_ref, x_ref, xh_ref, w_ref, dp_in, dp_ref, dw_ref, db_ref):
        i = pl.program_id(1)
        _zero_at_first(i, dw_ref, db_ref)
        dpre = d_ref[...] * _dsilu(p_ref[...])
        dpn = jnp.where(i < nt - 1, dn_ref[...] * _dsilu(pn_ref[...]), 0.0)
        db_ref[...] += jnp.sum(dpre, axis=0, keepdims=True)
        xb = x_ref[...]
        xh = jnp.where(i > 0, xh_ref[...], 0.0)
        dx = jnp.zeros_like(dpre)
        for idx in range(4):
            k = 3 - idx
            dw_ref[idx:idx + 1, :] += jnp.sum(dpre * _shift_down(xb, xh, k), axis=0, keepdims=True)
            dx = dx + w_ref[idx:idx + 1, :] * _shift_up(dpre, dpn, k)
        dp_ref[...] = dx.astype(BF16)

    cur = pl.BlockSpec((T, CONV_CB), lambda j, i: (i, j))
    nxt = pl.BlockSpec((CONV_HALO, CONV_CB), lambda j, i: (jnp.minimum((i + 1) * hb, S // CONV_HALO - 1), j))
    return pl.pallas_call(
        body, grid=(nj, nt),
        in_specs=[cur, nxt, cur, nxt,
                  pl.BlockSpec((T, CONV_CB), lambda j, i: (i, cb0 + j)),
                  pl.BlockSpec((CONV_HALO, CONV_CB), lambda j, i: (jnp.maximum(i * hb - 1, 0), cb0 + j)),
                  pl.BlockSpec((4, CONV_CB), lambda j, i: (0, j)), _any()],
        out_specs=[pl.BlockSpec((T, CONV_CB), lambda j, i: (i, cb0 + j)),
                   pl.BlockSpec((4, CONV_CB), lambda j, i: (0, j)),
                   pl.BlockSpec((1, CONV_CB), lambda j, i: (0, j))],
        out_shape=[_sds(dP.shape, BF16), _sds((4, XBC_W), F32), _sds((1, XBC_W), F32)],
        input_output_aliases={7: 0}, name=name,
        compiler_params=_cp(("arbitrary", "arbitrary")))(dxc, dxc, pre, pre, P, P, conv_w, dP)


def _col(v, h, lane):
    return jnp.sum(jnp.where(lane == h, v, 0.0), axis=1, keepdims=True)


def _row(v, h, rowi):
    return jnp.sum(jnp.where(rowi == h, v, 0.0), axis=0, keepdims=True)


def _ssd_common(dt_ref, dtb_ref, al_ref):
    L = SSD_CHUNK
    lane = _iota((L, L), 1)
    rowi = _iota((L, L), 0)
    dtr = dt_ref[...] + dtb_ref[...]
    dt = _softplus(dtr)
    A = -jnp.exp(al_ref[...])
    a = dt * A
    tril = (rowi >= lane).astype(F32)
    ac = jnp.dot(tril, a, precision=HIGHEST, preferred_element_type=F32)
    acT = ac.T
    last = _row(ac, L - 1, rowi)
    return dict(lane=lane, rowi=rowi, dtr=dtr, dt=dt, A=A, ac=ac, acT=acT, last=last,
                eac=jnp.exp(ac), fac=jnp.exp(last - ac), cd=jnp.exp(last),
                lo=lane < SSD_HEAD_DIM, causal=rowi >= lane)


def _pair(v, h0, q):
    lane = q["lane"][:v.shape[0]]
    lo = q["lo"][:v.shape[0]]
    return jnp.where(lo, _col(v, h0, lane), _col(v, h0 + 1, lane))


def _decay(q, h):
    seg = _col(q["ac"], h, q["lane"]) - _row(q["acT"], h, q["rowi"])
    return jnp.exp(jnp.where(q["causal"], seg, NEG))


def ssd_fwd(xc, P, dtb, alog, dsk, name):
    S = xc.shape[0]
    L = SSD_CHUNK
    nc = S // L

    def body(xs_ref, b_ref, c_ref, dt_ref, dtb_ref, al_ref, dk_ref, y_ref, st_ref, H):
        @pl.when(pl.program_id(0) == 0)
        def _():
            H[...] = jnp.zeros_like(H)

        st_ref[0] = H[...]
        q = _ssd_common(dt_ref, dtb_ref, al_ref)
        dk = dk_ref[...]
        for g in range(2):
            Bg = b_ref[:, g * L:(g + 1) * L].astype(BF16)
            Cg = c_ref[:, g * L:(g + 1) * L].astype(BF16)
            cb = _nt(Cg, Bg)
            for jj in range(4):
                j = g * 4 + jj
                h0 = 2 * j
                sl = slice(j * L, (j + 1) * L)
                xs_p = xs_ref[:, sl]
                X = xs_p * _pair(q["dt"], h0, q)
                Xb = X.astype(BF16)
                G0 = (cb * _decay(q, h0)).astype(BF16)
                G1 = (cb * _decay(q, h0 + 1)).astype(BF16)
                Yd = jnp.where(q["lo"], _nn(G0, Xb), _nn(G1, Xb))
                Hp = H[:, sl]
                Yo = _nn(Cg, Hp.astype(BF16)) * _pair(q["eac"], h0, q)
                y_ref[:, sl] = Yd + Yo + _pair(dk, h0, q) * xs_p
                Xd = (X * _pair(q["fac"], h0, q)).astype(BF16)
                H[:, sl] = _pair(q["cd"], h0, q) * Hp + _tn(Bg, Xd)

    v = _vec((1, LANES))
    return pl.pallas_call(
        body, grid=(nc,),
        in_specs=[pl.BlockSpec((L, 1024), lambda c: (c, 0)),
                  pl.BlockSpec((L, 256), lambda c: (c, 4)),
                  pl.BlockSpec((L, 256), lambda c: (c, 5)),
                  pl.BlockSpec((L, LANES), lambda c: (c, DT_OFF // LANES)), v, v, v],
        out_specs=[pl.BlockSpec((L, 1024), lambda c: (c, 0)),
                   pl.BlockSpec((1, SSD_STATE, 1024), lambda c: (c, 0, 0))],
        out_shape=[_sds((S, 1024), F32), _sds((nc, SSD_STATE, 1024), F32)],
        scratch_shapes=[pltpu.VMEM((SSD_STATE, 1024), F32)], name=name,
        compiler_params=_cp(("arbitrary",)))(xc, xc, xc, P, dtb, alog, dsk)


def ssd_bwd(xc, P, st, dy, dP, dtb, alog, dsk, name):
    S = xc.shape[0]
    L = SSD_CHUNK
    nc = S // L

    def body(xs_ref, b_ref, c_ref, dt_ref, st_ref, dy_ref, dtb_ref, al_ref, dk_ref, dp_in,
             dxc_ref, dp_ref, ddtb_ref, dal_ref, dD_ref, dH):
        step = pl.program_id(0)

        @pl.when(step == 0)
        def _():
            dH[...] = jnp.zeros_like(dH)

        _zero_at_first(step, ddtb_ref, dal_ref, dD_ref)
        q = _ssd_common(dt_ref, dtb_ref, al_ref)
        lane, rowi, lo = q["lane"], q["rowi"], q["lo"]
        lane1 = lane[:1]
        lo1 = lo[:1]
        dk = dk_ref[...]
        dac = jnp.zeros((L, L), F32)
        dacT = jnp.zeros((L, L), F32)
        dlast = jnp.zeros((1, L), F32)
        ddt = jnp.zeros((L, L), F32)

        def put(h0, s0, s1):
            return jnp.where(lane == h0, s0, 0.0) + jnp.where(lane == h0 + 1, s1, 0.0)

        def put1(h0, s0, s1):
            return jnp.where(lane1 == h0, s0, 0.0) + jnp.where(lane1 == h0 + 1, s1, 0.0)

        def hsum(v):
            return (jnp.sum(jnp.where(lo, v, 0.0), axis=1, keepdims=True),
                    jnp.sum(jnp.where(lo, 0.0, v), axis=1, keepdims=True))

        for g in range(2):
            Bg = b_ref[:, g * L:(g + 1) * L].astype(BF16)
            Cg = c_ref[:, g * L:(g + 1) * L].astype(BF16)
            cb = _nt(Cg, Bg)
            dcb = jnp.zeros((L, L), F32)
            dCg = jnp.zeros((L, L), F32)
            dBg = jnp.zeros((L, L), F32)
            for jj in range(4):
                j = g * 4 + jj
                h0 = 2 * j
                sl = slice(j * L, (j + 1) * L)
                xs_p = xs_ref[:, sl]
                dY = dy_ref[:, sl]
                dYb = dY.astype(BF16)
                dtp = _pair(q["dt"], h0, q)
                X = xs_p * dtp
                e_p = _pair(q["eac"], h0, q)
                f_p = _pair(q["fac"], h0, q)
                cd_p = _pair(q["cd"], h0, q)
                Hp = st_ref[0, :, sl]
                Hpb = Hp.astype(BF16)
                dHn = dH[:, sl]
                dHnb = dHn.astype(BF16)
                dD_ref[:, sl] += jnp.sum(dY * xs_p, axis=0, keepdims=True)
                dxs = _pair(dk, h0, q) * dY
                Yo = _nn(Cg, Hpb) * e_p
                dZb = (dY * e_p).astype(BF16)
                s0, s1 = hsum(dY * Yo)
                dac = dac + put(h0, s0, s1)
                dCg = dCg + _nt(dZb, Hpb)
                dHp = _tn(Cg, dZb) + cd_p * dHn
                t = jnp.sum(dHn * Hp, axis=0, keepdims=True)
                d0 = jnp.sum(jnp.where(lo1, t, 0.0), axis=1, keepdims=True)
                d1 = jnp.sum(jnp.where(lo1, 0.0, t), axis=1, keepdims=True)
                dlast = dlast + put1(h0, d0, d1) * q["cd"]
                Xd = X * f_p
                dXd = _nn(Bg, dHnb)
                dBg = dBg + _nt(Xd.astype(BF16), dHnb)
                dX = dXd * f_p
                s0, s1 = hsum(dXd * Xd)
                dac = dac - put(h0, s0, s1)
                dlast = dlast + put1(h0, jnp.sum(s0, axis=0, keepdims=True), jnp.sum(s1, axis=0, keepdims=True))
                for hh, mask in ((h0, lo), (h0 + 1, jnp.logical_not(lo))):
                    dec = _decay(q, hh)
                    Gm = cb * dec
                    Xh = jnp.where(mask, X, 0.0).astype(BF16)
                    dG = _nt(dYb, Xh)
                    dcb = dcb + dG * dec
                    Q = dG * Gm
                    dac = dac + jnp.where(lane == hh, jnp.sum(Q, axis=1, keepdims=True), 0.0)
                    dacT = dacT - jnp.where(rowi == hh, jnp.sum(Q, axis=0, keepdims=True), 0.0)
                    dX = dX + jnp.where(mask, _tn(Gm.astype(BF16), dYb), 0.0)
                dxs = dxs + dX * dtp
                s0, s1 = hsum(dX * xs_p)
                ddt = ddt + put(h0, s0, s1)
                dxc_ref[:, sl] = dxs
                dH[:, sl] = dHp
            dcbb = dcb.astype(BF16)
            dxc_ref[:, 1024 + g * L:1024 + (g + 1) * L] = dBg + _tn(dcbb, Cg)
            dxc_ref[:, 1280 + g * L:1280 + (g + 1) * L] = dCg + _nn(dcbb, Bg)

        dac = dac + dacT.T + jnp.where(rowi == L - 1, dlast, 0.0)
        triu = (rowi <= lane).astype(F32)
        da = jnp.dot(triu, dac, precision=HIGHEST, preferred_element_type=F32)
        ddt = ddt + da * q["A"]
        dal_ref[...] += jnp.sum(da * q["dt"], axis=0, keepdims=True) * q["A"]
        ddtr = jnp.where(lane < SSD_HEADS, ddt * _sigmoid(q["dtr"]), 0.0)
        ddtb_ref[...] += jnp.sum(ddtr, axis=0, keepdims=True)
        dp_ref[...] = jnp.concatenate([ddtr, jnp.zeros_like(ddtr)], axis=1).astype(BF16)

    v = _vec((1, LANES))
    rev = lambda c: nc - 1 - c
    return pl.pallas_call(
        body, grid=(nc,),
        in_specs=[pl.BlockSpec((L, 1024), lambda c: (rev(c), 0)),
                  pl.BlockSpec((L, 256), lambda c: (rev(c), 4)),
                  pl.BlockSpec((L, 256), lambda c: (rev(c), 5)),
                  pl.BlockSpec((L, LANES), lambda c: (rev(c), DT_OFF // LANES)),
                  pl.BlockSpec((1, SSD_STATE, 1024), lambda c: (rev(c), 0, 0)),
                  pl.BlockSpec((L, 1024), lambda c: (rev(c), 0)), v, v, v, _any()],
        out_specs=[pl.BlockSpec((L, XBC_W), lambda c: (rev(c), 0)),
                   pl.BlockSpec((L, DT_W), lambda c: (rev(c), DT_OFF // DT_W)),
                   v, v, _vec((1, 1024))],
        out_shape=[_sds((S, XBC_W), F32), _sds(dP.shape, BF16), _sds((1, LANES), F32), _sds((1, LANES), F32),
                   _sds((1, 1024), F32)],
        scratch_shapes=[pltpu.VMEM((SSD_STATE, 1024), F32)], input_output_aliases={9: 1}, name=name,
        compiler_params=_cp(("arbitrary",)))(xc, xc, xc, P, st, dy, dtb, alog, dsk, dP)


def ssd_post_fwd(y, P, nw, name):
    S = y.shape[0]
    T = ROW_TILE
    HW = 512

    def body(y_ref, z_ref, w_ref, o_ref):
        g = y_ref[...] * _silu(z_ref[...])
        for k in range(2):
            gk = g[:, k * HW:(k + 1) * HW]
            r = lax.rsqrt(jnp.mean(gk * gk, axis=-1, keepdims=True) + EPS)
            o_ref[:, k * HW:(k + 1) * HW] = (gk * r * w_ref[:, k * HW:(k + 1) * HW]).astype(BF16)

    return _rowcall(body, S, T, [_rows(T, 1024), _rows(T, 1024, Z_OFF // 1024), _vec((1, 1024))], _rows(T, 1024),
                    _sds((S, 1024), BF16), name)(y, P, nw)


def ssd_post_bwd(y, P, dout, dP, nw, name):
    S = y.shape[0]
    T = ROW_TILE
    HW = 512

    def body(y_ref, z_ref, d_ref, w_ref, dp_in, dy_ref, dz_ref, dw_ref):
        _zero_at_first(pl.program_id(0), dw_ref)
        yv = y_ref[...]
        zv = z_ref[...]
        sz = _silu(zv)
        g = yv * sz
        dv = d_ref[...]
        dgs = []
        for k in range(2):
            sl = slice(k * HW, (k + 1) * HW)
            gk = g[:, sl]
            r = lax.rsqrt(jnp.mean(gk * gk, axis=-1, keepdims=True) + EPS)
            gh = gk * r
            dk = dv[:, sl]
            dw_ref[:, sl] += jnp.sum(dk * gh, axis=0, keepdims=True)
            dgn = dk * w_ref[:, sl]
            dgs.append(r * (dgn - gh * jnp.mean(dgn * gh, axis=-1, keepdims=True)))
        dg = jnp.concatenate(dgs, axis=1)
        dy_ref[...] = dg * sz
        dz_ref[...] = (dg * yv * _dsilu(zv)).astype(BF16)

    zc = Z_OFF // 1024
    return _rowcall(body, S, T, [_rows(T, 1024), _rows(T, 1024, zc), _rows(T, 1024), _vec((1, 1024)), _any()],
                    [_rows(T, 1024), _rows(T, 1024, zc), _vec((1, 1024))],
                    [_sds((S, 1024), F32), _sds(dP.shape, BF16), _sds((1, 1024), F32)], name,
                    aliases={4: 1})(y, P, dout, nw, dP)


HEAD_W = 3 * ATTN_HEAD_DIM
HEAD_BLOCKS = NP // HEAD_W


def _slope(gi, h):
    c = [float(v) for v in SLOPES[gi]]
    return jnp.where(h == 0, c[0], jnp.where(h == 1, c[1], jnp.where(h == 2, c[2], c[3])))


def _rmsn(t):
    r = lax.rsqrt(jnp.mean(t * t, axis=-1, keepdims=True) + EPS)
    return t * r, r


def _rms_bwd(th, r, w, dn):
    dth = dn * w
    return r * (dth - th * jnp.mean(dth * th, axis=-1, keepdims=True))


def attn_fwd(P, qw, kw, gi, name):
    S = P.shape[0]
    d = DILATIONS[gi]
    n = S // d
    B = ATTN_STEPS
    nb = n // B
    E = ATTN_HEAD_DIM
    scale = E ** -0.5
    Pv = P.reshape(n, d * NP)

    def body(cur_ref, prev_ref, qw_ref, kw_ref, o_ref, l_ref):
        h = pl.program_id(1)
        b = pl.program_id(2)
        slope = _slope(gi, h) * float(d)
        qh, _ = _rmsn(cur_ref[:, 0:E])
        kh, _ = _rmsn(cur_ref[:, E:2 * E])
        kph, _ = _rmsn(prev_ref[:, E:2 * E])
        qn = (qh * qw_ref[...]).astype(BF16)
        kn = (kh * kw_ref[...]).astype(BF16)
        kpn = (kph * kw_ref[...]).astype(BF16)
        v = cur_ref[:, 2 * E:].astype(BF16)
        vp = prev_ref[:, 2 * E:].astype(BF16)
        qi = _iota((B, B), 0)
        kj = _iota((B, B), 1)
        rel_c = (qi - kj).astype(F32)
        rel_p = rel_c + float(B)
        s_c = jnp.where(qi >= kj, _nt(qn, kn) * scale - slope * rel_c, NEG)
        s_p = jnp.where(jnp.logical_and(qi <= kj, b > 0), _nt(qn, kpn) * scale - slope * rel_p, NEG)
        m = jnp.maximum(jnp.max(s_c, axis=1, keepdims=True), jnp.max(s_p, axis=1, keepdims=True))
        l = jnp.sum(jnp.exp(s_c - m), axis=1, keepdims=True) + jnp.sum(jnp.exp(s_p - m), axis=1, keepdims=True)
        lse = m + jnp.log(l)
        p_c = jnp.exp(s_c - lse).astype(BF16)
        p_p = jnp.exp(s_p - lse).astype(BF16)
        o_ref[...] = _nn(p_c, v) + _nn(p_p, vp)
        l_ref[...] = jnp.broadcast_to(lse, (B, E))

    col = lambda r, h: r * HEAD_BLOCKS + gi * 4 + h
    ob = pl.BlockSpec((B, E), lambda r, h, b: (b, r * 4 + h))
    o, l = pl.pallas_call(
        body, grid=(d, 4, nb),
        in_specs=[pl.BlockSpec((B, HEAD_W), lambda r, h, b: (b, col(r, h))),
                  pl.BlockSpec((B, HEAD_W), lambda r, h, b: (jnp.maximum(b - 1, 0), col(r, h))),
                  pl.BlockSpec((1, E), lambda r, h, b: (0, 0)), pl.BlockSpec((1, E), lambda r, h, b: (0, 0))],
        out_specs=[ob, ob], out_shape=[_sds((n, d * 512), F32), _sds((n, d * 512), F32)], name=name,
        compiler_params=_cp(("arbitrary", "arbitrary", "arbitrary")))(Pv, Pv, qw, kw)
    return o.reshape(S, 512), l.reshape(S, 512)


def attn_bwd(P, o, lse, do, dlse, dP, qw, kw, gi, name):
    S = P.shape[0]
    d = DILATIONS[gi]
    n = S // d
    B = ATTN_STEPS
    nb = n // B
    E = ATTN_HEAD_DIM
    scale = E ** -0.5
    Pv = P.reshape(n, d * NP)
    dPv = dP.reshape(n, d * NP)
    view = lambda t: t.reshape(n, d * 512)

    def body(cur_ref, prev_ref, nxt_ref, o_ref, on_ref, l_ref, ln_ref, do_ref, don_ref, dl_ref, dln_ref,
             qw_ref, kw_ref, dp_in, dp_ref, dqw_ref, dkw_ref):
        r_id = pl.program_id(0)
        h = pl.program_id(1)
        b = pl.program_id(2)

        @pl.when(jnp.logical_and(jnp.logical_and(r_id == 0, h == 0), b == 0))
        def _():
            dqw_ref[...] = jnp.zeros_like(dqw_ref)
            dkw_ref[...] = jnp.zeros_like(dkw_ref)

        slope = _slope(gi, h) * float(d)
        qwv = qw_ref[...]
        kwv = kw_ref[...]
        qh, rq = _rmsn(cur_ref[:, 0:E])
        kh, rk = _rmsn(cur_ref[:, E:2 * E])
        kph, _ = _rmsn(prev_ref[:, E:2 * E])
        qxh, _ = _rmsn(nxt_ref[:, 0:E])
        qn = (qh * qwv).astype(BF16)
        kn = (kh * kwv).astype(BF16)
        kpn = (kph * kwv).astype(BF16)
        qxn = (qxh * qwv).astype(BF16)
        v = cur_ref[:, 2 * E:].astype(BF16)
        vp = prev_ref[:, 2 * E:].astype(BF16)
        qi = _iota((B, B), 0)
        kj = _iota((B, B), 1)
        rel_c = (qi - kj).astype(F32)
        rel_p = rel_c + float(B)
        band = qi <= kj
        lse_c = l_ref[...]
        lse_n = ln_ref[...]
        pA = jnp.where(qi >= kj, jnp.exp(_nt(qn, kn) * scale - slope * rel_c - lse_c), 0.0)
        pB = jnp.where(jnp.logical_and(band, b > 0), jnp.exp(_nt(qn, kpn) * scale - slope * rel_p - lse_c), 0.0)
        pC = jnp.where(jnp.logical_and(band, b < nb - 1), jnp.exp(_nt(qxn, kn) * scale - slope * rel_p - lse_n), 0.0)
        doc = do_ref[...]
        don = don_ref[...]
        docb = doc.astype(BF16)
        donb = don.astype(BF16)
        corr_c = jnp.sum(dl_ref[...] - doc * o_ref[...], axis=1, keepdims=True)
        corr_n = jnp.sum(dln_ref[...] - don * on_ref[...], axis=1, keepdims=True)
        dsA = (pA * (_nt(docb, v) + corr_c)).astype(BF16)
        dsB = (pB * (_nt(docb, vp) + corr_c)).astype(BF16)
        dsC = (pC * (_nt(donb, v) + corr_n)).astype(BF16)
        dqn = (_nn(dsA, kn) + _nn(dsB, kpn)) * scale
        dkn = (_tn(dsA, qn) + _tn(dsC, qxn)) * scale
        dv = _tn(pA.astype(BF16), docb) + _tn(pC.astype(BF16), donb)
        dqw_ref[...] += jnp.sum(dqn * qh, axis=0, keepdims=True)
        dkw_ref[...] += jnp.sum(dkn * kh, axis=0, keepdims=True)
        dq = _rms_bwd(qh, rq, qwv, dqn)
        dk = _rms_bwd(kh, rk, kwv, dkn)
        dp_ref[...] = jnp.concatenate([dq, dk, dv], axis=1).astype(BF16)

    col = lambda r, h: r * HEAD_BLOCKS + gi * 4 + h
    pcur = pl.BlockSpec((B, HEAD_W), lambda r, h, b: (b, col(r, h)))
    pprev = pl.BlockSpec((B, HEAD_W), lambda r, h, b: (jnp.maximum(b - 1, 0), col(r, h)))
    pnxt = pl.BlockSpec((B, HEAD_W), lambda r, h, b: (jnp.minimum(b + 1, nb - 1), col(r, h)))
    ocur = pl.BlockSpec((B, E), lambda r, h, b: (b, r * 4 + h))
    onxt = pl.BlockSpec((B, E), lambda r, h, b: (jnp.minimum(b + 1, nb - 1), r * 4 + h))
    wv = pl.BlockSpec((1, E), lambda r, h, b: (0, 0))
    dPn, dqw, dkw = pl.pallas_call(
        body, grid=(d, 4, nb),
        in_specs=[pcur, pprev, pnxt, ocur, onxt, ocur, onxt, ocur, onxt, ocur, onxt, wv, wv, _any()],
        out_specs=[pcur, wv, wv],
        out_shape=[_sds(dPv.shape, BF16), _sds((1, E), F32), _sds((1, E), F32)],
        input_output_aliases={13: 0}, name=name,
        compiler_params=_cp(("arbitrary", "arbitrary", "arbitrary")))(
            Pv, Pv, Pv, view(o), view(o), view(lse), view(lse), view(do), view(do), view(dlse), view(dlse), qw, kw, dPv)
    return dPn.reshape(dP.shape), dqw, dkw


def attn_merge_fwd(os_, ls_, name):
    S = os_[0].shape[0]
    T = ROW_TILE

    def body(o0, o1, o2, l0, l1, l2, out):
        a, b, c = l0[...], l1[...], l2[...]
        m = jnp.maximum(jnp.maximum(a, b), c)
        ea, eb, ec = jnp.exp(a - m), jnp.exp(b - m), jnp.exp(c - m)
        inv = 1.0 / (ea + eb + ec)
        out[...] = ((ea * o0[...] + eb * o1[...] + ec * o2[...]) * inv).astype(BF16)

    return _rowcall(body, S, T, [_rows(T, 512)] * 6, _rows(T, 512), _sds((S, 512), BF16), name)(*os_, *ls_)


def attn_merge_bwd(os_, ls_, dom, name):
    S = os_[0].shape[0]
    T = ROW_TILE

    def body(o0, o1, o2, l0, l1, l2, d_ref, do0, do1, do2, dl0, dl1, dl2):
        a, b, c = l0[...], l1[...], l2[...]
        m = jnp.maximum(jnp.maximum(a, b), c)
        ea, eb, ec = jnp.exp(a - m), jnp.exp(b - m), jnp.exp(c - m)
        inv = 1.0 / (ea + eb + ec)
        wa, wb, wc = ea * inv, eb * inv, ec * inv
        dv = d_ref[...]
        do0[...] = wa * dv
        do1[...] = wb * dv
        do2[...] = wc * dv
        ga, gb, gc = dv * o0[...], dv * o1[...], dv * o2[...]
        tot = wa * ga + wb * gb + wc * gc
        dl0[...] = wa * (ga - tot)
        dl1[...] = wb * (gb - tot)
        dl2[...] = wc * (gc - tot)

    sp = _rows(T, 512)
    return _rowcall(body, S, T, [sp] * 7, [sp] * 6, [_sds((S, 512), F32)] * 6, name)(*os_, *ls_, dom)


def _counts(i, T, rows, w, offset=0):
    t = i * T + offset + _iota((rows, 1), 0)
    return jnp.minimum(t + 1, w).astype(F32)


def pool_fwd(P, wmix, scale, name):
    S = P.shape[0]
    T = ROW_TILE
    GW = 256
    uc = U_OFF // 1024
    hb = T // POOL_HALO

    def body(u_ref, h_ref, wm_ref, sc_ref, pl_ref, yp_ref):
        i = pl.program_id(0)
        u = u_ref[...]
        ext = jnp.concatenate([jnp.where(i > 0, h_ref[...], 0.0), u], axis=0)
        for g, w in enumerate(POOL_WINDOWS):
            sl = slice(g * GW, (g + 1) * GW)
            s = ext[:, sl]
            k = 1
            while k < w:
                s = s + pltpu.roll(s, k, 0)
                k *= 2
            pooled = (s[POOL_HALO:] / _counts(i, T, T, w) - u[:, sl]).astype(BF16)
            pl_ref[:, sl] = pooled
            yp_ref[:, sl] = (_nn(pooled, wm_ref[g]) * sc_ref[:, sl]).astype(BF16)

    return _rowcall(
        body, S, T,
        [_rows(T, 1024, uc), pl.BlockSpec((POOL_HALO, 1024), lambda i: (jnp.maximum(i * hb - 1, 0), uc)),
         _vec((4, GW, GW)), _vec((1, 1024))],
        [_rows(T, 1024), _rows(T, 1024)], [_sds((S, 1024), BF16), _sds((S, 1024), BF16)], name)(P, P, wmix, scale)


def pool_bwd(dyp, pooled, dP, wmix, scale, name):
    S = dyp.shape[0]
    T = ROW_TILE
    nt = S // T
    GW = 256
    uc = U_OFF // 1024
    hb = T // POOL_HALO
    TE = T + POOL_HALO

    def body(d_ref, dn_ref, p_ref, wm_ref, sc_ref, dp_in, du_ref, dwm_ref, dsc_ref):
        i = pl.program_id(0)
        _zero_at_first(i, dwm_ref, dsc_ref)
        dv = d_ref[...]
        dn = jnp.where(i < nt - 1, dn_ref[...], 0.0)
        for g, w in enumerate(POOL_WINDOWS):
            sl = slice(g * GW, (g + 1) * GW)
            pg = p_ref[:, sl]
            wm = wm_ref[g]
            dsc_ref[:, sl] += jnp.sum(dv[:, sl] * _nn(pg, wm), axis=0, keepdims=True)
            dmb = (dv[:, sl] * sc_ref[:, sl]).astype(BF16)
            dmnb = (dn[:, sl] * sc_ref[:, sl]).astype(BF16)
            dwm_ref[g] += _tn(pg, dmb)
            dpl = _nt(dmb, wm)
            dpln = _nt(dmnb, wm)
            s = jnp.concatenate([dpl / _counts(i, T, T, w), dpln / _counts(i, T, POOL_HALO, w, T)], axis=0)
            k = 1
            while k < w:
                s = s + pltpu.roll(s, TE - k, 0)
                k *= 2
            du_ref[:, sl] = (s[:T] - dpl).astype(BF16)

    return _rowcall(
        body, S, T,
        [_rows(T, 1024), pl.BlockSpec((POOL_HALO, 1024), lambda i: (jnp.minimum((i + 1) * hb, S // POOL_HALO - 1), 0)),
         _rows(T, 1024), _vec((4, GW, GW)), _vec((1, 1024)), _any()],
        [_rows(T, 1024, uc), _vec((4, GW, GW)), _vec((1, 1024))],
        [_sds(dP.shape, BF16), _sds((4, GW, GW), F32), _sds((1, 1024), F32)], name,
        aliases={5: 0})(dyp, dyp, pooled, wmix, scale, dP)


def gate_fwd(P, ys, ya, yp, name):
    S = ys.shape[0]
    T = ROW_TILE
    gc = GATE_OFF // 1024

    def body(g0, g1, g2, a, b, c, out):
        out[...] = (_sigmoid(g0[...]) * a[...] + _sigmoid(g1[...]) * b[...] + _sigmoid(g2[...]) * c[...]).astype(BF16)

    r = _rows(T, 1024)
    return _rowcall(body, S, T, [_rows(T, 1024, gc), _rows(T, 1024, gc + 1), _rows(T, 1024, gc + 2), r, r, r], r,
                    _sds((S, 1024), BF16), name)(P, P, P, ys, ya, yp)


def gate_bwd(P, ys, ya, yp, dm, dP, name):
    S = ys.shape[0]
    T = ROW_TILE
    gc = GATE_OFF // 1024

    def body(g0, g1, g2, a, b, c, d_ref, dp_in, dg_ref, da, db, dc):
        dv = d_ref[...]
        outs = []
        for gr, yr, dyr in ((g0, a, da), (g1, b, db), (g2, c, dc)):
            s = _sigmoid(gr[...])
            dyr[...] = (s * dv).astype(BF16)
            outs.append(dv * yr[...] * s * (1.0 - s))
        dg_ref[...] = jnp.concatenate(outs, axis=1).astype(BF16)

    r = _rows(T, 1024)
    return _rowcall(body, S, T, [_rows(T, 1024, gc), _rows(T, 1024, gc + 1), _rows(T, 1024, gc + 2), r, r, r, r, _any()],
                    [_rows(T, GATE_W, GATE_OFF // GATE_W), r, r, r],
                    [_sds(dP.shape, BF16)] + [_sds((S, 1024), BF16)] * 3, name,
                    aliases={7: 0})(P, P, P, ys, ya, yp, dm, dP)


def resid_fwd(x, g, y, name):
    S, D = x.shape
    T = ROW_TILE

    def body(x_ref, g_ref, y_ref, o_ref):
        o_ref[...] = x_ref[...] + g_ref[...] * y_ref[...]

    r = _rows(T, D)
    return _rowcall(body, S, T, [r, _vec((1, D)), r], r, _sds((S, D), F32), name)(x, g, y)


def resid_bwd(dx, g, y, name):
    S, D = dx.shape
    T = ROW_TILE

    def body(d_ref, g_ref, y_ref, dy_ref, dg_ref):
        _zero_at_first(pl.program_id(0), dg_ref)
        dv = d_ref[...]
        dy_ref[...] = (g_ref[...] * dv).astype(BF16)
        dg_ref[...] += jnp.sum(dv * y_ref[...], axis=0, keepdims=True)

    r = _rows(T, D)
    return _rowcall(body, S, T, [r, _vec((1, D)), r], [r, _vec((1, D))], [_sds((S, D), BF16), _sds((1, D), F32)],
                    name)(dx, g, y)


def relu2_fwd(f, name):
    S, W = f.shape
    T = ROW_TILE

    def body(f_ref, o_ref):
        r = jnp.maximum(f_ref[...], 0.0)
        o_ref[...] = (r * r).astype(BF16)

    return _rowcall(body, S, T, [_rows(T, W)], _rows(T, W), _sds((S, W), BF16), name)(f)


def relu2_bwd(f, da, name):
    S, W = f.shape
    T = ROW_TILE

    def body(f_ref, d_ref, o_ref):
        o_ref[...] = (2.0 * jnp.maximum(f_ref[...], 0.0) * d_ref[...]).astype(BF16)

    r = _rows(T, W)
    return _rowcall(body, S, T, [r, r], r, _sds((S, W), BF16), name)(f, da)


def loss_fwd_bwd(y, target, name):
    S, D = y.shape
    T = ROW_TILE

    def body(y_ref, t_ref, acc_ref, dy_ref):
        _zero_at_first(pl.program_id(0), acc_ref)
        e = y_ref[...] - t_ref[...]
        acc_ref[...] += jnp.sum(e * e, axis=0, keepdims=True)
        dy_ref[...] = e * (1.0 / D)

    r = _rows(T, D)
    return _rowcall(body, S, T, [r, r], [_vec((1, D)), r], [_sds((1, D), F32), _sds((S, D), F32)], name)(y, target)


def silu_rows(c, name):
    def body(c_ref, o_ref):
        o_ref[...] = _silu(c_ref[...])

    return pl.pallas_call(body, out_shape=_sds(c.shape, F32), name=name)(c)


def adamw(parts, w, m, v, name):
    n, R, C = parts.shape
    tr = R
    while tr * C * 4 > (1 << 20) and tr % 16 == 0:
        tr //= 2
    c1 = 1.0 / (1.0 - ADAM_B1 ** ADAM_STEP)
    c2 = 1.0 / (1.0 - ADAM_B2 ** ADAM_STEP)

    def body(p_ref, w_ref, m_ref, v_ref, g_ref, d_ref, nm_ref, nv_ref):
        g = p_ref[0].astype(F32)
        for k in range(1, n):
            g = g + p_ref[k].astype(F32)
        nm = ADAM_B1 * m_ref[...] + (1.0 - ADAM_B1) * g
        nv = ADAM_B2 * v_ref[...] + (1.0 - ADAM_B2) * (g * g)
        g_ref[...] = g
        nm_ref[...] = nm
        nv_ref[...] = nv
        d_ref[...] = -ADAM_LR * ((nm * c1) / (jnp.sqrt(nv * c2) + ADAM_EPS) + ADAM_WD * w_ref[...])

    r = _rows(tr, C)
    return _rowcall(body, R, tr, [pl.BlockSpec((n, tr, C), lambda i: (0, i, 0)), r, r, r], [r, r, r, r],
                    [_sds((R, C), F32)] * 4, name)(parts, w, m, v)


def _me_and_peers():
    x, y, c = lax.axis_index("x"), lax.axis_index("y"), lax.axis_index("c")
    peers = []
    for k in range(1, N_DEV):
        px = x ^ ((k >> 2) & 1)
        py = y ^ ((k >> 1) & 1)
        pc = c ^ (k & 1)
        peers.append(((px, py, pc), 4 * px + 2 * py + pc))
    return 4 * x + 2 * y + c, peers


def _exchange(arrs, name, scatter):
    n = len(arrs)
    if scatter:
        out_shape = [pltpu.HBM(a.shape, a.dtype) for a in arrs]
    else:
        out_shape = [pltpu.HBM((N_DEV,) + a.shape, a.dtype) for a in arrs]

    def body(*refs):
        ins, outs = refs[:n], refs[n:2 * n]
        send_sems, recv_sems, local_sems = refs[2 * n:]
        me, peers = _me_and_peers()
        started = []
        for a in range(n):
            src_me = ins[a].at[me] if scatter else ins[a]
            loc = pltpu.make_async_copy(src_me, outs[a].at[me], local_sems.at[a])
            loc.start()
            started.append(loc)
            for k, (peer, pidx) in enumerate(peers):
                src = ins[a].at[pidx] if scatter else ins[a]
                cp = pltpu.make_async_remote_copy(
                    src_ref=src, dst_ref=outs[a].at[me], send_sem=send_sems.at[a, k], recv_sem=recv_sems.at[a, k],
                    device_id=peer, device_id_type=MESH)
                cp.start()
                started.append(cp)
        for cp in started:
            cp.wait()

    hbm = pl.BlockSpec(memory_space=pltpu.HBM)
    return pl.pallas_call(
        body, in_specs=[hbm] * n, out_specs=[hbm] * n, out_shape=out_shape,
        scratch_shapes=[pltpu.SemaphoreType.DMA((n, N_DEV - 1)), pltpu.SemaphoreType.DMA((n, N_DEV - 1)),
                        pltpu.SemaphoreType.DMA((n,))],
        name=name, compiler_params=pltpu.CompilerParams(has_side_effects=True))(*arrs)


def all_gather(arrs, name):
    return _exchange(arrs, name, scatter=False)


def all_to_all(arrs, name):
    return _exchange(arrs, name, scatter=True)


def layer_fwd(x, mod, W, sm, tag):
    sh1, sc1, g1, sh2, sc2, g2 = mod
    sv = {}
    h1 = rms_mod_fwd(x, sm["norm1_w"], sc1, sh1, f"rms1_fwd_{tag}")
    P = mm(h1, W["in"], "nn", F32, f"in_proj_{tag}")
    pre, xc = conv_fwd(P, sm["conv_w"], sm["conv_b"], f"conv_fwd_{tag}")
    y, st = ssd_fwd(xc, P, sm["dt_bias"], sm["a_log"], sm["d_skip"], f"ssd_fwd_{tag}")
    ysn = ssd_post_fwd(y, P, sm["ssd_norm_w"], f"ssd_post_fwd_{tag}")
    ys = mm(ysn, W["ssd_out"], "nn", F32, f"ssd_out_{tag}")
    os_, ls_ = [], []
    for gi in range(3):
        o, l = attn_fwd(P, sm["q_norm_w"], sm["k_norm_w"], gi, f"attn_fwd{gi}_{tag}")
        os_.append(o)
        ls_.append(l)
    om = attn_merge_fwd(os_, ls_, f"attn_merge_fwd_{tag}")
    ya = mm(om, W["attn_out"], "nn", F32, f"attn_out_{tag}")
    pooled, ypp = pool_fwd(P, W["pool_mix"], sm["pool_scale"], f"pool_fwd_{tag}")
    yp = mm(ypp, W["pool_out"], "nn", F32, f"pool_out_{tag}")
    m = gate_fwd(P, ys, ya, yp, f"gate_fwd_{tag}")
    mo = mm(m, W["out"], "nn", F32, f"mix_out_{tag}")
    x1 = resid_fwd(x, g1, mo, f"resid1_fwd_{tag}")
    h2 = rms_mod_fwd(x1, sm["norm2_w"], sc2, sh2, f"rms2_fwd_{tag}")
    f = mm(h2, W["ff1"], "nn", F32, f"ff1_{tag}")
    a = relu2_fwd(f, f"relu2_fwd_{tag}")
    o2 = mm(a, W["ff2"], "nn", F32, f"ff2_{tag}")
    x2 = resid_fwd(x1, g2, o2, f"resid2_fwd_{tag}")
    sv.update(x=x, h1=h1, P=P, pre=pre, xc=xc, y=y, st=st, ysn=ysn, ys=ys, os=os_, ls=ls_, om=om, ya=ya,
              pooled=pooled, ypp=ypp, yp=yp, m=m, mo=mo, x1=x1, h2=h2, f=f, a=a, o2=o2)
    return x2, sv


def layer_bwd(dx2, sv, mod, W, sm, tag):
    sh1, sc1, g1, sh2, sc2, g2 = mod
    S = dx2.shape[0]
    P = sv["P"]
    gw, gs = {}, {}
    do2, dg2 = resid_bwd(dx2, g2, sv["o2"], f"resid2_bwd_{tag}")
    gw["ff2"] = mm(sv["a"], do2, "tn", BF16, f"d_w_ff2_{tag}")
    da = mm(do2, W["ff2"], "nt", F32, f"d_a_{tag}")
    df = relu2_bwd(sv["f"], da, f"relu2_bwd_{tag}")
    gw["ff1"] = mm(sv["h2"], df, "tn", BF16, f"d_w_ff1_{tag}")
    dh2 = mm(df, W["ff1"], "nt", F32, f"d_h2_{tag}")
    dx1, gs["norm2_w"], dsc2, dsh2 = rms_mod_bwd(sv["x1"], dh2, dx2, sm["norm2_w"], sc2, f"rms2_bwd_{tag}")
    dmo, dg1 = resid_bwd(dx1, g1, sv["mo"], f"resid1_bwd_{tag}")
    gw["out"] = mm(sv["m"], dmo, "tn", BF16, f"d_w_out_{tag}")
    dm = mm(dmo, W["out"], "nt", F32, f"d_m_{tag}")
    dP = jnp.zeros((S, NP), BF16)
    dP, dys, dya, dyp = gate_bwd(P, sv["ys"], sv["ya"], sv["yp"], dm, dP, f"gate_bwd_{tag}")
    gw["ssd_out"] = mm(sv["ysn"], dys, "tn", BF16, f"d_w_ssd_out_{tag}")
    dysn = mm(dys, W["ssd_out"], "nt", F32, f"d_ysn_{tag}")
    dy, dP, gs["ssd_norm_w"] = ssd_post_bwd(sv["y"], P, dysn, dP, sm["ssd_norm_w"], f"ssd_post_bwd_{tag}")
    dxc, dP, gs["dt_bias"], gs["a_log"], dDl = ssd_bwd(sv["xc"], P, sv["st"], dy, dP, sm["dt_bias"], sm["a_log"],
                                                      sm["d_skip"], f"ssd_bwd_{tag}")
    gs["d_skip"] = dDl.reshape(SSD_HEADS, SSD_HEAD_DIM).sum(axis=1)
    gs["dt_bias"] = gs["dt_bias"][0, :SSD_HEADS]
    gs["a_log"] = gs["a_log"][0, :SSD_HEADS]
    dP, gs["conv_w"], gs["conv_b"] = conv_bwd(dxc, sv["pre"], P, dP, sm["conv_w"], f"conv_bwd_{tag}")
    gw["attn_out"] = mm(sv["om"], dya, "tn", BF16, f"d_w_attn_out_{tag}")
    dom = mm(dya, W["attn_out"], "nt", F32, f"d_om_{tag}")
    mb = attn_merge_bwd(sv["os"], sv["ls"], dom, f"attn_merge_bwd_{tag}")
    dqw = dkw = None
    for gi in range(3):
        dP, a_, b_ = attn_bwd(P, sv["os"][gi], sv["ls"][gi], mb[gi], mb[3 + gi], dP, sm["q_norm_w"], sm["k_norm_w"], gi,
                              f"attn_bwd{gi}_{tag}")
        dqw = a_ if dqw is None else dqw + a_
        dkw = b_ if dkw is None else dkw + b_
    gs["q_norm_w"], gs["k_norm_w"] = dqw, dkw
    gw["pool_out"] = mm(sv["ypp"], dyp, "tn", BF16, f"d_w_pool_out_{tag}")
    dypp = mm(dyp, W["pool_out"], "nt", F32, f"d_ypp_{tag}")
    dP, dwm, gs["pool_scale"] = pool_bwd(dypp, sv["pooled"], dP, W["pool_mix"], sm["pool_scale"], f"pool_bwd_{tag}")
    gw["pool_mix"] = dwm.astype(BF16)
    gw["in"] = mm(sv["h1"], dP, "tn", BF16, f"d_w_in_{tag}")
    dh1 = mm(dP, W["in"], "nt", F32, f"d_h1_{tag}")
    dx, gs["norm1_w"], dsc1, dsh1 = rms_mod_bwd(sv["x"], dh1, dx1, sm["norm1_w"], sc1, f"rms1_bwd_{tag}")
    dmod = jnp.concatenate([dsh1, dsc1, dg1, dsh2, dsc2, dg2], axis=1)
    return dx, dmod, gw, gs


def _w_in_to_fused(w):
    E = ATTN_HEAD_DIM
    cols = []
    for h in range(ATTN_HEADS):
        for off in (O_Q, O_K, O_V):
            cols.append(w[:, off + h * E: off + (h + 1) * E])
    cols += [w[:, O_XBC:O_DT], w[:, O_G:IN_WIDTH], w[:, O_Z:O_XBC], w[:, O_U:O_G], w[:, O_DT:O_Q],
             jnp.zeros((w.shape[0], DT_W - SSD_HEADS), w.dtype)]
    return jnp.concatenate(cols, axis=1)


def _fused_to_w_in(g):
    E = ATTN_HEAD_DIM
    q = [g[:, h * HEAD_W: h * HEAD_W + E] for h in range(ATTN_HEADS)]
    k = [g[:, h * HEAD_W + E: h * HEAD_W + 2 * E] for h in range(ATTN_HEADS)]
    v = [g[:, h * HEAD_W + 2 * E: (h + 1) * HEAD_W] for h in range(ATTN_HEADS)]
    return jnp.concatenate([g[:, Z_OFF:Z_OFF + Z_W], g[:, XBC_OFF:XBC_OFF + XBC_W], g[:, DT_OFF:DT_OFF + SSD_HEADS]]
                           + q + k + v + [g[:, U_OFF:U_OFF + U_W], g[:, GATE_OFF:GATE_OFF + GATE_W]], axis=1)


def _cols_from_shards(g):
    return jnp.moveaxis(g, 0, 1).reshape(g.shape[1], N_DEV * g.shape[2])


def _cols_to_shards(w):
    R = w.shape[0]
    return jnp.moveaxis(w.reshape(R, N_DEV, w.shape[1] // N_DEV), 1, 0)


BIG = ("w_in", "w_ssd_out", "w_attn_out", "w_pool_mix", "w_pool_out", "w_out", "w_ff1", "w_ff2")
SMALL = ("b_ada", "norm1_w", "norm2_w", "conv_b", "dt_bias", "a_log", "d_skip", "ssd_norm_w", "q_norm_w", "k_norm_w",
         "pool_scale")
WEIGHTS = ("w_ada", "b_ada", "norm1_w", "norm2_w", "w_in", "conv_w", "conv_b", "dt_bias", "a_log", "d_skip",
           "ssd_norm_w", "w_ssd_out", "q_norm_w", "k_norm_w", "w_attn_out", "w_pool_mix", "pool_scale", "w_pool_out",
           "w_out", "w_ff1", "w_ff2")


def _pack(arrs):
    flat = jnp.concatenate([a.reshape(-1).astype(F32) for a in arrs])
    pad = (-flat.shape[0]) % (8 * LANES)
    return jnp.pad(flat, (0, pad)).reshape(-1, LANES)


def _unpack(packed, shapes):
    flat = packed.reshape(-1)
    out, off = [], 0
    for s in shapes:
        n = int(np.prod(s))
        out.append(flat[off:off + n].reshape(s))
        off += n
    return out


def _pad_lanes(v):
    return jnp.pad(v, ((0, 0), (0, LANES - v.shape[1])))[:, None, :]


def _step(x, c, target, p, m_, v_):
    Ld = p["w_in"].shape[0]
    me = 4 * lax.axis_index("x") + 2 * lax.axis_index("y") + lax.axis_index("c")

    gathered = all_gather([p[k].astype(BF16) for k in BIG] + [c, p["conv_w"]], "gather_weights")
    G = dict(zip(BIG, gathered[:len(BIG)]))
    c_all = gathered[len(BIG)].reshape(N_DEV, D_MODEL)
    conv_w_full = jnp.moveaxis(gathered[len(BIG) + 1], 0, 2).reshape(Ld, 4, XBC_W)

    def layer_weights(l):
        W = {}
        W["in"] = _w_in_to_fused(_cols_from_shards(G["w_in"][:, l]))
        W["ssd_out"] = G["w_ssd_out"][:, l].reshape(1024, D_MODEL)
        W["attn_out"] = _cols_from_shards(G["w_attn_out"][:, l])
        W["pool_mix"] = jnp.moveaxis(G["w_pool_mix"][:, l], 0, 1).reshape(4, 256, 256)
        W["pool_out"] = G["w_pool_out"][:, l].reshape(1024, D_MODEL)
        W["out"] = G["w_out"][:, l].reshape(D_MODEL, D_MODEL)
        W["ff1"] = _cols_from_shards(G["w_ff1"][:, l])
        W["ff2"] = G["w_ff2"][:, l].reshape(4 * D_MODEL, D_MODEL)
        return W

    def layer_small(l):
        sm = {k: p[k][l][None, :] for k in ("norm1_w", "norm2_w", "conv_b", "ssd_norm_w", "q_norm_w", "k_norm_w",
                                            "pool_scale")}
        for k in ("dt_bias", "a_log", "d_skip"):
            sm[k] = jnp.pad(p[k][l], (0, LANES - SSD_HEADS))[None, :]
        sm["conv_w"] = conv_w_full[l]
        return sm

    cond = silu_rows(c_all, "cond_silu")
    ncol = p["w_ada"].shape[2]
    b_mine = lax.dynamic_slice_in_dim(p["b_ada"], me * ncol, ncol, axis=1)
    parts = [mm(cond, p["w_ada"][l], "nn", F32, f"ada_{l}") + b_mine[l][None, :] for l in range(Ld)]
    mod_cols = jnp.stack(parts, axis=1)
    mod = all_to_all([mod_cols], "scatter_mod")[0]
    mod = jnp.moveaxis(mod, 0, 1).reshape(Ld, 6 * D_MODEL)

    def mods(l):
        return [mod[l, k * D_MODEL:(k + 1) * D_MODEL][None, :] for k in range(6)]

    saved, Ws, sms = [], [], []
    h = x
    for l in range(Ld):
        Ws.append(layer_weights(l))
        sms.append(layer_small(l))
        h, sv = layer_fwd(h, mods(l), Ws[l], sms[l], f"l{l}")
        saved.append(sv)
    lsum, dh = loss_fwd_bwd(h, target, "loss")
    loss = 0.5 / D_MODEL * jnp.sum(lsum)
    gws, gss, dmods = [None] * Ld, [None] * Ld, [None] * Ld
    for l in reversed(range(Ld)):
        dh, dmods[l], gws[l], gss[l] = layer_bwd(dh, saved[l], mods(l), Ws[l], sms[l], f"l{l}")
    dmod = jnp.concatenate(dmods, axis=0)

    def stack(k):
        return jnp.stack([gws[l][k] for l in range(Ld)], axis=1)

    send = {
        "w_in": jnp.stack([_cols_to_shards(_fused_to_w_in(gws[l]["in"])) for l in range(Ld)], axis=1),
        "w_ssd_out": jnp.stack([gws[l]["ssd_out"].reshape(N_DEV, -1, D_MODEL) for l in range(Ld)], axis=1),
        "w_attn_out": jnp.stack([_cols_to_shards(gws[l]["attn_out"]) for l in range(Ld)], axis=1),
        "w_pool_mix": jnp.stack([jnp.moveaxis(gws[l]["pool_mix"].reshape(4, N_DEV, 32, 256), 1, 0) for l in range(Ld)],
                                axis=1),
        "w_pool_out": jnp.stack([gws[l]["pool_out"].reshape(N_DEV, -1, D_MODEL) for l in range(Ld)], axis=1),
        "w_out": jnp.stack([gws[l]["out"].reshape(N_DEV, -1, D_MODEL) for l in range(Ld)], axis=1),
        "w_ff1": jnp.stack([_cols_to_shards(gws[l]["ff1"]) for l in range(Ld)], axis=1),
        "w_ff2": jnp.stack([gws[l]["ff2"].reshape(N_DEV, -1, D_MODEL) for l in range(Ld)], axis=1),
    }
    dmod_cols = jnp.moveaxis(dmod.reshape(Ld, N_DEV, ncol), 1, 0)
    recv = all_to_all([send[k] for k in BIG] + [dmod_cols], "scatter_grads")
    R = dict(zip(BIG, recv[:len(BIG)]))
    dmod_all = recv[len(BIG)]

    out_g, out_d, out_m, out_v = {}, {}, {}, {}

    def finish(k, res):
        shp = p[k].shape
        out_g[k], out_d[k], out_m[k], out_v[k] = [t.reshape(shp) for t in res]

    for k in BIG:
        C = p[k].shape[-1]
        finish(k, adamw(R[k].reshape(N_DEV, -1, C), p[k].reshape(-1, C), m_[k].reshape(-1, C), v_[k].reshape(-1, C),
                        f"adamw_{k}"))
    g_ada = jnp.stack([mm(cond, dmod_all[:, l], "tn", F32, f"d_w_ada_{l}") for l in range(Ld)], axis=0)
    finish("w_ada", adamw(g_ada.reshape(1, -1, ncol), p["w_ada"].reshape(-1, ncol), m_["w_ada"].reshape(-1, ncol),
                          v_["w_ada"].reshape(-1, ncol), "adamw_w_ada"))

    small_g = {k: jnp.stack([gss[l][k].reshape(p[k].shape[1:]) for l in range(Ld)], axis=0) for k in SMALL if k != "b_ada"}
    small_g["b_ada"] = dmod
    conv_g = jnp.stack([gss[l]["conv_w"] for l in range(Ld)], axis=0)
    packed = _pack([small_g[k] for k in SMALL] + [conv_g])
    parts_small = all_gather([packed], "gather_small_grads")[0]
    zeros_conv = jnp.zeros(conv_g.shape, F32)
    res = adamw(parts_small, _pack([p[k] for k in SMALL] + [zeros_conv]), _pack([m_[k] for k in SMALL] + [zeros_conv]),
                _pack([v_[k] for k in SMALL] + [zeros_conv]), "adamw_small")
    shapes = [p[k].shape for k in SMALL] + [conv_g.shape]
    unp = [_unpack(t, shapes) for t in res]
    for i, k in enumerate(SMALL):
        out_g[k], out_d[k], out_m[k], out_v[k] = unp[0][i], unp[1][i], unp[2][i], unp[3][i]
    cw = p["conv_w"].shape[-1]
    conv_mine = lax.dynamic_slice_in_dim(unp[0][len(SMALL)], me * cw, cw, axis=2)
    finish("conv_w", adamw(conv_mine.reshape(1, -1, cw), p["conv_w"].reshape(-1, cw), m_["conv_w"].reshape(-1, cw),
                           v_["conv_w"].reshape(-1, cw), "adamw_conv_w"))

    loss = lax.psum(loss, ("x", "y", "c"))
    return loss, dh, out_g, out_d, out_m, out_v


def kernel(x, c, w_ada, b_ada, norm1_w, norm2_w, w_in, conv_w, conv_b, dt_bias, a_log, d_skip, ssd_norm_w, w_ssd_out, q_norm_w, k_norm_w, w_attn_out, w_pool_mix, pool_scale, w_pool_out, w_out, w_ff1, w_ff2, loss_target, m_w_ada, m_b_ada, m_norm1_w, m_norm2_w, m_w_in, m_conv_w, m_conv_b, m_dt_bias, m_a_log, m_d_skip, m_ssd_norm_w, m_w_ssd_out, m_q_norm_w, m_k_norm_w, m_w_attn_out, m_w_pool_mix, m_pool_scale, m_w_pool_out, m_w_out, m_w_ff1, m_w_ff2, v_w_ada, v_b_ada, v_norm1_w, v_norm2_w, v_w_in, v_conv_w, v_conv_b, v_dt_bias, v_a_log, v_d_skip, v_ssd_norm_w, v_w_ssd_out, v_q_norm_w, v_k_norm_w, v_w_attn_out, v_w_pool_mix, v_pool_scale, v_w_pool_out, v_w_out, v_w_ff1, v_w_ff2):
    p = dict(w_ada=w_ada, b_ada=b_ada, norm1_w=norm1_w, norm2_w=norm2_w, w_in=w_in, conv_w=conv_w, conv_b=conv_b,
             dt_bias=dt_bias, a_log=a_log, d_skip=d_skip, ssd_norm_w=ssd_norm_w, w_ssd_out=w_ssd_out, q_norm_w=q_norm_w,
             k_norm_w=k_norm_w, w_attn_out=w_attn_out, w_pool_mix=w_pool_mix, pool_scale=pool_scale,
             w_pool_out=w_pool_out, w_out=w_out, w_ff1=w_ff1, w_ff2=w_ff2)
    m_ = dict(w_ada=m_w_ada, b_ada=m_b_ada, norm1_w=m_norm1_w, norm2_w=m_norm2_w, w_in=m_w_in, conv_w=m_conv_w,
              conv_b=m_conv_b, dt_bias=m_dt_bias, a_log=m_a_log, d_skip=m_d_skip, ssd_norm_w=m_ssd_norm_w,
              w_ssd_out=m_w_ssd_out, q_norm_w=m_q_norm_w, k_norm_w=m_k_norm_w, w_attn_out=m_w_attn_out,
              w_pool_mix=m_w_pool_mix, pool_scale=m_pool_scale, w_pool_out=m_w_pool_out, w_out=m_w_out, w_ff1=m_w_ff1,
              w_ff2=m_w_ff2)
    v_ = dict(w_ada=v_w_ada, b_ada=v_b_ada, norm1_w=v_norm1_w, norm2_w=v_norm2_w, w_in=v_w_in, conv_w=v_conv_w,
              conv_b=v_conv_b, dt_bias=v_dt_bias, a_log=v_a_log, d_skip=v_d_skip, ssd_norm_w=v_ssd_norm_w,
              w_ssd_out=v_w_ssd_out, q_norm_w=v_q_norm_w, k_norm_w=v_k_norm_w, w_attn_out=v_w_attn_out,
              w_pool_mix=v_w_pool_mix, pool_scale=v_pool_scale, w_pool_out=v_w_pool_out, w_out=v_w_out, w_ff1=v_w_ff1,
              w_ff2=v_w_ff2)
    loss, dx, g, d, nm, nv = _step(x[0], c, loss_target[0], p, m_, v_)
    return (loss, dx[None], *[g[k] for k in WEIGHTS], *[d[k] for k in WEIGHTS], *[nm[k] for k in WEIGHTS],
            *[nv[k] for k in WEIGHTS])
```

```python
import functools
import math

import numpy as np
import jax
import jax.numpy as jnp
from jax import lax
from jax.experimental import pallas as pl
from jax.experimental.pallas import tpu as pltpu

F32 = jnp.float32
BF16 = jnp.bfloat16
MESH = pl.DeviceIdType.MESH
HIGHEST = lax.Precision.HIGHEST

N_DEV = 8
V7X_VMEM_LIMIT_BYTES = 56 * 1024 * 1024
LANES = 128
ROW_TILE = 256

EPS = 1e-6
D_MODEL = 1024
SSD_HEADS = 16
SSD_HEAD_DIM = 64
SSD_CHUNK = 128
SSD_STATE = 128
ATTN_HEADS = 12
ATTN_HEAD_DIM = 128
ATTN_STEPS = 128
DILATIONS = (1, 4, 16)
POOL_WINDOWS = (2, 4, 8, 16)
POOL_HALO = 16
CONV_HALO = 8
NEG = -1e30

ADAM_LR = 0.001
ADAM_B1 = 0.9
ADAM_B2 = 0.999
ADAM_EPS = 1e-08
ADAM_WD = 0.01
ADAM_STEP = 10

QKV_OFF, QKV_W = 0, 4608
XBC_OFF, XBC_W = 4608, 1536
GATE_OFF, GATE_W = 6144, 3072
Z_OFF, Z_W = 9216, 1024
U_OFF, U_W = 10240, 1024
DT_OFF, DT_W = 11264, 256
NP = 11520
IN_WIDTH = 11280
O_Z, O_XBC, O_DT, O_Q, O_K, O_V, O_U, O_G = 0, 1024, 2560, 2576, 4112, 5648, 7184, 8208


def _alibi_slopes(n):
    def pow2(k):
        start = 2.0 ** (-8.0 / k)
        return [start ** (i + 1) for i in range(k)]
    if math.log2(n).is_integer():
        s = pow2(n)
    else:
        c = 2 ** math.floor(math.log2(n))
        s = pow2(c) + pow2(2 * c)[0::2][: n - c]
    return np.sort(np.asarray(s, np.float32))[::-1].copy()


SLOPES = _alibi_slopes(ATTN_HEADS).reshape(3, 4)


def _cp(sem=None):
    return pltpu.CompilerParams(dimension_semantics=sem, vmem_limit_bytes=V7X_VMEM_LIMIT_BYTES)


def _pick(dim, cap):
    if dim <= cap:
        return dim
    t = cap - cap % LANES
    while t >= LANES:
        if dim % t == 0:
            return t
        t -= LANES
    raise ValueError((dim, cap))


def _sigmoid(x):
    return 1.0 / (1.0 + jnp.exp(-x))


def _silu(x):
    return x * _sigmoid(x)


def _dsilu(x):
    s = _sigmoid(x)
    return s * (1.0 + x * (1.0 - s))


def _softplus(x):
    return jnp.maximum(x, 0.0) + jnp.log(1.0 + jnp.exp(-jnp.abs(x)))


def _dot(a, b, dims):
    return lax.dot_general(a, b, (dims, ((), ())), preferred_element_type=F32)


def _nn(a, b):
    return _dot(a, b, ((1,), (0,)))


def _nt(a, b):
    return _dot(a, b, ((1,), (1,)))


def _tn(a, b):
    return _dot(a, b, ((0,), (0,)))


def _iota(shape, axis):
    return lax.broadcasted_iota(jnp.int32, shape, axis)


def _position():
    return lax.axis_index("x"), lax.axis_index("y"), lax.axis_index("c")


def _index(px, py, pc):
    return 4 * px + 2 * py + pc


def _remote(src, dst, send_sems, recv_sems, a, k, to):
    return pltpu.make_async_remote_copy(src_ref=src, dst_ref=dst, send_sem=send_sems.at[a, k], recv_sem=recv_sems.at[a, k],
                                        device_id=to, device_id_type=MESH)


def _gather_plan(srcs, dsts, send_sems, recv_sems, local_sems):
    x, y, c = _position()
    me, sib = (x, y, c), (x, y, 1 - c)
    chips = [(1 - x, y), (x, 1 - y), (1 - x, 1 - y)]
    n = len(srcs)

    def start():
        for a in range(n):
            mine = dsts[a].at[_index(*me)]
            pltpu.make_async_copy(srcs[a], mine, local_sems.at[a]).start()
            _remote(srcs[a], mine, send_sems, recv_sems, a, 0, sib).start()
            for j, chip in enumerate(chips):
                _remote(srcs[a], mine, send_sems, recv_sems, a, 1 + j, (*chip, c)).start()

    def finish():
        for a in range(n):
            for j, chip in enumerate(chips):
                blk = dsts[a].at[_index(*chip, c)]
                _remote(blk, blk, send_sems, recv_sems, a, 1 + j, me).wait_recv()
                _remote(blk, blk, send_sems, recv_sems, a, 4 + j, sib).start()
        for a in range(n):
            mine = dsts[a].at[_index(*me)]
            blk = dsts[a].at[_index(*sib)]
            _remote(blk, blk, send_sems, recv_sems, a, 0, me).wait_recv()
            for j, chip in enumerate(chips):
                blk = dsts[a].at[_index(*chip, 1 - c)]
                _remote(blk, blk, send_sems, recv_sems, a, 4 + j, me).wait_recv()
            for k in range(N_DEV - 1):
                _remote(mine, mine, send_sems, recv_sems, a, k, me).wait_send()
            pltpu.make_async_copy(srcs[a], mine, local_sems.at[a]).wait()

    return start, finish


def _scatter_plan(srcs, dsts, send_sems, recv_sems, local_sems):
    x, y, c = _position()
    me = _index(x, y, c)
    n = len(srcs)
    copies = []
    for a in range(n):
        copies.append(pltpu.make_async_copy(srcs[a].at[me], dsts[a].at[me], local_sems.at[a]))
        for k in range(1, N_DEV):
            px, py, pc = x ^ ((k >> 2) & 1), y ^ ((k >> 1) & 1), c ^ (k & 1)
            copies.append(_remote(srcs[a].at[_index(px, py, pc)], dsts[a].at[me], send_sems, recv_sems, a, k - 1, (px, py, pc)))

    def start():
        for cp in copies:
            cp.start()

    def finish():
        for cp in copies:
            cp.wait()

    return start, finish


class Rider:
    def __init__(self, kind, arrs):
        self.kind, self.arrs = kind, list(arrs)

    def out_shapes(self):
        if self.kind == "gather":
            return [pltpu.HBM((N_DEV,) + a.shape, a.dtype) for a in self.arrs]
        return [pltpu.HBM(a.shape, a.dtype) for a in self.arrs]

    def sems(self):
        n = len(self.arrs)
        return [pltpu.SemaphoreType.DMA((n, N_DEV - 1)), pltpu.SemaphoreType.DMA((n, N_DEV - 1)),
                pltpu.SemaphoreType.DMA((n,))]

    def plan(self, srcs, dsts, sems):
        return (_gather_plan if self.kind == "gather" else _scatter_plan)(srcs, dsts, *sems)


def _exchange(kind, arrs, name):
    rd = Rider(kind, arrs)
    n = len(arrs)

    def body(*refs):
        start, finish = rd.plan(refs[:n], refs[n:2 * n], refs[2 * n:])
        start()
        finish()

    hbm = pl.BlockSpec(memory_space=pltpu.HBM)
    return pl.pallas_call(
        body, in_specs=[hbm] * n, out_specs=[hbm] * n, out_shape=rd.out_shapes(), scratch_shapes=rd.sems(),
        name=name, compiler_params=pltpu.CompilerParams(has_side_effects=True))(*arrs)


def all_gather(arrs, name):
    return _exchange("gather", arrs, name)


def all_to_all(arrs, name):
    return _exchange("scatter", arrs, name)


def mm(a, b, mode, out_dtype, name, rider=None, tm_cap=512, tn_cap=1280, tk_cap=1024):
    if mode == "nn":
        (M, K), (K2, N) = a.shape, b.shape
    elif mode == "nt":
        (M, K), (N, K2) = a.shape, b.shape
    else:
        (K, M), (K2, N) = a.shape, b.shape
    assert K == K2, (a.shape, b.shape, mode)
    tm, tn, tk = _pick(M, tm_cap), _pick(N, tn_cap), _pick(K, tk_cap)
    nk = K // tk
    if mode == "tn":
        a_spec = pl.BlockSpec((tk, tm), lambda i, j, k: (k, i))
    else:
        a_spec = pl.BlockSpec((tm, tk), lambda i, j, k: (i, k))
    if mode == "nt":
        b_spec = pl.BlockSpec((tn, tk), lambda i, j, k: (j, k))
    else:
        b_spec = pl.BlockSpec((tk, tn), lambda i, j, k: (k, j))
    dims = {"nn": ((1,), (0,)), "nt": ((1,), (1,)), "tn": ((0,), (0,))}[mode]
    o_spec = pl.BlockSpec((tm, tn), lambda i, j, k: (i, j))
    grid = (M // tm, N // tn, nk)
    nr = len(rider.arrs) if rider is not None else 0

    def body(*refs):
        a_ref, b_ref = refs[0], refs[1]
        o_ref = refs[2 + nr]
        acc = refs[3 + 2 * nr]
        i, j, k = pl.program_id(0), pl.program_id(1), pl.program_id(2)
        if nr:
            start, finish = rider.plan(refs[2:2 + nr], refs[3 + nr:3 + 2 * nr], refs[4 + 2 * nr:])

            @pl.when(jnp.logical_and(jnp.logical_and(i == 0, j == 0), k == 0))
            def _():
                start()

        @pl.when(k == 0)
        def _():
            acc[...] = jnp.zeros_like(acc)

        acc[...] += _dot(a_ref[...].astype(BF16), b_ref[...].astype(BF16), dims)

        @pl.when(k == nk - 1)
        def _():
            o_ref[...] = acc[...].astype(out_dtype)

        if nr:
            @pl.when(jnp.logical_and(jnp.logical_and(i == grid[0] - 1, j == grid[1] - 1), k == nk - 1))
            def _():
                finish()

    hbm = pl.BlockSpec(memory_space=pltpu.HBM)
    out_shape = [jax.ShapeDtypeStruct((M, N), out_dtype)]
    scratch = [pltpu.VMEM((tm, tn), F32)]
    if nr:
        res = pl.pallas_call(
            body, grid=grid, in_specs=[a_spec, b_spec] + [hbm] * nr, out_specs=[o_spec] + [hbm] * nr,
            out_shape=out_shape + rider.out_shapes(), scratch_shapes=scratch + rider.sems(), name=name,
            compiler_params=pltpu.CompilerParams(dimension_semantics=("arbitrary",) * 3,
                                                 vmem_limit_bytes=V7X_VMEM_LIMIT_BYTES, has_side_effects=True),
        )(a, b, *rider.arrs)
        return res[0], list(res[1:])
    return pl.pallas_call(
        body, grid=grid, in_specs=[a_spec, b_spec], out_specs=[o_spec], out_shape=out_shape, scratch_shapes=scratch,
        name=name, compiler_params=_cp(("parallel", "parallel", "arbitrary")))(a, b)[0]


def _rows(tile, w, cb=0):
    return pl.BlockSpec((tile, w), lambda i: (i, cb))


def _vec(shape):
    nd = len(shape)
    return pl.BlockSpec(shape, lambda i: (0,) * nd)


def _any():
    return pl.BlockSpec(memory_space=pl.ANY)


def _sds(shape, dtype):
    return jax.ShapeDtypeStruct(shape, dtype)


def _rowcall(body, n_rows, tile, in_specs, out_specs, out_shape, name, aliases=None):
    return pl.pallas_call(
        body, grid=(n_rows // tile,), in_specs=in_specs, out_specs=out_specs, out_shape=out_shape,
        name=name, input_output_aliases=aliases or {}, compiler_params=_cp(("arbitrary",)))


def _zero_at_first(step, *refs):
    @pl.when(step == 0)
    def _():
        for r in refs:
            r[...] = jnp.zeros_like(r)


def rms_mod_fwd(x, nw, sc, sh, name):
    S, D = x.shape
    T = ROW_TILE

    def body(x_ref, nw_ref, sc_ref, sh_ref, h_ref):
        xv = x_ref[...]
        r = lax.rsqrt(jnp.mean(xv * xv, axis=-1, keepdims=True) + EPS)
        y = xv * r * nw_ref[...]
        h_ref[...] = (y * (1.0 + sc_ref[...]) + sh_ref[...]).astype(BF16)

    return _rowcall(body, S, T, [_rows(T, D), _vec((1, D)), _vec((1, D)), _vec((1, D))], _rows(T, D),
                    _sds((S, D), BF16), name)(x, nw, sc, sh)


def rms_mod_bwd(x, dh, dres, nw, sc, name):
    S, D = x.shape
    T = ROW_TILE

    def body(x_ref, dh_ref, dres_ref, nw_ref, sc_ref, dx_ref, dnw_ref, dsc_ref, dsh_ref):
        _zero_at_first(pl.program_id(0), dnw_ref, dsc_ref, dsh_ref)
        xv = x_ref[...]
        d = dh_ref[...]
        r = lax.rsqrt(jnp.mean(xv * xv, axis=-1, keepdims=True) + EPS)
        xh = xv * r
        nwv = nw_ref[...]
        dsh_ref[...] += jnp.sum(d, axis=0, keepdims=True)
        dsc_ref[...] += jnp.sum(d * (xh * nwv), axis=0, keepdims=True)
        dy = d * (1.0 + sc_ref[...])
        dnw_ref[...] += jnp.sum(dy * xh, axis=0, keepdims=True)
        dxh = dy * nwv
        dx_ref[...] = dres_ref[...] + r * (dxh - xh * jnp.mean(dxh * xh, axis=-1, keepdims=True))

    v = _vec((1, D))
    return _rowcall(body, S, T, [_rows(T, D), _rows(T, D), _rows(T, D), v, v],
                    [_rows(T, D), v, v, v],
                    [_sds((S, D), F32), _sds((1, D), F32), _sds((1, D), F32), _sds((1, D), F32)], name)(x, dh, dres, nw, sc)


def _shift_down(tile, halo, k):
    if k == 0:
        return tile
    hr = halo.shape[0]
    rolled = pltpu.roll(tile, k, 0)
    rh = pltpu.roll(halo, k, 0)
    rid = _iota(halo.shape, 0)
    first = jnp.where(rid < k, rh, rolled[:hr])
    return jnp.concatenate([first, rolled[hr:]], axis=0)


def _shift_up(tile, nxt, k):
    if k == 0:
        return tile
    T = tile.shape[0]
    hr = nxt.shape[0]
    rolled = pltpu.roll(tile, T - k, 0)
    rn = pltpu.roll(nxt, hr - k, 0)
    rid = _iota(nxt.shape, 0)
    last = jnp.where(rid >= hr - k, rn, rolled[T - hr:])
    return jnp.concatenate([rolled[:T - hr], last], axis=0)


CONV_CB = 512


def conv_fwd(P, conv_w, conv_b, name):
    S = P.shape[0]
    T = ROW_TILE
    nj = XBC_W // CONV_CB
    cb0 = XBC_OFF // CONV_CB
    hb = T // CONV_HALO

    def body(x_ref, h_ref, w_ref, b_ref, pre_ref, xc_ref):
        i = pl.program_id(0)
        xb = x_ref[...]
        xh = jnp.where(i > 0, h_ref[...], 0.0)
        acc = b_ref[...] + w_ref[3:4, :] * xb
        for idx in range(3):
            acc = acc + w_ref[idx:idx + 1, :] * _shift_down(xb, xh, 3 - idx)
        pre_ref[...] = acc
        xc_ref[...] = _silu(acc)

    blk = pl.BlockSpec((T, CONV_CB), lambda i, j: (i, j))
    return pl.pallas_call(
        body, grid=(S // T, nj),
        in_specs=[pl.BlockSpec((T, CONV_CB), lambda i, j: (i, cb0 + j)),
                  pl.BlockSpec((CONV_HALO, CONV_CB), lambda i, j: (jnp.maximum(i * hb - 1, 0), cb0 + j)),
                  pl.BlockSpec((4, CONV_CB), lambda i, j: (0, j)),
                  pl.BlockSpec((1, CONV_CB), lambda i, j: (0, j))],
        out_specs=[blk, blk], out_shape=[_sds((S, XBC_W), F32), _sds((S, XBC_W), F32)], name=name,
        compiler_params=_cp(("arbitrary", "arbitrary")))(P, P, conv_w, conv_b)


def conv_bwd(dxc, pre, P, dP, conv_w, name):
    S = P.shape[0]
    T = ROW_TILE
    nt = S // T
    nj = XBC_W // CONV_CB
    cb0 = XBC_OFF // CONV_CB
    hb = T // CONV_HALO

    def body(d_ref, dn_ref, p_ref, pn_ref, x_ref, xh_ref, w_ref, dp_in, dp_ref, dw_ref, db_ref):
        i = pl.program_id(1)
        _zero_at_first(i, dw_ref, db_ref)
        dpre = d_ref[...] * _dsilu(p_ref[...])
        dpn = jnp.where(i < nt - 1, dn_ref[...] * _dsilu(pn_ref[...]), 0.0)
        db_ref[...] += jnp.sum(dpre, axis=0, keepdims=True)
        xb = x_ref[...]
        xh = jnp.where(i > 0, xh_ref[...], 0.0)
        dx = jnp.zeros_like(dpre)
        for idx in range(4):
            k = 3 - idx
            dw_ref[idx:idx + 1, :] += jnp.sum(dpre * _shift_down(xb, xh, k), axis=0, keepdims=True)
            dx = dx + w_ref[idx:idx + 1, :] * _shift_up(dpre, dpn, k)
        dp_ref[...] = dx.astype(BF16)

    cur = pl.BlockSpec((T, CONV_CB), lambda j, i: (i, j))
    nxt = pl.BlockSpec((CONV_HALO, CONV_CB), lambda j, i: (jnp.minimum((i + 1) * hb, S // CONV_HALO - 1), j))
    return pl.pallas_call(
        body, grid=(nj, nt),
        in_specs=[cur, nxt, cur, nxt,
                  pl.BlockSpec((T, CONV_CB), lambda j, i: (i, cb0 + j)),
                  pl.BlockSpec((CONV_HALO, CONV_CB), lambda j, i: (jnp.maximum(i * hb - 1, 0), cb0 + j)),
                  pl.BlockSpec((4, CONV_CB), lambda j, i: (0, j)), _any()],
        out_specs=[pl.BlockSpec((T, CONV_CB), lambda j, i: (i, cb0 + j)),
                   pl.BlockSpec((4, CONV_CB), lambda j, i: (0, j)),
                   pl.BlockSpec((1, CONV_CB), lambda j, i: (0, j))],
        out_shape=[_sds(dP.shape, BF16), _sds((4, XBC_W), F32), _sds((1, XBC_W), F32)],
        input_output_aliases={7: 0}, name=name,
        compiler_params=_cp(("arbitrary", "arbitrary")))(dxc, dxc, pre, pre, P, P, conv_w, dP)


def _col(v, h, lane):
    return jnp.sum(jnp.where(lane == h, v, 0.0), axis=1, keepdims=True)


def _row(v, h, rowi):
    return jnp.sum(jnp.where(rowi == h, v, 0.0), axis=0, keepdims=True)


def _ssd_common(dt_ref, dtb_ref, al_ref):
    L = SSD_CHUNK
    lane = _iota((L, L), 1)
    rowi = _iota((L, L), 0)
    dtr = dt_ref[...] + dtb_ref[...]
    dt = _softplus(dtr)
    A = -jnp.exp(al_ref[...])
    a = dt * A
    tril = (rowi >= lane).astype(F32)
    ac = jnp.dot(tril, a, precision=HIGHEST, preferred_element_type=F32)
    acT = ac.T
    last = _row(ac, L - 1, rowi)
    return dict(lane=lane, rowi=rowi, dtr=dtr, dt=dt, A=A, ac=ac, acT=acT, last=last,
                eac=jnp.exp(ac), fac=jnp.exp(last - ac), cd=jnp.exp(last),
                lo=lane < SSD_HEAD_DIM, causal=rowi >= lane)


def _pair(v, h0, q):
    lane = q["lane"][:v.shape[0]]
    lo = q["lo"][:v.shape[0]]
    return jnp.where(lo, _col(v, h0, lane), _col(v, h0 + 1, lane))


def _decay(q, h):
    seg = _col(q["ac"], h, q["lane"]) - _row(q["acT"], h, q["rowi"])
    return jnp.exp(jnp.where(q["causal"], seg, NEG))


def ssd_fwd(xc, P, dtb, alog, dsk, name):
    S = xc.shape[0]
    L = SSD_CHUNK
    nc = S // L

    def body(xs_ref, b_ref, c_ref, dt_ref, dtb_ref, al_ref, dk_ref, y_ref, st_ref, H):
        @pl.when(pl.program_id(0) == 0)
        def _():
            H[...] = jnp.zeros_like(H)

        st_ref[0] = H[...]
        q = _ssd_common(dt_ref, dtb_ref, al_ref)
        dk = dk_ref[...]
        for g in range(2):
            Bg = b_ref[:, g * L:(g + 1) * L].astype(BF16)
            Cg = c_ref[:, g * L:(g + 1) * L].astype(BF16)
            cb = _nt(Cg, Bg)
            for jj in range(4):
                j = g * 4 + jj
                h0 = 2 * j
                sl = slice(j * L, (j + 1) * L)
                xs_p = xs_ref[:, sl]
                X = xs_p * _pair(q["dt"], h0, q)
                Xb = X.astype(BF16)
                G0 = (cb * _decay(q, h0)).astype(BF16)
                G1 = (cb * _decay(q, h0 + 1)).astype(BF16)
                Yd = jnp.where(q["lo"], _nn(G0, Xb), _nn(G1, Xb))
                Hp = H[:, sl]
                Yo = _nn(Cg, Hp.astype(BF16)) * _pair(q["eac"], h0, q)
                y_ref[:, sl] = Yd + Yo + _pair(dk, h0, q) * xs_p
                Xd = (X * _pair(q["fac"], h0, q)).astype(BF16)
                H[:, sl] = _pair(q["cd"], h0, q) * Hp + _tn(Bg, Xd)

    v = _vec((1, LANES))
    return pl.pallas_call(
        body, grid=(nc,),
        in_specs=[pl.BlockSpec((L, 1024), lambda c: (c, 0)),
                  pl.BlockSpec((L, 256), lambda c: (c, 4)),
                  pl.BlockSpec((L, 256), lambda c: (c, 5)),
                  pl.BlockSpec((L, LANES), lambda c: (c, DT_OFF // LANES)), v, v, v],
        out_specs=[pl.BlockSpec((L, 1024), lambda c: (c, 0)),
                   pl.BlockSpec((1, SSD_STATE, 1024), lambda c: (c, 0, 0))],
        out_shape=[_sds((S, 1024), F32), _sds((nc, SSD_STATE, 1024), F32)],
        scratch_shapes=[pltpu.VMEM((SSD_STATE, 1024), F32)], name=name,
        compiler_params=_cp(("arbitrary",)))(xc, xc, xc, P, dtb, alog, dsk)


def ssd_bwd(xc, P, st, dy, dP, dtb, alog, dsk, name):
    S = xc.shape[0]
    L = SSD_CHUNK
    nc = S // L

    def body(xs_ref, b_ref, c_ref, dt_ref, st_ref, dy_ref, dtb_ref, al_ref, dk_ref, dp_in,
             dxc_ref, dp_ref, ddtb_ref, dal_ref, dD_ref, dH):
        step = pl.program_id(0)

        @pl.when(step == 0)
        def _():
            dH[...] = jnp.zeros_like(dH)

        _zero_at_first(step, ddtb_ref, dal_ref, dD_ref)
        q = _ssd_common(dt_ref, dtb_ref, al_ref)
        lane, rowi, lo = q["lane"], q["rowi"], q["lo"]
        lane1 = lane[:1]
        lo1 = lo[:1]
        dk = dk_ref[...]
        dac = jnp.zeros((L, L), F32)
        dacT = jnp.zeros((L, L), F32)
        dlast = jnp.zeros((1, L), F32)
        ddt = jnp.zeros((L, L), F32)

        def put(h0, s0, s1):
            return jnp.where(lane == h0, s0, 0.0) + jnp.where(lane == h0 + 1, s1, 0.0)

        def put1(h0, s0, s1):
            return jnp.where(lane1 == h0, s0, 0.0) + jnp.where(lane1 == h0 + 1, s1, 0.0)

        def hsum(v):
            return (jnp.sum(jnp.where(lo, v, 0.0), axis=1, keepdims=True),
                    jnp.sum(jnp.where(lo, 0.0, v), axis=1, keepdims=True))

        for g in range(2):
            Bg = b_ref[:, g * L:(g + 1) * L].astype(BF16)
            Cg = c_ref[:, g * L:(g + 1) * L].astype(BF16)
            cb = _nt(Cg, Bg)
            dcb = jnp.zeros((L, L), F32)
            dCg = jnp.zeros((L, L), F32)
            dBg = jnp.zeros((L, L), F32)
            for jj in range(4):
                j = g * 4 + jj
                h0 = 2 * j
                sl = slice(j * L, (j + 1) * L)
                xs_p = xs_ref[:, sl]
                dY = dy_ref[:, sl]
                dYb = dY.astype(BF16)
                dtp = _pair(q["dt"], h0, q)
                X = xs_p * dtp
                e_p = _pair(q["eac"], h0, q)
                f_p = _pair(q["fac"], h0, q)
                cd_p = _pair(q["cd"], h0, q)
                Hp = st_ref[0, :, sl]
                Hpb = Hp.astype(BF16)
                dHn = dH[:, sl]
                dHnb = dHn.astype(BF16)
                dD_ref[:, sl] += jnp.sum(dY * xs_p, axis=0, keepdims=True)
                dxs = _pair(dk, h0, q) * dY
                Yo = _nn(Cg, Hpb) * e_p
                dZb = (dY * e_p).astype(BF16)
                s0, s1 = hsum(dY * Yo)
                dac = dac + put(h0, s0, s1)
                dCg = dCg + _nt(dZb, Hpb)
                dHp = _tn(Cg, dZb) + cd_p * dHn
                t = jnp.sum(dHn * Hp, axis=0, keepdims=True)
                d0 = jnp.sum(jnp.where(lo1, t, 0.0), axis=1, keepdims=True)
                d1 = jnp.sum(jnp.where(lo1, 0.0, t), axis=1, keepdims=True)
                dlast = dlast + put1(h0, d0, d1) * q["cd"]
                Xd = X * f_p
                dXd = _nn(Bg, dHnb)
                dBg = dBg + _nt(Xd.astype(BF16), dHnb)
                dX = dXd * f_p
                s0, s1 = hsum(dXd * Xd)
                dac = dac - put(h0, s0, s1)
                dlast = dlast + put1(h0, jnp.sum(s0, axis=0, keepdims=True), jnp.sum(s1, axis=0, keepdims=True))
                for hh, mask in ((h0, lo), (h0 + 1, jnp.logical_not(lo))):
                    dec = _decay(q, hh)
                    Gm = cb * dec
                    Xh = jnp.where(mask, X, 0.0).astype(BF16)
                    dG = _nt(dYb, Xh)
                    dcb = dcb + dG * dec
                    Q = dG * Gm
                    dac = dac + jnp.where(lane == hh, jnp.sum(Q, axis=1, keepdims=True), 0.0)
                    dacT = dacT - jnp.where(rowi == hh, jnp.sum(Q, axis=0, keepdims=True), 0.0)
                    dX = dX + jnp.where(mask, _tn(Gm.astype(BF16), dYb), 0.0)
                dxs = dxs + dX * dtp
                s0, s1 = hsum(dX * xs_p)
                ddt = ddt + put(h0, s0, s1)
                dxc_ref[:, sl] = dxs
                dH[:, sl] = dHp
            dcbb = dcb.astype(BF16)
            dxc_ref[:, 1024 + g * L:1024 + (g + 1) * L] = dBg + _tn(dcbb, Cg)
            dxc_ref[:, 1280 + g * L:1280 + (g + 1) * L] = dCg + _nn(dcbb, Bg)

        dac = dac + dacT.T + jnp.where(rowi == L - 1, dlast, 0.0)
        triu = (rowi <= lane).astype(F32)
        da = jnp.dot(triu, dac, precision=HIGHEST, preferred_element_type=F32)
        ddt = ddt + da * q["A"]
        dal_ref[...] += jnp.sum(da * q["dt"], axis=0, keepdims=True) * q["A"]
        ddtr = jnp.where(lane < SSD_HEADS, ddt * _sigmoid(q["dtr"]), 0.0)
        ddtb_ref[...] += jnp.sum(ddtr, axis=0, keepdims=True)
        dp_ref[...] = jnp.concatenate([ddtr, jnp.zeros_like(ddtr)], axis=1).astype(BF16)

    v = _vec((1, LANES))
    rev = lambda c: nc - 1 - c
    return pl.pallas_call(
        body, grid=(nc,),
        in_specs=[pl.BlockSpec((L, 1024), lambda c: (rev(c), 0)),
                  pl.BlockSpec((L, 256), lambda c: (rev(c), 4)),
                  pl.BlockSpec((L, 256), lambda c: (rev(c), 5)),
                  pl.BlockSpec((L, LANES), lambda c: (rev(c), DT_OFF // LANES)),
                  pl.BlockSpec((1, SSD_STATE, 1024), lambda c: (rev(c), 0, 0)),
                  pl.BlockSpec((L, 1024), lambda c: (rev(c), 0)), v, v, v, _any()],
        out_specs=[pl.BlockSpec((L, XBC_W), lambda c: (rev(c), 0)),
                   pl.BlockSpec((L, DT_W), lambda c: (rev(c), DT_OFF // DT_W)),
                   v, v, _vec((1, 1024))],
        out_shape=[_sds((S, XBC_W), F32), _sds(dP.shape, BF16), _sds((1, LANES), F32), _sds((1, LANES), F32),
                   _sds((1, 1024), F32)],
        scratch_shapes=[pltpu.VMEM((SSD_STATE, 1024), F32)], input_output_aliases={9: 1}, name=name,
        compiler_params=_cp(("arbitrary",)))(xc, xc, xc, P, st, dy, dtb, alog, dsk, dP)


def ssd_post_fwd(y, P, nw, name):
    S = y.shape[0]
    T = ROW_TILE
    HW = 512

    def body(y_ref, z_ref, w_ref, o_ref):
        g = y_ref[...] * _silu(z_ref[...])
        for k in range(2):
            gk = g[:, k * HW:(k + 1) * HW]
            r = lax.rsqrt(jnp.mean(gk * gk, axis=-1, keepdims=True) + EPS)
            o_ref[:, k * HW:(k + 1) * HW] = (gk * r * w_ref[:, k * HW:(k + 1) * HW]).astype(BF16)

    return _rowcall(body, S, T, [_rows(T, 1024), _rows(T, 1024, Z_OFF // 1024), _vec((1, 1024))], _rows(T, 1024),
                    _sds((S, 1024), BF16), name)(y, P, nw)


def ssd_post_bwd(y, P, dout, dP, nw, name):
    S = y.shape[0]
    T = ROW_TILE
    HW = 512

    def body(y_ref, z_ref, d_ref, w_ref, dp_in, dy_ref, dz_ref, dw_ref):
        _zero_at_first(pl.program_id(0), dw_ref)
        yv = y_ref[...]
        zv = z_ref[...]
        sz = _silu(zv)
        g = yv * sz
        dv = d_ref[...]
        dgs = []
        for k in range(2):
            sl = slice(k * HW, (k + 1) * HW)
            gk = g[:, sl]
            r = lax.rsqrt(jnp.mean(gk * gk, axis=-1, keepdims=True) + EPS)
            gh = gk * r
            dk = dv[:, sl]
            dw_ref[:, sl] += jnp.sum(dk * gh, axis=0, keepdims=True)
            dgn = dk * w_ref[:, sl]
            dgs.append(r * (dgn - gh * jnp.mean(dgn * gh, axis=-1, keepdims=True)))
        dg = jnp.concatenate(dgs, axis=1)
        dy_ref[...] = dg * sz
        dz_ref[...] = (dg * yv * _dsilu(zv)).astype(BF16)

    zc = Z_OFF // 1024
    return _rowcall(body, S, T, [_rows(T, 1024), _rows(T, 1024, zc), _rows(T, 1024), _vec((1, 1024)), _any()],
                    [_rows(T, 1024), _rows(T, 1024, zc), _vec((1, 1024))],
                    [_sds((S, 1024), F32), _sds(dP.shape, BF16), _sds((1, 1024), F32)], name,
                    aliases={4: 1})(y, P, dout, nw, dP)


HEAD_W = 3 * ATTN_HEAD_DIM
HEAD_BLOCKS = NP // HEAD_W


def _slope(gi, h):
    c = [float(v) for v in SLOPES[gi]]
    return jnp.where(h == 0, c[0], jnp.where(h == 1, c[1], jnp.where(h == 2, c[2], c[3])))


def _rmsn(t):
    r = lax.rsqrt(jnp.mean(t * t, axis=-1, keepdims=True) + EPS)
    return t * r, r


def _rms_bwd(th, r, w, dn):
    dth = dn * w
    return r * (dth - th * jnp.mean(dth * th, axis=-1, keepdims=True))


ATT_ROWS = 2048


def _strided(r0, d):
    return pl.ds(r0, ATTN_STEPS, stride=d) if d > 1 else pl.ds(r0, ATTN_STEPS)


def attn_fwd(P, qw, kw, gi, name):
    S = P.shape[0]
    d = DILATIONS[gi]
    B = ATTN_STEPS
    E = ATTN_HEAD_DIM
    BR = B * d
    nq = ATT_ROWS // BR
    nsb = S // ATT_ROWS
    scale = E ** -0.5

    def body(q_ref, k_ref, v_ref, kp_ref, vp_ref, qw_ref, kw_ref, o_ref, l_ref):
        h = pl.program_id(0)
        s = pl.program_id(1)
        slope = _slope(gi, h) * float(d)
        qwv = qw_ref[...]
        kwv = kw_ref[...]
        qi = _iota((B, B), 0)
        kj = _iota((B, B), 1)
        rel_c = (qi - kj).astype(F32)
        rel_p = rel_c + float(B)
        bias_c = jnp.where(qi >= kj, -slope * rel_c, NEG)
        bias_p0 = jnp.where(jnp.logical_and(qi <= kj, s > 0), -slope * rel_p, NEG)
        bias_p = jnp.where(qi <= kj, -slope * rel_p, NEG)
        for j in range(nq):
            for r in range(d):
                rows = _strided(j * BR + r, d)
                if j == 0:
                    kpv, vpv, bp = kp_ref[_strided(r, d), :], vp_ref[_strided(r, d), :], bias_p0
                else:
                    prow = _strided((j - 1) * BR + r, d)
                    kpv, vpv, bp = k_ref[prow, :], v_ref[prow, :], bias_p
                qh, _ = _rmsn(q_ref[rows, :])
                kh, _ = _rmsn(k_ref[rows, :])
                kph, _ = _rmsn(kpv)
                qn = (qh * qwv).astype(BF16)
                kn = (kh * kwv).astype(BF16)
                kpn = (kph * kwv).astype(BF16)
                s_c = _nt(qn, kn) * scale + bias_c
                s_p = _nt(qn, kpn) * scale + bp
                m = jnp.maximum(jnp.max(s_c, axis=1, keepdims=True), jnp.max(s_p, axis=1, keepdims=True))
                l = jnp.sum(jnp.exp(s_c - m), axis=1, keepdims=True) + jnp.sum(jnp.exp(s_p - m), axis=1, keepdims=True)
                lse = m + jnp.log(l)
                p_c = jnp.exp(s_c - lse).astype(BF16)
                p_p = jnp.exp(s_p - lse).astype(BF16)
                o_ref[rows, :] = _nn(p_c, v_ref[rows, :].astype(BF16)) + _nn(p_p, vpv.astype(BF16))
                l_ref[rows, :] = jnp.broadcast_to(lse, (B, E))

    def cur(i):
        return pl.BlockSpec((ATT_ROWS, E), lambda h, s: (s, (gi * 4 + h) * 3 + i))

    def prev(i):
        return pl.BlockSpec((BR, E), lambda h, s: (jnp.maximum(s * nq - 1, 0), (gi * 4 + h) * 3 + i))

    wv = pl.BlockSpec((1, E), lambda h, s: (0, 0))
    ob = pl.BlockSpec((ATT_ROWS, E), lambda h, s: (s, h))
    return pl.pallas_call(
        body, grid=(4, nsb), in_specs=[cur(0), cur(1), cur(2), prev(1), prev(2), wv, wv],
        out_specs=[ob, ob], out_shape=[_sds((S, 512), F32), _sds((S, 512), F32)], name=name,
        compiler_params=_cp(("arbitrary", "arbitrary")))(P, P, P, P, P, qw, kw)


def attn_bwd(P, o, lse, do, dlse, dP, qw, kw, gi, name):
    S = P.shape[0]
    d = DILATIONS[gi]
    B = ATTN_STEPS
    E = ATTN_HEAD_DIM
    BR = B * d
    nq = ATT_ROWS // BR
    nsb = S // ATT_ROWS
    nblk = S // BR
    scale = E ** -0.5

    def body(q_ref, k_ref, v_ref, kp_ref, vp_ref, qx_ref, o_ref, l_ref, do_ref, dl_ref, ox_ref, lx_ref, dox_ref, dlx_ref,
             qw_ref, kw_ref, dp_in, dp_ref, dqw_ref, dkw_ref, stage_q, stage_k, stage_v):
        h = pl.program_id(0)
        s = pl.program_id(1)

        @pl.when(jnp.logical_and(h == 0, s == 0))
        def _():
            dqw_ref[...] = jnp.zeros_like(dqw_ref)
            dkw_ref[...] = jnp.zeros_like(dkw_ref)

        slope = _slope(gi, h) * float(d)
        qwv = qw_ref[...]
        kwv = kw_ref[...]
        qi = _iota((B, B), 0)
        kj = _iota((B, B), 1)
        rel_c = (qi - kj).astype(F32)
        rel_p = rel_c + float(B)
        band = qi <= kj
        bias_c = jnp.where(qi >= kj, -slope * rel_c, NEG)
        bias_p = jnp.where(band, -slope * rel_p, NEG)
        bias_p_first = jnp.where(jnp.logical_and(band, s > 0), -slope * rel_p, NEG)
        bias_p_last = jnp.where(jnp.logical_and(band, s < nsb - 1), -slope * rel_p, NEG)
        dqw = jnp.zeros((1, E), F32)
        dkw = jnp.zeros((1, E), F32)
        for j in range(nq):
            for r in range(d):
                rows = _strided(j * BR + r, d)
                if j == 0:
                    kpv, vpv, bB = kp_ref[_strided(r, d), :], vp_ref[_strided(r, d), :], bias_p_first
                else:
                    prow = _strided((j - 1) * BR + r, d)
                    kpv, vpv, bB = k_ref[prow, :], v_ref[prow, :], bias_p
                if j == nq - 1:
                    xr = _strided(r, d)
                    qxv, oxv, lxv, doxv, dlxv, bC = (qx_ref[xr, :], ox_ref[xr, :], lx_ref[xr, :], dox_ref[xr, :],
                                                     dlx_ref[xr, :], bias_p_last)
                else:
                    xr = _strided((j + 1) * BR + r, d)
                    qxv, oxv, lxv, doxv, dlxv, bC = (q_ref[xr, :], o_ref[xr, :], l_ref[xr, :], do_ref[xr, :],
                                                     dl_ref[xr, :], bias_p)
                qh, rq = _rmsn(q_ref[rows, :])
                kh, rk = _rmsn(k_ref[rows, :])
                kph, _ = _rmsn(kpv)
                qxh, _ = _rmsn(qxv)
                qn = (qh * qwv).astype(BF16)
                kn = (kh * kwv).astype(BF16)
                kpn = (kph * kwv).astype(BF16)
                qxn = (qxh * qwv).astype(BF16)
                v = v_ref[rows, :].astype(BF16)
                vp = vpv.astype(BF16)
                lse_c = l_ref[rows, :]
                pA = jnp.exp(_nt(qn, kn) * scale + bias_c - lse_c)
                pB = jnp.exp(_nt(qn, kpn) * scale + bB - lse_c)
                pC = jnp.exp(_nt(qxn, kn) * scale + bC - lxv)
                doc = do_ref[rows, :]
                docb = doc.astype(BF16)
                doxb = doxv.astype(BF16)
                corr_c = jnp.sum(dl_ref[rows, :] - doc * o_ref[rows, :], axis=1, keepdims=True)
                corr_n = jnp.sum(dlxv - doxv * oxv, axis=1, keepdims=True)
                dsA = (pA * (_nt(docb, v) + corr_c)).astype(BF16)
                dsB = (pB * (_nt(docb, vp) + corr_c)).astype(BF16)
                dsC = (pC * (_nt(doxb, v) + corr_n)).astype(BF16)
                dqn = (_nn(dsA, kn) + _nn(dsB, kpn)) * scale
                dkn = (_tn(dsA, qn) + _tn(dsC, qxn)) * scale
                dv = _tn(pA.astype(BF16), docb) + _tn(pC.astype(BF16), doxb)
                dqw = dqw + jnp.sum(dqn * qh, axis=0, keepdims=True)
                dkw = dkw + jnp.sum(dkn * kh, axis=0, keepdims=True)
                stage_q[rows, :] = _rms_bwd(qh, rq, qwv, dqn)
                stage_k[rows, :] = _rms_bwd(kh, rk, kwv, dkn)
                stage_v[rows, :] = dv
        dqw_ref[...] += dqw
        dkw_ref[...] += dkw
        dp_ref[:, 0:E] = stage_q[...].astype(BF16)
        dp_ref[:, E:2 * E] = stage_k[...].astype(BF16)
        dp_ref[:, 2 * E:3 * E] = stage_v[...].astype(BF16)

    def cur(i):
        return pl.BlockSpec((ATT_ROWS, E), lambda h, s: (s, (gi * 4 + h) * 3 + i))

    def prev(i):
        return pl.BlockSpec((BR, E), lambda h, s: (jnp.maximum(s * nq - 1, 0), (gi * 4 + h) * 3 + i))

    nxt_q = pl.BlockSpec((BR, E), lambda h, s: (jnp.minimum((s + 1) * nq, nblk - 1), (gi * 4 + h) * 3))
    ocur = pl.BlockSpec((ATT_ROWS, E), lambda h, s: (s, h))
    onxt = pl.BlockSpec((BR, E), lambda h, s: (jnp.minimum((s + 1) * nq, nblk - 1), h))
    wv = pl.BlockSpec((1, E), lambda h, s: (0, 0))
    return pl.pallas_call(
        body, grid=(4, nsb),
        in_specs=[cur(0), cur(1), cur(2), prev(1), prev(2), nxt_q, ocur, ocur, ocur, ocur, onxt, onxt, onxt, onxt, wv, wv,
                  _any()],
        out_specs=[pl.BlockSpec((ATT_ROWS, HEAD_W), lambda h, s: (s, gi * 4 + h)), wv, wv],
        out_shape=[_sds(dP.shape, BF16), _sds((1, E), F32), _sds((1, E), F32)],
        scratch_shapes=[pltpu.VMEM((ATT_ROWS, E), F32)] * 3,
        input_output_aliases={16: 0}, name=name,
        compiler_params=_cp(("arbitrary", "arbitrary")))(
            P, P, P, P, P, P, o, lse, do, dlse, o, lse, do, dlse, qw, kw, dP)


def attn_merge_fwd(os_, ls_, name):
    S = os_[0].shape[0]
    T = ROW_TILE

    def body(o0, o1, o2, l0, l1, l2, out):
        a, b, c = l0[...], l1[...], l2[...]
        m = jnp.maximum(jnp.maximum(a, b), c)
        ea, eb, ec = jnp.exp(a - m), jnp.exp(b - m), jnp.exp(c - m)
        inv = 1.0 / (ea + eb + ec)
        out[...] = ((ea * o0[...] + eb * o1[...] + ec * o2[...]) * inv).astype(BF16)

    return _rowcall(body, S, T, [_rows(T, 512)] * 6, _rows(T, 512), _sds((S, 512), BF16), name)(*os_, *ls_)


def attn_merge_bwd(os_, ls_, dom, name):
    S = os_[0].shape[0]
    T = ROW_TILE

    def body(o0, o1, o2, l0, l1, l2, d_ref, do0, do1, do2, dl0, dl1, dl2):
        a, b, c = l0[...], l1[...], l2[...]
        m = jnp.maximum(jnp.maximum(a, b), c)
        ea, eb, ec = jnp.exp(a - m), jnp.exp(b - m), jnp.exp(c - m)
        inv = 1.0 / (ea + eb + ec)
        wa, wb, wc = ea * inv, eb * inv, ec * inv
        dv = d_ref[...]
        do0[...] = wa * dv
        do1[...] = wb * dv
        do2[...] = wc * dv
        ga, gb, gc = dv * o0[...], dv * o1[...], dv * o2[...]
        tot = wa * ga + wb * gb + wc * gc
        dl0[...] = wa * (ga - tot)
        dl1[...] = wb * (gb - tot)
        dl2[...] = wc * (gc - tot)

    sp = _rows(T, 512)
    return _rowcall(body, S, T, [sp] * 7, [sp] * 6, [_sds((S, 512), F32)] * 6, name)(*os_, *ls_, dom)


def _counts(i, T, rows, w, offset=0):
    t = i * T + offset + _iota((rows, 1), 0)
    return jnp.minimum(t + 1, w).astype(F32)


def pool_fwd(P, wmix, scale, name):
    S = P.shape[0]
    T = ROW_TILE
    GW = 256
    uc = U_OFF // 1024
    hb = T // POOL_HALO

    def body(u_ref, h_ref, wm_ref, sc_ref, pl_ref, yp_ref):
        i = pl.program_id(0)
        u = u_ref[...]
        ext = jnp.concatenate([jnp.where(i > 0, h_ref[...], 0.0), u], axis=0)
        for g, w in enumerate(POOL_WINDOWS):
            sl = slice(g * GW, (g + 1) * GW)
            s = ext[:, sl]
            k = 1
            while k < w:
                s = s + pltpu.roll(s, k, 0)
                k *= 2
            pooled = (s[POOL_HALO:] / _counts(i, T, T, w) - u[:, sl]).astype(BF16)
            pl_ref[:, sl] = pooled
            yp_ref[:, sl] = (_nn(pooled, wm_ref[g]) * sc_ref[:, sl]).astype(BF16)

    return _rowcall(
        body, S, T,
        [_rows(T, 1024, uc), pl.BlockSpec((POOL_HALO, 1024), lambda i: (jnp.maximum(i * hb - 1, 0), uc)),
         _vec((4, GW, GW)), _vec((1, 1024))],
        [_rows(T, 1024), _rows(T, 1024)], [_sds((S, 1024), BF16), _sds((S, 1024), BF16)], name)(P, P, wmix, scale)


def pool_bwd(dyp, pooled, dP, wmix, scale, name):
    S = dyp.shape[0]
    T = ROW_TILE
    nt = S // T
    GW = 256
    uc = U_OFF // 1024
    hb = T // POOL_HALO
    TE = T + POOL_HALO

    def body(d_ref, dn_ref, p_ref, wm_ref, sc_ref, dp_in, du_ref, dwm_ref, dsc_ref):
        i = pl.program_id(0)
        _zero_at_first(i, dwm_ref, dsc_ref)
        dv = d_ref[...]
        dn = jnp.where(i < nt - 1, dn_ref[...], 0.0)
        for g, w in enumerate(POOL_WINDOWS):
            sl = slice(g * GW, (g + 1) * GW)
            pg = p_ref[:, sl]
            wm = wm_ref[g]
            dsc_ref[:, sl] += jnp.sum(dv[:, sl] * _nn(pg, wm), axis=0, keepdims=True)
            dmb = (dv[:, sl] * sc_ref[:, sl]).astype(BF16)
            dmnb = (dn[:, sl] * sc_ref[:, sl]).astype(BF16)
            dwm_ref[g] += _tn(pg, dmb)
            dpl = _nt(dmb, wm)
            dpln = _nt(dmnb, wm)
            s = jnp.concatenate([dpl / _counts(i, T, T, w), dpln / _counts(i, T, POOL_HALO, w, T)], axis=0)
            k = 1
            while k < w:
                s = s + pltpu.roll(s, TE - k, 0)
                k *= 2
            du_ref[:, sl] = (s[:T] - dpl).astype(BF16)

    return _rowcall(
        body, S, T,
        [_rows(T, 1024), pl.BlockSpec((POOL_HALO, 1024), lambda i: (jnp.minimum((i + 1) * hb, S // POOL_HALO - 1), 0)),
         _rows(T, 1024), _vec((4, GW, GW)), _vec((1, 1024)), _any()],
        [_rows(T, 1024, uc), _vec((4, GW, GW)), _vec((1, 1024))],
        [_sds(dP.shape, BF16), _sds((4, GW, GW), F32), _sds((1, 1024), F32)], name,
        aliases={5: 0})(dyp, dyp, pooled, wmix, scale, dP)


def gate_fwd(P, ys, ya, yp, name):
    S = ys.shape[0]
    T = ROW_TILE
    gc = GATE_OFF // 1024

    def body(g0, g1, g2, a, b, c, out):
        out[...] = (_sigmoid(g0[...]) * a[...] + _sigmoid(g1[...]) * b[...] + _sigmoid(g2[...]) * c[...]).astype(BF16)

    r = _rows(T, 1024)
    return _rowcall(body, S, T, [_rows(T, 1024, gc), _rows(T, 1024, gc + 1), _rows(T, 1024, gc + 2), r, r, r], r,
                    _sds((S, 1024), BF16), name)(P, P, P, ys, ya, yp)


def gate_bwd(P, ys, ya, yp, dm, dP, name):
    S = ys.shape[0]
    T = ROW_TILE
    gc = GATE_OFF // 1024

    def body(g0, g1, g2, a, b, c, d_ref, dp_in, dg_ref, da, db, dc):
        dv = d_ref[...]
        outs = []
        for gr, yr, dyr in ((g0, a, da), (g1, b, db), (g2, c, dc)):
            s = _sigmoid(gr[...])
            dyr[...] = (s * dv).astype(BF16)
            outs.append(dv * yr[...] * s * (1.0 - s))
        dg_ref[...] = jnp.concatenate(outs, axis=1).astype(BF16)

    r = _rows(T, 1024)
    return _rowcall(body, S, T, [_rows(T, 1024, gc), _rows(T, 1024, gc + 1), _rows(T, 1024, gc + 2), r, r, r, r, _any()],
                    [_rows(T, GATE_W, GATE_OFF // GATE_W), r, r, r],
                    [_sds(dP.shape, BF16)] + [_sds((S, 1024), BF16)] * 3, name,
                    aliases={7: 0})(P, P, P, ys, ya, yp, dm, dP)


def resid_fwd(x, g, y, name):
    S, D = x.shape
    T = ROW_TILE

    def body(x_ref, g_ref, y_ref, o_ref):
        o_ref[...] = x_ref[...] + g_ref[...] * y_ref[...]

    r = _rows(T, D)
    return _rowcall(body, S, T, [r, _vec((1, D)), r], r, _sds((S, D), F32), name)(x, g, y)


def resid_bwd(dx, g, y, name):
    S, D = dx.shape
    T = ROW_TILE

    def body(d_ref, g_ref, y_ref, dy_ref, dg_ref):
        _zero_at_first(pl.program_id(0), dg_ref)
        dv = d_ref[...]
        dy_ref[...] = (g_ref[...] * dv).astype(BF16)
        dg_ref[...] += jnp.sum(dv * y_ref[...], axis=0, keepdims=True)

    r = _rows(T, D)
    return _rowcall(body, S, T, [r, _vec((1, D)), r], [r, _vec((1, D))], [_sds((S, D), BF16), _sds((1, D), F32)],
                    name)(dx, g, y)


def relu2_fwd(f, name):
    S, W = f.shape
    T = ROW_TILE

    def body(f_ref, o_ref):
        r = jnp.maximum(f_ref[...], 0.0)
        o_ref[...] = (r * r).astype(BF16)

    return _rowcall(body, S, T, [_rows(T, W)], _rows(T, W), _sds((S, W), BF16), name)(f)


def relu2_bwd(f, da, name):
    S, W = f.shape
    T = ROW_TILE

    def body(f_ref, d_ref, o_ref):
        o_ref[...] = (2.0 * jnp.maximum(f_ref[...], 0.0) * d_ref[...]).astype(BF16)

    r = _rows(T, W)
    return _rowcall(body, S, T, [r, r], r, _sds((S, W), BF16), name)(f, da)


def loss_fwd_bwd(y, target, name):
    S, D = y.shape
    T = ROW_TILE

    def body(y_ref, t_ref, acc_ref, dy_ref):
        _zero_at_first(pl.program_id(0), acc_ref)
        e = y_ref[...] - t_ref[...]
        acc_ref[...] += jnp.sum(e * e, axis=0, keepdims=True)
        dy_ref[...] = e * (1.0 / D)

    r = _rows(T, D)
    return _rowcall(body, S, T, [r, r], [_vec((1, D)), r], [_sds((1, D), F32), _sds((S, D), F32)], name)(y, target)


def silu_rows(c, name):
    def body(c_ref, o_ref):
        o_ref[...] = _silu(c_ref[...])

    return pl.pallas_call(body, out_shape=_sds(c.shape, F32), name=name)(c)


def adamw(parts, w, m, v, name, row0=0, prev=None):
    n, R, C = parts.shape
    Rt = w.shape[0]
    tr = R
    while tr * C * 4 > (1 << 20) and tr % 16 == 0:
        tr //= 2
    b0 = row0 // tr
    c1 = 1.0 / (1.0 - ADAM_B1 ** ADAM_STEP)
    c2 = 1.0 / (1.0 - ADAM_B2 ** ADAM_STEP)

    def body(p_ref, w_ref, m_ref, v_ref, *rest):
        g_ref, d_ref, nm_ref, nv_ref = rest[-4:]
        g = p_ref[0].astype(F32)
        for k in range(1, n):
            g = g + p_ref[k].astype(F32)
        nm = ADAM_B1 * m_ref[...] + (1.0 - ADAM_B1) * g
        nv = ADAM_B2 * v_ref[...] + (1.0 - ADAM_B2) * (g * g)
        g_ref[...] = g
        nm_ref[...] = nm
        nv_ref[...] = nv
        d_ref[...] = -ADAM_LR * ((nm * c1) / (jnp.sqrt(nv * c2) + ADAM_EPS) + ADAM_WD * w_ref[...])

    r = pl.BlockSpec((tr, C), lambda i: (b0 + i, 0))
    ins = [parts, w, m, v] + (list(prev) if prev is not None else [])
    in_specs = [pl.BlockSpec((n, tr, C), lambda i: (0, i, 0)), r, r, r] + ([_any()] * 4 if prev is not None else [])
    aliases = {4 + k: k for k in range(4)} if prev is not None else None
    return _rowcall(body, R, tr, in_specs, [r, r, r, r], [_sds((Rt, C), F32)] * 4, name, aliases=aliases)(*ins)


FWD_HOSTS = {"in_proj": ("w_in", "w_attn_out", "w_pool_mix"), "ff1": ("w_ff1", "w_ssd_out"),
             "ff2": ("w_ff2", "w_pool_out", "w_out")}
BWD_HOSTS = {"d_a": ("w_ff1",), "d_h2": ("w_ff2",),
             "d_w_in": ("w_ssd_out", "w_attn_out", "w_pool_mix", "w_pool_out", "w_out"), "d_h1": ("w_in",)}


def _ride(mmargs, kind, hosts, host, cargo, got):
    if cargo is None:
        return mm(*mmargs)
    out, res = mm(*mmargs, rider=Rider(kind, [cargo[k] for k in hosts[host]]))
    got.update(zip(hosts[host], res))
    return out


def layer_fwd(x, mod, W, sm, tag, cargo=None):
    sh1, sc1, g1, sh2, sc2, g2 = mod
    sv, got = {}, {}
    ride = functools.partial(_ride, kind="gather", hosts=FWD_HOSTS, cargo=cargo, got=got)
    h1 = rms_mod_fwd(x, sm["norm1_w"], sc1, sh1, f"rms1_fwd_{tag}")
    P = ride((h1, W["in"], "nn", F32, f"in_proj_{tag}"), host="in_proj")
    pre, xc = conv_fwd(P, sm["conv_w"], sm["conv_b"], f"conv_fwd_{tag}")
    y, st = ssd_fwd(xc, P, sm["dt_bias"], sm["a_log"], sm["d_skip"], f"ssd_fwd_{tag}")
    ysn = ssd_post_fwd(y, P, sm["ssd_norm_w"], f"ssd_post_fwd_{tag}")
    ys = mm(ysn, W["ssd_out"], "nn", F32, f"ssd_out_{tag}")
    os_, ls_ = [], []
    for gi in range(3):
        o, l = attn_fwd(P, sm["q_norm_w"], sm["k_norm_w"], gi, f"attn_fwd{gi}_{tag}")
        os_.append(o)
        ls_.append(l)
    om = attn_merge_fwd(os_, ls_, f"attn_merge_fwd_{tag}")
    ya = mm(om, W["attn_out"], "nn", F32, f"attn_out_{tag}")
    pooled, ypp = pool_fwd(P, W["pool_mix"], sm["pool_scale"], f"pool_fwd_{tag}")
    yp = mm(ypp, W["pool_out"], "nn", F32, f"pool_out_{tag}")
    m = gate_fwd(P, ys, ya, yp, f"gate_fwd_{tag}")
    mo = mm(m, W["out"], "nn", F32, f"mix_out_{tag}")
    x1 = resid_fwd(x, g1, mo, f"resid1_fwd_{tag}")
    h2 = rms_mod_fwd(x1, sm["norm2_w"], sc2, sh2, f"rms2_fwd_{tag}")
    f = ride((h2, W["ff1"], "nn", F32, f"ff1_{tag}"), host="ff1")
    a = relu2_fwd(f, f"relu2_fwd_{tag}")
    o2 = ride((a, W["ff2"], "nn", F32, f"ff2_{tag}"), host="ff2")
    x2 = resid_fwd(x1, g2, o2, f"resid2_fwd_{tag}")
    sv.update(x=x, h1=h1, P=P, pre=pre, xc=xc, y=y, st=st, ysn=ysn, ys=ys, os=os_, ls=ls_, om=om, ya=ya,
              pooled=pooled, ypp=ypp, yp=yp, m=m, mo=mo, x1=x1, h2=h2, f=f, a=a, o2=o2)
    return x2, sv, got


def layer_bwd(dx2, sv, mod, W, sm, tag, cargo=None):
    sh1, sc1, g1, sh2, sc2, g2 = mod
    S = dx2.shape[0]
    P = sv["P"]
    gw, gs, got = {}, {}, {}
    ride = functools.partial(_ride, kind="scatter", hosts=BWD_HOSTS, cargo=cargo, got=got)
    do2, dg2 = resid_bwd(dx2, g2, sv["o2"], f"resid2_bwd_{tag}")
    gw["ff2"] = mm(sv["a"], do2, "tn", BF16, f"d_w_ff2_{tag}")
    da = ride((do2, W["ff2"], "nt", F32, f"d_a_{tag}"), host="d_a")
    df = relu2_bwd(sv["f"], da, f"relu2_bwd_{tag}")
    gw["ff1"] = mm(sv["h2"], df, "tn", BF16, f"d_w_ff1_{tag}")
    dh2 = ride((df, W["ff1"], "nt", F32, f"d_h2_{tag}"), host="d_h2")
    dx1, gs["norm2_w"], dsc2, dsh2 = rms_mod_bwd(sv["x1"], dh2, dx2, sm["norm2_w"], sc2, f"rms2_bwd_{tag}")
    dmo, dg1 = resid_bwd(dx1, g1, sv["mo"], f"resid1_bwd_{tag}")
    gw["out"] = mm(sv["m"], dmo, "tn", BF16, f"d_w_out_{tag}")
    dm = mm(dmo, W["out"], "nt", F32, f"d_m_{tag}")
    dP = jnp.zeros((S, NP), BF16)
    dP, dys, dya, dyp = gate_bwd(P, sv["ys"], sv["ya"], sv["yp"], dm, dP, f"gate_bwd_{tag}")
    gw["ssd_out"] = mm(sv["ysn"], dys, "tn", BF16, f"d_w_ssd_out_{tag}")
    dysn = mm(dys, W["ssd_out"], "nt", F32, f"d_ysn_{tag}")
    dy, dP, gs["ssd_norm_w"] = ssd_post_bwd(sv["y"], P, dysn, dP, sm["ssd_norm_w"], f"ssd_post_bwd_{tag}")
    dxc, dP, gs["dt_bias"], gs["a_log"], dDl = ssd_bwd(sv["xc"], P, sv["st"], dy, dP, sm["dt_bias"], sm["a_log"],
                                                      sm["d_skip"], f"ssd_bwd_{tag}")
    gs["d_skip"] = dDl.reshape(SSD_HEADS, SSD_HEAD_DIM).sum(axis=1)
    gs["dt_bias"] = gs["dt_bias"][0, :SSD_HEADS]
    gs["a_log"] = gs["a_log"][0, :SSD_HEADS]
    dP, gs["conv_w"], gs["conv_b"] = conv_bwd(dxc, sv["pre"], P, dP, sm["conv_w"], f"conv_bwd_{tag}")
    gw["attn_out"] = mm(sv["om"], dya, "tn", BF16, f"d_w_attn_out_{tag}")
    dom = mm(dya, W["attn_out"], "nt", F32, f"d_om_{tag}")
    mb = attn_merge_bwd(sv["os"], sv["ls"], dom, f"attn_merge_bwd_{tag}")
    dqw = dkw = None
    for gi in range(3):
        dP, a_, b_ = attn_bwd(P, sv["os"][gi], sv["ls"][gi], mb[gi], mb[3 + gi], dP, sm["q_norm_w"], sm["k_norm_w"], gi,
                              f"attn_bwd{gi}_{tag}")
        dqw = a_ if dqw is None else dqw + a_
        dkw = b_ if dkw is None else dkw + b_
    gs["q_norm_w"], gs["k_norm_w"] = dqw, dkw
    gw["pool_out"] = mm(sv["ypp"], dyp, "tn", BF16, f"d_w_pool_out_{tag}")
    dypp = mm(dyp, W["pool_out"], "nt", F32, f"d_ypp_{tag}")
    dP, dwm, gs["pool_scale"] = pool_bwd(dypp, sv["pooled"], dP, W["pool_mix"], sm["pool_scale"], f"pool_bwd_{tag}")
    gw["pool_mix"] = dwm.astype(BF16)
    gw["in"] = ride((sv["h1"], dP, "tn", BF16, f"d_w_in_{tag}"), host="d_w_in")
    dh1 = ride((dP, W["in"], "nt", F32, f"d_h1_{tag}"), host="d_h1")
    dx, gs["norm1_w"], dsc1, dsh1 = rms_mod_bwd(sv["x"], dh1, dx1, sm["norm1_w"], sc1, f"rms1_bwd_{tag}")
    dmod = jnp.concatenate([dsh1, dsc1, dg1, dsh2, dsc2, dg2], axis=1)
    return dx, dmod, gw, gs, got


def _w_in_to_fused(w):
    E = ATTN_HEAD_DIM
    cols = []
    for h in range(ATTN_HEADS):
        for off in (O_Q, O_K, O_V):
            cols.append(w[:, off + h * E: off + (h + 1) * E])
    cols += [w[:, O_XBC:O_DT], w[:, O_G:IN_WIDTH], w[:, O_Z:O_XBC], w[:, O_U:O_G], w[:, O_DT:O_Q],
             jnp.zeros((w.shape[0], DT_W - SSD_HEADS), w.dtype)]
    return jnp.concatenate(cols, axis=1)


def _fused_to_w_in(g):
    E = ATTN_HEAD_DIM
    q = [g[:, h * HEAD_W: h * HEAD_W + E] for h in range(ATTN_HEADS)]
    k = [g[:, h * HEAD_W + E: h * HEAD_W + 2 * E] for h in range(ATTN_HEADS)]
    v = [g[:, h * HEAD_W + 2 * E: (h + 1) * HEAD_W] for h in range(ATTN_HEADS)]
    return jnp.concatenate([g[:, Z_OFF:Z_OFF + Z_W], g[:, XBC_OFF:XBC_OFF + XBC_W], g[:, DT_OFF:DT_OFF + SSD_HEADS]]
                           + q + k + v + [g[:, U_OFF:U_OFF + U_W], g[:, GATE_OFF:GATE_OFF + GATE_W]], axis=1)


def _cols_from_shards(g):
    return jnp.moveaxis(g, 0, 1).reshape(g.shape[1], N_DEV * g.shape[2])


def _cols_to_shards(w):
    R = w.shape[0]
    return jnp.moveaxis(w.reshape(R, N_DEV, w.shape[1] // N_DEV), 1, 0)


BIG = ("w_in", "w_ssd_out", "w_attn_out", "w_pool_mix", "w_pool_out", "w_out", "w_ff1", "w_ff2")
SMALL = ("b_ada", "norm1_w", "norm2_w", "conv_b", "dt_bias", "a_log", "d_skip", "ssd_norm_w", "q_norm_w", "k_norm_w",
         "pool_scale")
WEIGHTS = ("w_ada", "b_ada", "norm1_w", "norm2_w", "w_in", "conv_w", "conv_b", "dt_bias", "a_log", "d_skip",
           "ssd_norm_w", "w_ssd_out", "q_norm_w", "k_norm_w", "w_attn_out", "w_pool_mix", "pool_scale", "w_pool_out",
           "w_out", "w_ff1", "w_ff2")


def _pack(arrs):
    flat = jnp.concatenate([a.reshape(-1).astype(F32) for a in arrs])
    pad = (-flat.shape[0]) % (8 * LANES)
    return jnp.pad(flat, (0, pad)).reshape(-1, LANES)


def _unpack(packed, shapes):
    flat = packed.reshape(-1)
    out, off = [], 0
    for s in shapes:
        n = int(np.prod(s))
        out.append(flat[off:off + n].reshape(s))
        off += n
    return out


def _pad_lanes(v):
    return jnp.pad(v, ((0, 0), (0, LANES - v.shape[1])))[:, None, :]


def _step(x, c, target, p, m_, v_):
    Ld = p["w_in"].shape[0]
    me = 4 * lax.axis_index("x") + 2 * lax.axis_index("y") + lax.axis_index("c")

    def local_blocks(l):
        return {k: p[k][l].astype(BF16) for k in BIG}

    blk = local_blocks(0)
    gathered = all_gather([blk[k] for k in BIG] + [c, p["conv_w"]], "gather_first")
    G = dict(zip(BIG, gathered[:len(BIG)]))
    c_all = gathered[len(BIG)].reshape(N_DEV, D_MODEL)
    conv_w_full = jnp.moveaxis(gathered[len(BIG) + 1], 0, 2).reshape(Ld, 4, XBC_W)

    def layer_weights(G):
        W = {}
        W["in"] = _w_in_to_fused(_cols_from_shards(G["w_in"]))
        W["ssd_out"] = G["w_ssd_out"].reshape(1024, D_MODEL)
        W["attn_out"] = _cols_from_shards(G["w_attn_out"])
        W["pool_mix"] = jnp.moveaxis(G["w_pool_mix"], 0, 1).reshape(4, 256, 256)
        W["pool_out"] = G["w_pool_out"].reshape(1024, D_MODEL)
        W["out"] = G["w_out"].reshape(D_MODEL, D_MODEL)
        W["ff1"] = _cols_from_shards(G["w_ff1"])
        W["ff2"] = G["w_ff2"].reshape(4 * D_MODEL, D_MODEL)
        return W

    def slabs(gw):
        return {
            "w_in": _cols_to_shards(_fused_to_w_in(gw["in"])),
            "w_ssd_out": gw["ssd_out"].reshape(N_DEV, -1, D_MODEL),
            "w_attn_out": _cols_to_shards(gw["attn_out"]),
            "w_pool_mix": jnp.moveaxis(gw["pool_mix"].reshape(4, N_DEV, 32, 256), 1, 0),
            "w_pool_out": gw["pool_out"].reshape(N_DEV, -1, D_MODEL),
            "w_out": gw["out"].reshape(N_DEV, -1, D_MODEL),
            "w_ff1": _cols_to_shards(gw["ff1"]),
            "w_ff2": gw["ff2"].reshape(N_DEV, -1, D_MODEL),
        }

    def layer_small(l):
        sm = {k: p[k][l][None, :] for k in ("norm1_w", "norm2_w", "conv_b", "ssd_norm_w", "q_norm_w", "k_norm_w",
                                            "pool_scale")}
        for k in ("dt_bias", "a_log", "d_skip"):
            sm[k] = jnp.pad(p[k][l], (0, LANES - SSD_HEADS))[None, :]
        sm["conv_w"] = conv_w_full[l]
        return sm

    cond = silu_rows(c_all, "cond_silu")
    ncol = p["w_ada"].shape[2]
    b_mine = lax.dynamic_slice_in_dim(p["b_ada"], me * ncol, ncol, axis=1)
    parts = [mm(cond, p["w_ada"][l], "nn", F32, f"ada_{l}") + b_mine[l][None, :] for l in range(Ld)]
    mod_cols = jnp.stack(parts, axis=1)
    mod = all_to_all([mod_cols], "scatter_mod")[0]
    mod = jnp.moveaxis(mod, 0, 1).reshape(Ld, 6 * D_MODEL)

    def mods(l):
        return [mod[l, k * D_MODEL:(k + 1) * D_MODEL][None, :] for k in range(6)]

    saved, Ws, sms = [], [], []
    h = x
    for l in range(Ld):
        Ws.append(layer_weights(G))
        sms.append(layer_small(l))
        h, sv, G = layer_fwd(h, mods(l), Ws[l], sms[l], f"l{l}", local_blocks(l + 1) if l + 1 < Ld else None)
        saved.append(sv)
    lsum, dh = loss_fwd_bwd(h, target, "loss")
    loss = 0.5 / D_MODEL * jnp.sum(lsum)
    gss, dmods, recvs = [None] * Ld, [None] * Ld, [None] * Ld
    cargo = None
    for l in reversed(range(Ld)):
        dh, dmods[l], gw, gss[l], got = layer_bwd(dh, saved[l], mods(l), Ws[l], sms[l], f"l{l}", cargo)
        if cargo is not None:
            recvs[l + 1] = got
        cargo = slabs(gw)
    dmod = jnp.concatenate(dmods, axis=0)
    dmod_cols = jnp.moveaxis(dmod.reshape(Ld, N_DEV, ncol), 1, 0)
    last = all_to_all([cargo[k] for k in BIG] + [dmod_cols], "scatter_last")
    recvs[0] = dict(zip(BIG, last[:len(BIG)]))
    dmod_all = last[len(BIG)]

    out_g, out_d, out_m, out_v = {}, {}, {}, {}

    def finish(k, res):
        shp = p[k].shape
        out_g[k], out_d[k], out_m[k], out_v[k] = [t.reshape(shp) for t in res]

    for k in BIG:
        C = p[k].shape[-1]
        rows = int(np.prod(p[k].shape[1:-1]))
        res = None
        for l in reversed(range(Ld)):
            res = adamw(recvs[l][k].reshape(N_DEV, rows, C), p[k].reshape(-1, C), m_[k].reshape(-1, C),
                        v_[k].reshape(-1, C), f"adamw_{k}_l{l}", row0=l * rows, prev=res)
        finish(k, res)
    g_ada = jnp.stack([mm(cond, dmod_all[:, l], "tn", F32, f"d_w_ada_{l}") for l in range(Ld)], axis=0)
    finish("w_ada", adamw(g_ada.reshape(1, -1, ncol), p["w_ada"].reshape(-1, ncol), m_["w_ada"].reshape(-1, ncol),
                          v_["w_ada"].reshape(-1, ncol), "adamw_w_ada"))

    small_g = {k: jnp.stack([gss[l][k].reshape(p[k].shape[1:]) for l in range(Ld)], axis=0) for k in SMALL if k != "b_ada"}
    small_g["b_ada"] = dmod
    conv_g = jnp.stack([gss[l]["conv_w"] for l in range(Ld)], axis=0)
    packed = _pack([small_g[k] for k in SMALL] + [conv_g])
    parts_small = all_gather([packed], "gather_small_grads")[0]
    zeros_conv = jnp.zeros(conv_g.shape, F32)
    res = adamw(parts_small, _pack([p[k] for k in SMALL] + [zeros_conv]), _pack([m_[k] for k in SMALL] + [zeros_conv]),
                _pack([v_[k] for k in SMALL] + [zeros_conv]), "adamw_small")
    shapes = [p[k].shape for k in SMALL] + [conv_g.shape]
    unp = [_unpack(t, shapes) for t in res]
    for i, k in enumerate(SMALL):
        out_g[k], out_d[k], out_m[k], out_v[k] = unp[0][i], unp[1][i], unp[2][i], unp[3][i]
    cw = p["conv_w"].shape[-1]
    conv_mine = lax.dynamic_slice_in_dim(unp[0][len(SMALL)], me * cw, cw, axis=2)
    finish("conv_w", adamw(conv_mine.reshape(1, -1, cw), p["conv_w"].reshape(-1, cw), m_["conv_w"].reshape(-1, cw),
                           v_["conv_w"].reshape(-1, cw), "adamw_conv_w"))

    loss = lax.psum(loss, ("x", "y", "c"))
    return loss, dh, out_g, out_d, out_m, out_v


def kernel(x, c, w_ada, b_ada, norm1_w, norm2_w, w_in, conv_w, conv_b, dt_bias, a_log, d_skip, ssd_norm_w, w_ssd_out, q_norm_w, k_norm_w, w_attn_out, w_pool_mix, pool_scale, w_pool_out, w_out, w_ff1, w_ff2, loss_target, m_w_ada, m_b_ada, m_norm1_w, m_norm2_w, m_w_in, m_conv_w, m_conv_b, m_dt_bias, m_a_log, m_d_skip, m_ssd_norm_w, m_w_ssd_out, m_q_norm_w, m_k_norm_w, m_w_attn_out, m_w_pool_mix, m_pool_scale, m_w_pool_out, m_w_out, m_w_ff1, m_w_ff2, v_w_ada, v_b_ada, v_norm1_w, v_norm2_w, v_w_in, v_conv_w, v_conv_b, v_dt_bias, v_a_log, v_d_skip, v_ssd_norm_w, v_w_ssd_out, v_q_norm_w, v_k_norm_w, v_w_attn_out, v_w_pool_mix, v_pool_scale, v_w_pool_out, v_w_out, v_w_ff1, v_w_ff2):
    p = dict(w_ada=w_ada, b_ada=b_ada, norm1_w=norm1_w, norm2_w=norm2_w, w_in=w_in, conv_w=conv_w, conv_b=conv_b,
             dt_bias=dt_bias, a_log=a_log, d_skip=d_skip, ssd_norm_w=ssd_norm_w, w_ssd_out=w_ssd_out, q_norm_w=q_norm_w,
             k_norm_w=k_norm_w, w_attn_out=w_attn_out, w_pool_mix=w_pool_mix, pool_scale=pool_scale,
             w_pool_out=w_pool_out, w_out=w_out, w_ff1=w_ff1, w_ff2=w_ff2)
    m_ = dict(w_ada=m_w_ada, b_ada=m_b_ada, norm1_w=m_norm1_w, norm2_w=m_norm2_w, w_in=m_w_in, conv_w=m_conv_w,
              conv_b=m_conv_b, dt_bias=m_dt_bias, a_log=m_a_log, d_skip=m_d_skip, ssd_norm_w=m_ssd_norm_w,
              w_ssd_out=m_w_ssd_out, q_norm_w=m_q_norm_w, k_norm_w=m_k_norm_w, w_attn_out=m_w_attn_out,
              w_pool_mix=m_w_pool_mix, pool_scale=m_pool_scale, w_pool_out=m_w_pool_out, w_out=m_w_out, w_ff1=m_w_ff1,
              w_ff2=m_w_ff2)
    v_ = dict(w_ada=v_w_ada, b_ada=v_b_ada, norm1_w=v_norm1_w, norm2_w=v_norm2_w, w_in=v_w_in, conv_w=v_conv_w,
              conv_b=v_conv_b, dt_bias=v_dt_bias, a_log=v_a_log, d_skip=v_d_skip, ssd_norm_w=v_ssd_norm_w,
              w_ssd_out=v_w_ssd_out, q_norm_w=v_q_norm_w, k_norm_w=v_k_norm_w, w_attn_out=v_w_attn_out,
              w_pool_mix=v_w_pool_mix, pool_scale=v_pool_scale, w_pool_out=v_w_pool_out, w_out=v_w_out, w_ff1=v_w_ff1,
              w_ff2=v_w_ff2)
    loss, dx, g, d, nm, nv = _step(x[0], c, loss_target[0], p, m_, v_)
    return (loss, dx[None], *[g[k] for k in WEIGHTS], *[d[k] for k in WEIGHTS], *[nm[k] for k in WEIGHTS],
            *[nv[k] for k in WEIGHTS])
```

```python
import functools
import math

import numpy as np
import jax
import jax.numpy as jnp
from jax import lax
from jax.experimental import pallas as pl
from jax.experimental.pallas import tpu as pltpu

F32 = jnp.float32
BF16 = jnp.bfloat16
MESH = pl.DeviceIdType.MESH
HIGHEST = lax.Precision.HIGHEST

N_DEV = 8
V7X_VMEM_LIMIT_BYTES = 56 * 1024 * 1024
LANES = 128
ROW_TILE = 256

EPS = 1e-6
D_MODEL = 1024
SSD_HEADS = 16
SSD_HEAD_DIM = 64
SSD_CHUNK = 128
SSD_STATE = 128
ATTN_HEADS = 12
ATTN_HEAD_DIM = 128
ATTN_STEPS = 128
DILATIONS = (1, 4, 16)
POOL_WINDOWS = (2, 4, 8, 16)
POOL_HALO = 16
CONV_HALO = 8
NEG = -1e30

ADAM_LR = 0.001
ADAM_B1 = 0.9
ADAM_B2 = 0.999
ADAM_EPS = 1e-08
ADAM_WD = 0.01
ADAM_STEP = 10

QKV_OFF, QKV_W = 0, 4608
XBC_OFF, XBC_W = 4608, 1536
GATE_OFF, GATE_W = 6144, 3072
Z_OFF, Z_W = 9216, 1024
U_OFF, U_W = 10240, 1024
DT_OFF, DT_W = 11264, 256
NP = 11520
IN_WIDTH = 11280
O_Z, O_XBC, O_DT, O_Q, O_K, O_V, O_U, O_G = 0, 1024, 2560, 2576, 4112, 5648, 7184, 8208


def _alibi_slopes(n):
    def pow2(k):
        start = 2.0 ** (-8.0 / k)
        return [start ** (i + 1) for i in range(k)]
    if math.log2(n).is_integer():
        s = pow2(n)
    else:
        c = 2 ** math.floor(math.log2(n))
        s = pow2(c) + pow2(2 * c)[0::2][: n - c]
    return np.sort(np.asarray(s, np.float32))[::-1].copy()


SLOPES = _alibi_slopes(ATTN_HEADS).reshape(3, 4)


def _cp(sem=None):
    return pltpu.CompilerParams(dimension_semantics=sem, vmem_limit_bytes=V7X_VMEM_LIMIT_BYTES)


def _pick(dim, cap):
    if dim <= cap:
        return dim
    t = cap - cap % LANES
    while t >= LANES:
        if dim % t == 0:
            return t
        t -= LANES
    raise ValueError((dim, cap))


def _sigmoid(x):
    return 1.0 / (1.0 + jnp.exp(-x))


def _silu(x):
    return x * _sigmoid(x)


def _dsilu(x):
    s = _sigmoid(x)
    return s * (1.0 + x * (1.0 - s))


def _softplus(x):
    return jnp.maximum(x, 0.0) + jnp.log(1.0 + jnp.exp(-jnp.abs(x)))


def _dot(a, b, dims):
    return lax.dot_general(a, b, (dims, ((), ())), preferred_element_type=F32)


def _nn(a, b):
    return _dot(a, b, ((1,), (0,)))


def _nt(a, b):
    return _dot(a, b, ((1,), (1,)))


def _tn(a, b):
    return _dot(a, b, ((0,), (0,)))


def _iota(shape, axis):
    return lax.broadcasted_iota(jnp.int32, shape, axis)


def _position():
    return lax.axis_index("x"), lax.axis_index("y"), lax.axis_index("c")


def _index(px, py, pc):
    return 4 * px + 2 * py + pc


def _remote(src, dst, send_sems, recv_sems, a, k, to):
    return pltpu.make_async_remote_copy(src_ref=src, dst_ref=dst, send_sem=send_sems.at[a, k], recv_sem=recv_sems.at[a, k],
                                        device_id=to, device_id_type=MESH)


def _gather_plan(srcs, dsts, send_sems, recv_sems, local_sems):
    x, y, c = _position()
    me, sib = (x, y, c), (x, y, 1 - c)
    chips = [(1 - x, y), (x, 1 - y), (1 - x, 1 - y)]
    n = len(srcs)

    def start():
        for a in range(n):
            mine = dsts[a].at[_index(*me)]
            pltpu.make_async_copy(srcs[a], mine, local_sems.at[a]).start()
            _remote(srcs[a], mine, send_sems, recv_sems, a, 0, sib).start()
            for j, chip in enumerate(chips):
                _remote(srcs[a], mine, send_sems, recv_sems, a, 1 + j, (*chip, c)).start()

    def finish():
        for a in range(n):
            for j, chip in enumerate(chips):
                blk = dsts[a].at[_index(*chip, c)]
                _remote(blk, blk, send_sems, recv_sems, a, 1 + j, me).wait_recv()
                _remote(blk, blk, send_sems, recv_sems, a, 4 + j, sib).start()
        for a in range(n):
            mine = dsts[a].at[_index(*me)]
            blk = dsts[a].at[_index(*sib)]
            _remote(blk, blk, send_sems, recv_sems, a, 0, me).wait_recv()
            for j, chip in enumerate(chips):
                blk = dsts[a].at[_index(*chip, 1 - c)]
                _remote(blk, blk, send_sems, recv_sems, a, 4 + j, me).wait_recv()
            for k in range(N_DEV - 1):
                _remote(mine, mine, send_sems, recv_sems, a, k, me).wait_send()
            pltpu.make_async_copy(srcs[a], mine, local_sems.at[a]).wait()

    return start, finish


def _scatter_plan(srcs, dsts, send_sems, recv_sems, local_sems):
    x, y, c = _position()
    me = _index(x, y, c)
    n = len(srcs)
    copies = []
    for a in range(n):
        copies.append(pltpu.make_async_copy(srcs[a].at[me], dsts[a].at[me], local_sems.at[a]))
        for k in range(1, N_DEV):
            px, py, pc = x ^ ((k >> 2) & 1), y ^ ((k >> 1) & 1), c ^ (k & 1)
            copies.append(_remote(srcs[a].at[_index(px, py, pc)], dsts[a].at[me], send_sems, recv_sems, a, k - 1, (px, py, pc)))

    def start():
        for cp in copies:
            cp.start()

    def finish():
        for cp in copies:
            cp.wait()

    return start, finish


class Rider:
    def __init__(self, kind, arrs):
        self.kind, self.arrs = kind, list(arrs)

    def out_shapes(self):
        if self.kind == "gather":
            return [pltpu.HBM((N_DEV,) + a.shape, a.dtype) for a in self.arrs]
        return [pltpu.HBM(a.shape, a.dtype) for a in self.arrs]

    def sems(self):
        n = len(self.arrs)
        return [pltpu.SemaphoreType.DMA((n, N_DEV - 1)), pltpu.SemaphoreType.DMA((n, N_DEV - 1)),
                pltpu.SemaphoreType.DMA((n,))]

    def plan(self, srcs, dsts, sems):
        return (_gather_plan if self.kind == "gather" else _scatter_plan)(srcs, dsts, *sems)


def _exchange(kind, arrs, name):
    rd = Rider(kind, arrs)
    n = len(arrs)

    def body(*refs):
        start, finish = rd.plan(refs[:n], refs[n:2 * n], refs[2 * n:])
        start()
        finish()

    hbm = pl.BlockSpec(memory_space=pltpu.HBM)
    return pl.pallas_call(
        body, in_specs=[hbm] * n, out_specs=[hbm] * n, out_shape=rd.out_shapes(), scratch_shapes=rd.sems(),
        name=name, compiler_params=pltpu.CompilerParams(has_side_effects=True))(*arrs)


def all_gather(arrs, name):
    return _exchange("gather", arrs, name)


def all_to_all(arrs, name):
    return _exchange("scatter", arrs, name)


def mm(a, b, mode, out_dtype, name, rider=None, extra=(), epi=None, out_dtypes=None, tm_cap=1024, tn_cap=1280,
       tk_cap=1024):
    if mode == "nn":
        (M, K), (K2, N) = a.shape, b.shape
    elif mode == "nt":
        (M, K), (N, K2) = a.shape, b.shape
    else:
        (K, M), (K2, N) = a.shape, b.shape
    assert K == K2, (a.shape, b.shape, mode)
    tm, tn, tk = _pick(M, tm_cap), _pick(N, tn_cap), _pick(K, tk_cap)
    nk = K // tk
    if mode == "tn":
        a_spec = pl.BlockSpec((tk, tm), lambda i, j, k: (k, i))
    else:
        a_spec = pl.BlockSpec((tm, tk), lambda i, j, k: (i, k))
    if mode == "nt":
        b_spec = pl.BlockSpec((tn, tk), lambda i, j, k: (j, k))
    else:
        b_spec = pl.BlockSpec((tk, tn), lambda i, j, k: (k, j))
    dims = {"nn": ((1,), (0,)), "nt": ((1,), (1,)), "tn": ((0,), (0,))}[mode]
    o_spec = pl.BlockSpec((tm, tn), lambda i, j, k: (i, j))
    v_spec = pl.BlockSpec((1, tn), lambda i, j, k: (0, j))
    grid = (M // tm, N // tn, nk)
    out_dtypes = list(out_dtypes) if out_dtypes is not None else [out_dtype]
    nr = len(rider.arrs) if rider is not None else 0
    ne, no = len(extra), len(out_dtypes)
    extra_specs = []
    for e in extra:
        assert e.shape in ((M, N), (1, N)), (e.shape, M, N)
        extra_specs.append(o_spec if e.shape[0] == M and M != 1 else v_spec)

    def body(*refs):
        a_ref, b_ref = refs[0], refs[1]
        e_refs = refs[2:2 + ne]
        srcs = refs[2 + ne:2 + ne + nr]
        o_refs = refs[2 + ne + nr:2 + ne + nr + no]
        dsts = refs[2 + ne + nr + no:2 + ne + 2 * nr + no]
        rest = refs[2 + ne + 2 * nr + no:]
        i, j, k = pl.program_id(0), pl.program_id(1), pl.program_id(2)
        if nr:
            start, finish = rider.plan(srcs, dsts, rest[-3:])

            @pl.when(jnp.logical_and(jnp.logical_and(i == 0, j == 0), k == 0))
            def _():
                start()

        def write(r):
            vals = epi(r, *[e[...] for e in e_refs]) if epi is not None else (r,)
            for o_ref, val, dt in zip(o_refs, vals, out_dtypes):
                o_ref[...] = val.astype(dt)

        prod = _dot(a_ref[...].astype(BF16), b_ref[...].astype(BF16), dims)
        if nk == 1:
            write(prod)
        else:
            acc = rest[0]

            @pl.when(k == 0)
            def _():
                acc[...] = prod

            @pl.when(k > 0)
            def _():
                acc[...] += prod

            @pl.when(k == nk - 1)
            def _():
                write(acc[...])

        if nr:
            @pl.when(jnp.logical_and(jnp.logical_and(i == grid[0] - 1, j == grid[1] - 1), k == nk - 1))
            def _():
                finish()

    hbm = pl.BlockSpec(memory_space=pltpu.HBM)
    out_shape = [jax.ShapeDtypeStruct((M, N), dt) for dt in out_dtypes]
    scratch = [pltpu.VMEM((tm, tn), F32)] if nk > 1 else []
    if nr:
        res = pl.pallas_call(
            body, grid=grid, in_specs=[a_spec, b_spec] + extra_specs + [hbm] * nr, out_specs=[o_spec] * no + [hbm] * nr,
            out_shape=out_shape + rider.out_shapes(), scratch_shapes=scratch + rider.sems(), name=name,
            compiler_params=pltpu.CompilerParams(dimension_semantics=("arbitrary",) * 3,
                                                 vmem_limit_bytes=V7X_VMEM_LIMIT_BYTES, has_side_effects=True),
        )(a, b, *extra, *rider.arrs)
        outs, got = list(res[:no]), list(res[no:])
        return (outs[0] if no == 1 else outs), got
    res = pl.pallas_call(
        body, grid=grid, in_specs=[a_spec, b_spec] + extra_specs, out_specs=[o_spec] * no, out_shape=out_shape,
        scratch_shapes=scratch, name=name, compiler_params=_cp(("parallel", "parallel", "arbitrary")))(a, b, *extra)
    return res[0] if no == 1 else list(res)


def _rows(tile, w, cb=0):
    return pl.BlockSpec((tile, w), lambda i: (i, cb))


def _vec(shape):
    nd = len(shape)
    return pl.BlockSpec(shape, lambda i: (0,) * nd)


def _any():
    return pl.BlockSpec(memory_space=pl.ANY)


def _sds(shape, dtype):
    return jax.ShapeDtypeStruct(shape, dtype)


def _rowcall(body, n_rows, tile, in_specs, out_specs, out_shape, name, aliases=None):
    return pl.pallas_call(
        body, grid=(n_rows // tile,), in_specs=in_specs, out_specs=out_specs, out_shape=out_shape,
        name=name, input_output_aliases=aliases or {}, compiler_params=_cp(("arbitrary",)))


def _zero_at_first(step, *refs):
    @pl.when(step == 0)
    def _():
        for r in refs:
            r[...] = jnp.zeros_like(r)


def rms_mod_fwd(x, nw, sc, sh, name):
    S, D = x.shape
    T = ROW_TILE

    def body(x_ref, nw_ref, sc_ref, sh_ref, h_ref):
        xv = x_ref[...]
        r = lax.rsqrt(jnp.mean(xv * xv, axis=-1, keepdims=True) + EPS)
        y = xv * r * nw_ref[...]
        h_ref[...] = (y * (1.0 + sc_ref[...]) + sh_ref[...]).astype(BF16)

    return _rowcall(body, S, T, [_rows(T, D), _vec((1, D)), _vec((1, D)), _vec((1, D))], _rows(T, D),
                    _sds((S, D), BF16), name)(x, nw, sc, sh)


def rms_mod_bwd(x, dh, dres, nw, sc, name):
    S, D = x.shape
    T = ROW_TILE

    def body(x_ref, dh_ref, dres_ref, nw_ref, sc_ref, dx_ref, dnw_ref, dsc_ref, dsh_ref):
        _zero_at_first(pl.program_id(0), dnw_ref, dsc_ref, dsh_ref)
        xv = x_ref[...]
        d = dh_ref[...]
        r = lax.rsqrt(jnp.mean(xv * xv, axis=-1, keepdims=True) + EPS)
        xh = xv * r
        nwv = nw_ref[...]
        dsh_ref[...] += jnp.sum(d, axis=0, keepdims=True)
        dsc_ref[...] += jnp.sum(d * (xh * nwv), axis=0, keepdims=True)
        dy = d * (1.0 + sc_ref[...])
        dnw_ref[...] += jnp.sum(dy * xh, axis=0, keepdims=True)
        dxh = dy * nwv
        dx_ref[...] = dres_ref[...] + r * (dxh - xh * jnp.mean(dxh * xh, axis=-1, keepdims=True))

    v = _vec((1, D))
    return _rowcall(body, S, T, [_rows(T, D), _rows(T, D), _rows(T, D), v, v],
                    [_rows(T, D), v, v, v],
                    [_sds((S, D), F32), _sds((1, D), F32), _sds((1, D), F32), _sds((1, D), F32)], name)(x, dh, dres, nw, sc)


def _shift_down(tile, halo, k):
    if k == 0:
        return tile
    hr = halo.shape[0]
    rolled = pltpu.roll(tile, k, 0)
    rh = pltpu.roll(halo, k, 0)
    rid = _iota(halo.shape, 0)
    first = jnp.where(rid < k, rh, rolled[:hr])
    return jnp.concatenate([first, rolled[hr:]], axis=0)


def _shift_up(tile, nxt, k):
    if k == 0:
        return tile
    T = tile.shape[0]
    hr = nxt.shape[0]
    rolled = pltpu.roll(tile, T - k, 0)
    rn = pltpu.roll(nxt, hr - k, 0)
    rid = _iota(nxt.shape, 0)
    last = jnp.where(rid >= hr - k, rn, rolled[T - hr:])
    return jnp.concatenate([rolled[:T - hr], last], axis=0)


CONV_CB = 512


def conv_fwd(P, conv_w, conv_b, name):
    S = P.shape[0]
    T = ROW_TILE
    nj = XBC_W // CONV_CB
    cb0 = XBC_OFF // CONV_CB
    hb = T // CONV_HALO

    def body(x_ref, h_ref, w_ref, b_ref, pre_ref, xc_ref):
        i = pl.program_id(0)
        xb = x_ref[...]
        xh = jnp.where(i > 0, h_ref[...], 0.0)
        acc = b_ref[...] + w_ref[3:4, :] * xb
        for idx in range(3):
            acc = acc + w_ref[idx:idx + 1, :] * _shift_down(xb, xh, 3 - idx)
        pre_ref[...] = acc
        xc_ref[...] = _silu(acc)

    blk = pl.BlockSpec((T, CONV_CB), lambda i, j: (i, j))
    return pl.pallas_call(
        body, grid=(S // T, nj),
        in_specs=[pl.BlockSpec((T, CONV_CB), lambda i, j: (i, cb0 + j)),
                  pl.BlockSpec((CONV_HALO, CONV_CB), lambda i, j: (jnp.maximum(i * hb - 1, 0), cb0 + j)),
                  pl.BlockSpec((4, CONV_CB), lambda i, j: (0, j)),
                  pl.BlockSpec((1, CONV_CB), lambda i, j: (0, j))],
        out_specs=[blk, blk], out_shape=[_sds((S, XBC_W), F32), _sds((S, XBC_W), F32)], name=name,
        compiler_params=_cp(("arbitrary", "arbitrary")))(P, P, conv_w, conv_b)


def conv_bwd(dxc, pre, P, dP, conv_w, name):
    S = P.shape[0]
    T = ROW_TILE
    nt = S // T
    nj = XBC_W // CONV_CB
    cb0 = XBC_OFF // CONV_CB
    hb = T // CONV_HALO

    def body(d_ref, dn_ref, p_ref, pn_ref, x_ref, xh_ref, w_ref, dp_in, dp_ref, dw_ref, db_ref):
        i = pl.program_id(1)
        _zero_at_first(i, dw_ref, db_ref)
        dpre = d_ref[...] * _dsilu(p_ref[...])
        dpn = jnp.where(i < nt - 1, dn_ref[...] * _dsilu(pn_ref[...]), 0.0)
        db_ref[...] += jnp.sum(dpre, axis=0, keepdims=True)
        xb = x_ref[...]
        xh = jnp.where(i > 0, xh_ref[...], 0.0)
        dx = jnp.zeros_like(dpre)
        for idx in range(4):
            k = 3 - idx
            dw_ref[idx:idx + 1, :] += jnp.sum(dpre * _shift_down(xb, xh, k), axis=0, keepdims=True)
            dx = dx + w_ref[idx:idx + 1, :] * _shift_up(dpre, dpn, k)
        dp_ref[...] = dx.astype(BF16)

    cur = pl.BlockSpec((T, CONV_CB), lambda j, i: (i, j))
    nxt = pl.BlockSpec((CONV_HALO, CONV_CB), lambda j, i: (jnp.minimum((i + 1) * hb, S // CONV_HALO - 1), j))
    return pl.pallas_call(
        body, grid=(nj, nt),
        in_specs=[cur, nxt, cur, nxt,
                  pl.BlockSpec((T, CONV_CB), lambda j, i: (i, cb0 + j)),
                  pl.BlockSpec((CONV_HALO, CONV_CB), lambda j, i: (jnp.maximum(i * hb - 1, 0), cb0 + j)),
                  pl.BlockSpec((4, CONV_CB), lambda j, i: (0, j)), _any()],
        out_specs=[pl.BlockSpec((T, CONV_CB), lambda j, i: (i, cb0 + j)),
                   pl.BlockSpec((4, CONV_CB), lambda j, i: (0, j)),
                   pl.BlockSpec((1, CONV_CB), lambda j, i: (0, j))],
        out_shape=[_sds(dP.shape, BF16), _sds((4, XBC_W), F32), _sds((1, XBC_W), F32)],
        input_output_aliases={7: 0}, name=name,
        compiler_params=_cp(("arbitrary", "arbitrary")))(dxc, dxc, pre, pre, P, P, conv_w, dP)


def _col(v, h, lane):
    return jnp.sum(jnp.where(lane == h, v, 0.0), axis=1, keepdims=True)


def _row(v, h, rowi):
    return jnp.sum(jnp.where(rowi == h, v, 0.0), axis=0, keepdims=True)


def _ssd_common(dt_ref, dtb_ref, al_ref):
    L = SSD_CHUNK
    lane = _iota((L, L), 1)
    rowi = _iota((L, L), 0)
    dtr = dt_ref[...] + dtb_ref[...]
    dt = _softplus(dtr)
    A = -jnp.exp(al_ref[...])
    a = dt * A
    tril = (rowi >= lane).astype(F32)
    ac = jnp.dot(tril, a, precision=HIGHEST, preferred_element_type=F32)
    acT = ac.T
    last = _row(ac, L - 1, rowi)
    return dict(lane=lane, rowi=rowi, dtr=dtr, dt=dt, A=A, ac=ac, acT=acT, last=last,
                eac=jnp.exp(ac), fac=jnp.exp(last - ac), cd=jnp.exp(last),
                lo=lane < SSD_HEAD_DIM, causal=rowi >= lane)


def _pair(v, h0, q):
    lane = q["lane"][:v.shape[0]]
    lo = q["lo"][:v.shape[0]]
    return jnp.where(lo, _col(v, h0, lane), _col(v, h0 + 1, lane))


def _decay(q, h):
    seg = _col(q["ac"], h, q["lane"]) - _row(q["acT"], h, q["rowi"])
    return jnp.exp(jnp.where(q["causal"], seg, NEG))


def ssd_fwd(xc, P, dtb, alog, dsk, name):
    S = xc.shape[0]
    L = SSD_CHUNK
    nc = S // L

    def body(xs_ref, b_ref, c_ref, dt_ref, dtb_ref, al_ref, dk_ref, y_ref, st_ref, H):
        @pl.when(pl.program_id(0) == 0)
        def _():
            H[...] = jnp.zeros_like(H)

        st_ref[0] = H[...]
        q = _ssd_common(dt_ref, dtb_ref, al_ref)
        dk = dk_ref[...]
        for g in range(2):
            Bg = b_ref[:, g * L:(g + 1) * L].astype(BF16)
            Cg = c_ref[:, g * L:(g + 1) * L].astype(BF16)
            cb = _nt(Cg, Bg)
            for jj in range(4):
                j = g * 4 + jj
                h0 = 2 * j
                sl = slice(j * L, (j + 1) * L)
                xs_p = xs_ref[:, sl]
                X = xs_p * _pair(q["dt"], h0, q)
                Xb = X.astype(BF16)
                G0 = (cb * _decay(q, h0)).astype(BF16)
                G1 = (cb * _decay(q, h0 + 1)).astype(BF16)
                Yd = jnp.where(q["lo"], _nn(G0, Xb), _nn(G1, Xb))
                Hp = H[:, sl]
                Yo = _nn(Cg, Hp.astype(BF16)) * _pair(q["eac"], h0, q)
                y_ref[:, sl] = Yd + Yo + _pair(dk, h0, q) * xs_p
                Xd = (X * _pair(q["fac"], h0, q)).astype(BF16)
                H[:, sl] = _pair(q["cd"], h0, q) * Hp + _tn(Bg, Xd)

    v = _vec((1, LANES))
    return pl.pallas_call(
        body, grid=(nc,),
        in_specs=[pl.BlockSpec((L, 1024), lambda c: (c, 0)),
                  pl.BlockSpec((L, 256), lambda c: (c, 4)),
                  pl.BlockSpec((L, 256), lambda c: (c, 5)),
                  pl.BlockSpec((L, LANES), lambda c: (c, DT_OFF // LANES)), v, v, v],
        out_specs=[pl.BlockSpec((L, 1024), lambda c: (c, 0)),
                   pl.BlockSpec((1, SSD_STATE, 1024), lambda c: (c, 0, 0))],
        out_shape=[_sds((S, 1024), F32), _sds((nc, SSD_STATE, 1024), F32)],
        scratch_shapes=[pltpu.VMEM((SSD_STATE, 1024), F32)], name=name,
        compiler_params=_cp(("arbitrary",)))(xc, xc, xc, P, dtb, alog, dsk)


def ssd_bwd(xc, P, st, dy, dP, dtb, alog, dsk, name):
    S = xc.shape[0]
    L = SSD_CHUNK
    nc = S // L

    def body(xs_ref, b_ref, c_ref, dt_ref, st_ref, dy_ref, dtb_ref, al_ref, dk_ref, dp_in,
             dxc_ref, dp_ref, ddtb_ref, dal_ref, dD_ref, dH):
        step = pl.program_id(0)

        @pl.when(step == 0)
        def _():
            dH[...] = jnp.zeros_like(dH)

        _zero_at_first(step, ddtb_ref, dal_ref, dD_ref)
        q = _ssd_common(dt_ref, dtb_ref, al_ref)
        lane, rowi, lo = q["lane"], q["rowi"], q["lo"]
        lane1 = lane[:1]
        lo1 = lo[:1]
        dk = dk_ref[...]
        dac = jnp.zeros((L, L), F32)
        dacT = jnp.zeros((L, L), F32)
        dlast = jnp.zeros((1, L), F32)
        ddt = jnp.zeros((L, L), F32)

        def put(h0, s0, s1):
            return jnp.where(lane == h0, s0, 0.0) + jnp.where(lane == h0 + 1, s1, 0.0)

        def put1(h0, s0, s1):
            return jnp.where(lane1 == h0, s0, 0.0) + jnp.where(lane1 == h0 + 1, s1, 0.0)

        def hsum(v):
            return (jnp.sum(jnp.where(lo, v, 0.0), axis=1, keepdims=True),
                    jnp.sum(jnp.where(lo, 0.0, v), axis=1, keepdims=True))

        for g in range(2):
            Bg = b_ref[:, g * L:(g + 1) * L].astype(BF16)
            Cg = c_ref[:, g * L:(g + 1) * L].astype(BF16)
            cb = _nt(Cg, Bg)
            dcb = jnp.zeros((L, L), F32)
            dCg = jnp.zeros((L, L), F32)
            dBg = jnp.zeros((L, L), F32)
            for jj in range(4):
                j = g * 4 + jj
                h0 = 2 * j
                sl = slice(j * L, (j + 1) * L)
                xs_p = xs_ref[:, sl]
                dY = dy_ref[:, sl]
                dYb = dY.astype(BF16)
                dtp = _pair(q["dt"], h0, q)
                X = xs_p * dtp
                e_p = _pair(q["eac"], h0, q)
                f_p = _pair(q["fac"], h0, q)
                cd_p = _pair(q["cd"], h0, q)
                Hp = st_ref[0, :, sl]
                Hpb = Hp.astype(BF16)
                dHn = dH[:, sl]
                dHnb = dHn.astype(BF16)
                dD_ref[:, sl] += jnp.sum(dY * xs_p, axis=0, keepdims=True)
                dxs = _pair(dk, h0, q) * dY
                Yo = _nn(Cg, Hpb) * e_p
                dZb = (dY * e_p).astype(BF16)
                s0, s1 = hsum(dY * Yo)
                dac = dac + put(h0, s0, s1)
                dCg = dCg + _nt(dZb, Hpb)
                dHp = _tn(Cg, dZb) + cd_p * dHn
                t = jnp.sum(dHn * Hp, axis=0, keepdims=True)
                d0 = jnp.sum(jnp.where(lo1, t, 0.0), axis=1, keepdims=True)
                d1 = jnp.sum(jnp.where(lo1, 0.0, t), axis=1, keepdims=True)
                dlast = dlast + put1(h0, d0, d1) * q["cd"]
                Xd = X * f_p
                dXd = _nn(Bg, dHnb)
                dBg = dBg + _nt(Xd.astype(BF16), dHnb)
                dX = dXd * f_p
                s0, s1 = hsum(dXd * Xd)
                dac = dac - put(h0, s0, s1)
                dlast = dlast + put1(h0, jnp.sum(s0, axis=0, keepdims=True), jnp.sum(s1, axis=0, keepdims=True))
                for hh, mask in ((h0, lo), (h0 + 1, jnp.logical_not(lo))):
                    dec = _decay(q, hh)
                    Gm = cb * dec
                    Xh = jnp.where(mask, X, 0.0).astype(BF16)
                    dG = _nt(dYb, Xh)
                    dcb = dcb + dG * dec
                    Q = dG * Gm
                    dac = dac + jnp.where(lane == hh, jnp.sum(Q, axis=1, keepdims=True), 0.0)
                    dacT = dacT - jnp.where(rowi == hh, jnp.sum(Q, axis=0, keepdims=True), 0.0)
                    dX = dX + jnp.where(mask, _tn(Gm.astype(BF16), dYb), 0.0)
                dxs = dxs + dX * dtp
                s0, s1 = hsum(dX * xs_p)
                ddt = ddt + put(h0, s0, s1)
                dxc_ref[:, sl] = dxs
                dH[:, sl] = dHp
            dcbb = dcb.astype(BF16)
            dxc_ref[:, 1024 + g * L:1024 + (g + 1) * L] = dBg + _tn(dcbb, Cg)
            dxc_ref[:, 1280 + g * L:1280 + (g + 1) * L] = dCg + _nn(dcbb, Bg)

        dac = dac + dacT.T + jnp.where(rowi == L - 1, dlast, 0.0)
        triu = (rowi <= lane).astype(F32)
        da = jnp.dot(triu, dac, precision=HIGHEST, preferred_element_type=F32)
        ddt = ddt + da * q["A"]
        dal_ref[...] += jnp.sum(da * q["dt"], axis=0, keepdims=True) * q["A"]
        ddtr = jnp.where(lane < SSD_HEADS, ddt * _sigmoid(q["dtr"]), 0.0)
        ddtb_ref[...] += jnp.sum(ddtr, axis=0, keepdims=True)
        dp_ref[...] = jnp.concatenate([ddtr, jnp.zeros_like(ddtr)], axis=1).astype(BF16)

    v = _vec((1, LANES))
    rev = lambda c: nc - 1 - c
    return pl.pallas_call(
        body, grid=(nc,),
        in_specs=[pl.BlockSpec((L, 1024), lambda c: (rev(c), 0)),
                  pl.BlockSpec((L, 256), lambda c: (rev(c), 4)),
                  pl.BlockSpec((L, 256), lambda c: (rev(c), 5)),
                  pl.BlockSpec((L, LANES), lambda c: (rev(c), DT_OFF // LANES)),
                  pl.BlockSpec((1, SSD_STATE, 1024), lambda c: (rev(c), 0, 0)),
                  pl.BlockSpec((L, 1024), lambda c: (rev(c), 0)), v, v, v, _any()],
        out_specs=[pl.BlockSpec((L, XBC_W), lambda c: (rev(c), 0)),
                   pl.BlockSpec((L, DT_W), lambda c: (rev(c), DT_OFF // DT_W)),
                   v, v, _vec((1, 1024))],
        out_shape=[_sds((S, XBC_W), F32), _sds(dP.shape, BF16), _sds((1, LANES), F32), _sds((1, LANES), F32),
                   _sds((1, 1024), F32)],
        scratch_shapes=[pltpu.VMEM((SSD_STATE, 1024), F32)], input_output_aliases={9: 1}, name=name,
        compiler_params=_cp(("arbitrary",)))(xc, xc, xc, P, st, dy, dtb, alog, dsk, dP)


def ssd_post_fwd(y, P, nw, name):
    S = y.shape[0]
    T = ROW_TILE
    HW = 512

    def body(y_ref, z_ref, w_ref, o_ref):
        g = y_ref[...] * _silu(z_ref[...])
        for k in range(2):
            gk = g[:, k * HW:(k + 1) * HW]
            r = lax.rsqrt(jnp.mean(gk * gk, axis=-1, keepdims=True) + EPS)
            o_ref[:, k * HW:(k + 1) * HW] = (gk * r * w_ref[:, k * HW:(k + 1) * HW]).astype(BF16)

    return _rowcall(body, S, T, [_rows(T, 1024), _rows(T, 1024, Z_OFF // 1024), _vec((1, 1024))], _rows(T, 1024),
                    _sds((S, 1024), BF16), name)(y, P, nw)


def ssd_post_bwd(y, P, dout, dP, nw, name):
    S = y.shape[0]
    T = ROW_TILE
    HW = 512

    def body(y_ref, z_ref, d_ref, w_ref, dp_in, dy_ref, dz_ref, dw_ref):
        _zero_at_first(pl.program_id(0), dw_ref)
        yv = y_ref[...]
        zv = z_ref[...]
        sz = _silu(zv)
        g = yv * sz
        dv = d_ref[...]
        dgs = []
        for k in range(2):
            sl = slice(k * HW, (k + 1) * HW)
            gk = g[:, sl]
            r = lax.rsqrt(jnp.mean(gk * gk, axis=-1, keepdims=True) + EPS)
            gh = gk * r
            dk = dv[:, sl]
            dw_ref[:, sl] += jnp.sum(dk * gh, axis=0, keepdims=True)
            dgn = dk * w_ref[:, sl]
            dgs.append(r * (dgn - gh * jnp.mean(dgn * gh, axis=-1, keepdims=True)))
        dg = jnp.concatenate(dgs, axis=1)
        dy_ref[...] = dg * sz
        dz_ref[...] = (dg * yv * _dsilu(zv)).astype(BF16)

    zc = Z_OFF // 1024
    return _rowcall(body, S, T, [_rows(T, 1024), _rows(T, 1024, zc), _rows(T, 1024), _vec((1, 1024)), _any()],
                    [_rows(T, 1024), _rows(T, 1024, zc), _vec((1, 1024))],
                    [_sds((S, 1024), F32), _sds(dP.shape, BF16), _sds((1, 1024), F32)], name,
                    aliases={4: 1})(y, P, dout, nw, dP)


HEAD_W = 3 * ATTN_HEAD_DIM
HEAD_BLOCKS = NP // HEAD_W


def _slope(gi, h):
    c = [float(v) for v in SLOPES[gi]]
    return jnp.where(h == 0, c[0], jnp.where(h == 1, c[1], jnp.where(h == 2, c[2], c[3])))


def _rmsn(t):
    r = lax.rsqrt(jnp.mean(t * t, axis=-1, keepdims=True) + EPS)
    return t * r, r


def _rms_bwd(th, r, w, dn):
    dth = dn * w
    return r * (dth - th * jnp.mean(dth * th, axis=-1, keepdims=True))


ATT_ROWS = 2048


def _strided(r0, d):
    return pl.ds(r0, ATTN_STEPS, stride=d) if d > 1 else pl.ds(r0, ATTN_STEPS)


def attn_fwd(P, qw, kw, gi, name):
    S = P.shape[0]
    d = DILATIONS[gi]
    B = ATTN_STEPS
    E = ATTN_HEAD_DIM
    BR = B * d
    nq = ATT_ROWS // BR
    nsb = S // ATT_ROWS
    scale = E ** -0.5

    def body(q_ref, k_ref, v_ref, kp_ref, vp_ref, qw_ref, kw_ref, o_ref, l_ref):
        h = pl.program_id(0)
        s = pl.program_id(1)
        slope = _slope(gi, h) * float(d)
        qwv = qw_ref[...]
        kwv = kw_ref[...]
        qi = _iota((B, B), 0)
        kj = _iota((B, B), 1)
        rel_c = (qi - kj).astype(F32)
        rel_p = rel_c + float(B)
        bias_c = jnp.where(qi >= kj, -slope * rel_c, NEG)
        bias_p0 = jnp.where(jnp.logical_and(qi <= kj, s > 0), -slope * rel_p, NEG)
        bias_p = jnp.where(qi <= kj, -slope * rel_p, NEG)
        for j in range(nq):
            for r in range(d):
                rows = _strided(j * BR + r, d)
                if j == 0:
                    kpv, vpv, bp = kp_ref[_strided(r, d), :], vp_ref[_strided(r, d), :], bias_p0
                else:
                    prow = _strided((j - 1) * BR + r, d)
                    kpv, vpv, bp = k_ref[prow, :], v_ref[prow, :], bias_p
                qh, _ = _rmsn(q_ref[rows, :])
                kh, _ = _rmsn(k_ref[rows, :])
                kph, _ = _rmsn(kpv)
                qn = (qh * qwv).astype(BF16)
                kn = (kh * kwv).astype(BF16)
                kpn = (kph * kwv).astype(BF16)
                s_c = _nt(qn, kn) * scale + bias_c
                s_p = _nt(qn, kpn) * scale + bp
                m = jnp.maximum(jnp.max(s_c, axis=1, keepdims=True), jnp.max(s_p, axis=1, keepdims=True))
                l = jnp.sum(jnp.exp(s_c - m), axis=1, keepdims=True) + jnp.sum(jnp.exp(s_p - m), axis=1, keepdims=True)
                lse = m + jnp.log(l)
                p_c = jnp.exp(s_c - lse).astype(BF16)
                p_p = jnp.exp(s_p - lse).astype(BF16)
                o_ref[rows, :] = _nn(p_c, v_ref[rows, :].astype(BF16)) + _nn(p_p, vpv.astype(BF16))
                l_ref[rows, :] = jnp.broadcast_to(lse, (B, E))

    def cur(i):
        return pl.BlockSpec((ATT_ROWS, E), lambda h, s: (s, (gi * 4 + h) * 3 + i))

    def prev(i):
        return pl.BlockSpec((BR, E), lambda h, s: (jnp.maximum(s * nq - 1, 0), (gi * 4 + h) * 3 + i))

    wv = pl.BlockSpec((1, E), lambda h, s: (0, 0))
    ob = pl.BlockSpec((ATT_ROWS, E), lambda h, s: (s, h))
    return pl.pallas_call(
        body, grid=(4, nsb), in_specs=[cur(0), cur(1), cur(2), prev(1), prev(2), wv, wv],
        out_specs=[ob, ob], out_shape=[_sds((S, 512), F32), _sds((S, 512), F32)], name=name,
        compiler_params=_cp(("arbitrary", "arbitrary")))(P, P, P, P, P, qw, kw)


def attn_bwd(P, o, lse, do, dlse, dP, qw, kw, gi, name):
    S = P.shape[0]
    d = DILATIONS[gi]
    B = ATTN_STEPS
    E = ATTN_HEAD_DIM
    BR = B * d
    nq = ATT_ROWS // BR
    nsb = S // ATT_ROWS
    nblk = S // BR
    scale = E ** -0.5

    def body(q_ref, k_ref, v_ref, kp_ref, vp_ref, qx_ref, o_ref, l_ref, do_ref, dl_ref, ox_ref, lx_ref, dox_ref, dlx_ref,
             qw_ref, kw_ref, dp_in, dp_ref, dqw_ref, dkw_ref, stage_q, stage_k, stage_v):
        h = pl.program_id(0)
        s = pl.program_id(1)

        @pl.when(jnp.logical_and(h == 0, s == 0))
        def _():
            dqw_ref[...] = jnp.zeros_like(dqw_ref)
            dkw_ref[...] = jnp.zeros_like(dkw_ref)

        slope = _slope(gi, h) * float(d)
        qwv = qw_ref[...]
        kwv = kw_ref[...]
        qi = _iota((B, B), 0)
        kj = _iota((B, B), 1)
        rel_c = (qi - kj).astype(F32)
        rel_p = rel_c + float(B)
        band = qi <= kj
        bias_c = jnp.where(qi >= kj, -slope * rel_c, NEG)
        bias_p = jnp.where(band, -slope * rel_p, NEG)
        bias_p_first = jnp.where(jnp.logical_and(band, s > 0), -slope * rel_p, NEG)
        bias_p_last = jnp.where(jnp.logical_and(band, s < nsb - 1), -slope * rel_p, NEG)
        dqw = jnp.zeros((1, E), F32)
        dkw = jnp.zeros((1, E), F32)
        for j in range(nq):
            for r in range(d):
                rows = _strided(j * BR + r, d)
                if j == 0:
                    kpv, vpv, bB = kp_ref[_strided(r, d), :], vp_ref[_strided(r, d), :], bias_p_first
                else:
                    prow = _strided((j - 1) * BR + r, d)
                    kpv, vpv, bB = k_ref[prow, :], v_ref[prow, :], bias_p
                if j == nq - 1:
                    xr = _strided(r, d)
                    qxv, oxv, lxv, doxv, dlxv, bC = (qx_ref[xr, :], ox_ref[xr, :], lx_ref[xr, :], dox_ref[xr, :],
                                                     dlx_ref[xr, :], bias_p_last)
                else:
                    xr = _strided((j + 1) * BR + r, d)
                    qxv, oxv, lxv, doxv, dlxv, bC = (q_ref[xr, :], o_ref[xr, :], l_ref[xr, :], do_ref[xr, :],
                                                     dl_ref[xr, :], bias_p)
                qh, rq = _rmsn(q_ref[rows, :])
                kh, rk = _rmsn(k_ref[rows, :])
                kph, _ = _rmsn(kpv)
                qxh, _ = _rmsn(qxv)
                qn = (qh * qwv).astype(BF16)
                kn = (kh * kwv).astype(BF16)
                kpn = (kph * kwv).astype(BF16)
                qxn = (qxh * qwv).astype(BF16)
                v = v_ref[rows, :].astype(BF16)
                vp = vpv.astype(BF16)
                lse_c = l_ref[rows, :]
                pA = jnp.exp(_nt(qn, kn) * scale + bias_c - lse_c)
                pB = jnp.exp(_nt(qn, kpn) * scale + bB - lse_c)
                pC = jnp.exp(_nt(qxn, kn) * scale + bC - lxv)
                doc = do_ref[rows, :]
                docb = doc.astype(BF16)
                doxb = doxv.astype(BF16)
                corr_c = jnp.sum(dl_ref[rows, :] - doc * o_ref[rows, :], axis=1, keepdims=True)
                corr_n = jnp.sum(dlxv - doxv * oxv, axis=1, keepdims=True)
                dsA = (pA * (_nt(docb, v) + corr_c)).astype(BF16)
                dsB = (pB * (_nt(docb, vp) + corr_c)).astype(BF16)
                dsC = (pC * (_nt(doxb, v) + corr_n)).astype(BF16)
                dqn = (_nn(dsA, kn) + _nn(dsB, kpn)) * scale
                dkn = (_tn(dsA, qn) + _tn(dsC, qxn)) * scale
                dv = _tn(pA.astype(BF16), docb) + _tn(pC.astype(BF16), doxb)
                dqw = dqw + jnp.sum(dqn * qh, axis=0, keepdims=True)
                dkw = dkw + jnp.sum(dkn * kh, axis=0, keepdims=True)
                stage_q[rows, :] = _rms_bwd(qh, rq, qwv, dqn)
                stage_k[rows, :] = _rms_bwd(kh, rk, kwv, dkn)
                stage_v[rows, :] = dv
        dqw_ref[...] += dqw
        dkw_ref[...] += dkw
        dp_ref[:, 0:E] = stage_q[...].astype(BF16)
        dp_ref[:, E:2 * E] = stage_k[...].astype(BF16)
        dp_ref[:, 2 * E:3 * E] = stage_v[...].astype(BF16)

    def cur(i):
        return pl.BlockSpec((ATT_ROWS, E), lambda h, s: (s, (gi * 4 + h) * 3 + i))

    def prev(i):
        return pl.BlockSpec((BR, E), lambda h, s: (jnp.maximum(s * nq - 1, 0), (gi * 4 + h) * 3 + i))

    nxt_q = pl.BlockSpec((BR, E), lambda h, s: (jnp.minimum((s + 1) * nq, nblk - 1), (gi * 4 + h) * 3))
    ocur = pl.BlockSpec((ATT_ROWS, E), lambda h, s: (s, h))
    onxt = pl.BlockSpec((BR, E), lambda h, s: (jnp.minimum((s + 1) * nq, nblk - 1), h))
    wv = pl.BlockSpec((1, E), lambda h, s: (0, 0))
    return pl.pallas_call(
        body, grid=(4, nsb),
        in_specs=[cur(0), cur(1), cur(2), prev(1), prev(2), nxt_q, ocur, ocur, ocur, ocur, onxt, onxt, onxt, onxt, wv, wv,
                  _any()],
        out_specs=[pl.BlockSpec((ATT_ROWS, HEAD_W), lambda h, s: (s, gi * 4 + h)), wv, wv],
        out_shape=[_sds(dP.shape, BF16), _sds((1, E), F32), _sds((1, E), F32)],
        scratch_shapes=[pltpu.VMEM((ATT_ROWS, E), F32)] * 3,
        input_output_aliases={16: 0}, name=name,
        compiler_params=_cp(("arbitrary", "arbitrary")))(
            P, P, P, P, P, P, o, lse, do, dlse, o, lse, do, dlse, qw, kw, dP)


def attn_merge_fwd(os_, ls_, name):
    S = os_[0].shape[0]
    T = ROW_TILE

    def body(o0, o1, o2, l0, l1, l2, out):
        a, b, c = l0[...], l1[...], l2[...]
        m = jnp.maximum(jnp.maximum(a, b), c)
        ea, eb, ec = jnp.exp(a - m), jnp.exp(b - m), jnp.exp(c - m)
        inv = 1.0 / (ea + eb + ec)
        out[...] = ((ea * o0[...] + eb * o1[...] + ec * o2[...]) * inv).astype(BF16)

    return _rowcall(body, S, T, [_rows(T, 512)] * 6, _rows(T, 512), _sds((S, 512), BF16), name)(*os_, *ls_)


def attn_merge_bwd(os_, ls_, dom, name):
    S = os_[0].shape[0]
    T = ROW_TILE

    def body(o0, o1, o2, l0, l1, l2, d_ref, do0, do1, do2, dl0, dl1, dl2):
        a, b, c = l0[...], l1[...], l2[...]
        m = jnp.maximum(jnp.maximum(a, b), c)
        ea, eb, ec = jnp.exp(a - m), jnp.exp(b - m), jnp.exp(c - m)
        inv = 1.0 / (ea + eb + ec)
        wa, wb, wc = ea * inv, eb * inv, ec * inv
        dv = d_ref[...]
        do0[...] = wa * dv
        do1[...] = wb * dv
        do2[...] = wc * dv
        ga, gb, gc = dv * o0[...], dv * o1[...], dv * o2[...]
        tot = wa * ga + wb * gb + wc * gc
        dl0[...] = wa * (ga - tot)
        dl1[...] = wb * (gb - tot)
        dl2[...] = wc * (gc - tot)

    sp = _rows(T, 512)
    return _rowcall(body, S, T, [sp] * 7, [sp] * 6, [_sds((S, 512), F32)] * 6, name)(*os_, *ls_, dom)


def _counts(i, T, rows, w, offset=0):
    t = i * T + offset + _iota((rows, 1), 0)
    return jnp.minimum(t + 1, w).astype(F32)


def pool_fwd(P, wmix, scale, name):
    S = P.shape[0]
    T = ROW_TILE
    GW = 256
    uc = U_OFF // 1024
    hb = T // POOL_HALO

    def body(u_ref, h_ref, wm_ref, sc_ref, pl_ref, yp_ref):
        i = pl.program_id(0)
        u = u_ref[...]
        ext = jnp.concatenate([jnp.where(i > 0, h_ref[...], 0.0), u], axis=0)
        for g, w in enumerate(POOL_WINDOWS):
            sl = slice(g * GW, (g + 1) * GW)
            s = ext[:, sl]
            k = 1
            while k < w:
                s = s + pltpu.roll(s, k, 0)
                k *= 2
            pooled = (s[POOL_HALO:] / _counts(i, T, T, w) - u[:, sl]).astype(BF16)
            pl_ref[:, sl] = pooled
            yp_ref[:, sl] = (_nn(pooled, wm_ref[g]) * sc_ref[:, sl]).astype(BF16)

    return _rowcall(
        body, S, T,
        [_rows(T, 1024, uc), pl.BlockSpec((POOL_HALO, 1024), lambda i: (jnp.maximum(i * hb - 1, 0), uc)),
         _vec((4, GW, GW)), _vec((1, 1024))],
        [_rows(T, 1024), _rows(T, 1024)], [_sds((S, 1024), BF16), _sds((S, 1024), BF16)], name)(P, P, wmix, scale)


def pool_bwd(dyp, pooled, dP, wmix, scale, name):
    S = dyp.shape[0]
    T = ROW_TILE
    nt = S // T
    GW = 256
    uc = U_OFF // 1024
    hb = T // POOL_HALO
    TE = T + POOL_HALO

    def body(d_ref, dn_ref, p_ref, wm_ref, sc_ref, dp_in, du_ref, dwm_ref, dsc_ref):
        i = pl.program_id(0)
        _zero_at_first(i, dwm_ref, dsc_ref)
        dv = d_ref[...]
        dn = jnp.where(i < nt - 1, dn_ref[...], 0.0)
        for g, w in enumerate(POOL_WINDOWS):
            sl = slice(g * GW, (g + 1) * GW)
            pg = p_ref[:, sl]
            wm = wm_ref[g]
            dsc_ref[:, sl] += jnp.sum(dv[:, sl] * _nn(pg, wm), axis=0, keepdims=True)
            dmb = (dv[:, sl] * sc_ref[:, sl]).astype(BF16)
            dmnb = (dn[:, sl] * sc_ref[:, sl]).astype(BF16)
            dwm_ref[g] += _tn(pg, dmb)
            dpl = _nt(dmb, wm)
            dpln = _nt(dmnb, wm)
            s = jnp.concatenate([dpl / _counts(i, T, T, w), dpln / _counts(i, T, POOL_HALO, w, T)], axis=0)
            k = 1
            while k < w:
                s = s + pltpu.roll(s, TE - k, 0)
                k *= 2
            du_ref[:, sl] = (s[:T] - dpl).astype(BF16)

    return _rowcall(
        body, S, T,
        [_rows(T, 1024), pl.BlockSpec((POOL_HALO, 1024), lambda i: (jnp.minimum((i + 1) * hb, S // POOL_HALO - 1), 0)),
         _rows(T, 1024), _vec((4, GW, GW)), _vec((1, 1024)), _any()],
        [_rows(T, 1024, uc), _vec((4, GW, GW)), _vec((1, 1024))],
        [_sds(dP.shape, BF16), _sds((4, GW, GW), F32), _sds((1, 1024), F32)], name,
        aliases={5: 0})(dyp, dyp, pooled, wmix, scale, dP)


def gate_fwd(P, ys, ya, yp, name):
    S = ys.shape[0]
    T = ROW_TILE
    gc = GATE_OFF // 1024

    def body(g0, g1, g2, a, b, c, out):
        out[...] = (_sigmoid(g0[...]) * a[...] + _sigmoid(g1[...]) * b[...] + _sigmoid(g2[...]) * c[...]).astype(BF16)

    r = _rows(T, 1024)
    return _rowcall(body, S, T, [_rows(T, 1024, gc), _rows(T, 1024, gc + 1), _rows(T, 1024, gc + 2), r, r, r], r,
                    _sds((S, 1024), BF16), name)(P, P, P, ys, ya, yp)


def gate_bwd(P, ys, ya, yp, dm, dP, name):
    S = ys.shape[0]
    T = ROW_TILE
    gc = GATE_OFF // 1024

    def body(g0, g1, g2, a, b, c, d_ref, dp_in, dg_ref, da, db, dc):
        dv = d_ref[...]
        outs = []
        for gr, yr, dyr in ((g0, a, da), (g1, b, db), (g2, c, dc)):
            s = _sigmoid(gr[...])
            dyr[...] = (s * dv).astype(BF16)
            outs.append(dv * yr[...] * s * (1.0 - s))
        dg_ref[...] = jnp.concatenate(outs, axis=1).astype(BF16)

    r = _rows(T, 1024)
    return _rowcall(body, S, T, [_rows(T, 1024, gc), _rows(T, 1024, gc + 1), _rows(T, 1024, gc + 2), r, r, r, r, _any()],
                    [_rows(T, GATE_W, GATE_OFF // GATE_W), r, r, r],
                    [_sds(dP.shape, BF16)] + [_sds((S, 1024), BF16)] * 3, name,
                    aliases={7: 0})(P, P, P, ys, ya, yp, dm, dP)


def resid_bwd(dx, g, y, name):
    S, D = dx.shape
    T = ROW_TILE

    def body(d_ref, g_ref, y_ref, dy_ref, dg_ref):
        _zero_at_first(pl.program_id(0), dg_ref)
        dv = d_ref[...]
        dy_ref[...] = (g_ref[...] * dv).astype(BF16)
        dg_ref[...] += jnp.sum(dv * y_ref[...], axis=0, keepdims=True)

    r = _rows(T, D)
    return _rowcall(body, S, T, [r, _vec((1, D)), r], [r, _vec((1, D))], [_sds((S, D), BF16), _sds((1, D), F32)],
                    name)(dx, g, y)


def loss_fwd_bwd(y, target, name):
    S, D = y.shape
    T = ROW_TILE

    def body(y_ref, t_ref, acc_ref, dy_ref):
        _zero_at_first(pl.program_id(0), acc_ref)
        e = y_ref[...] - t_ref[...]
        acc_ref[...] += jnp.sum(e * e, axis=0, keepdims=True)
        dy_ref[...] = e * (1.0 / D)

    r = _rows(T, D)
    return _rowcall(body, S, T, [r, r], [_vec((1, D)), r], [_sds((1, D), F32), _sds((S, D), F32)], name)(y, target)


def silu_rows(c, name):
    def body(c_ref, o_ref):
        o_ref[...] = _silu(c_ref[...])

    return pl.pallas_call(body, out_shape=_sds(c.shape, F32), name=name)(c)


def adamw(parts, w, m, v, name, row0=0, prev=None):
    n, R, C = parts.shape
    Rt = w.shape[0]
    tr = R
    while tr * C * 4 > (1 << 20) and tr % 16 == 0:
        tr //= 2
    b0 = row0 // tr
    c1 = 1.0 / (1.0 - ADAM_B1 ** ADAM_STEP)
    c2 = 1.0 / (1.0 - ADAM_B2 ** ADAM_STEP)

    def body(p_ref, w_ref, m_ref, v_ref, *rest):
        g_ref, d_ref, nm_ref, nv_ref = rest[-4:]
        g = p_ref[0].astype(F32)
        for k in range(1, n):
            g = g + p_ref[k].astype(F32)
        nm = ADAM_B1 * m_ref[...] + (1.0 - ADAM_B1) * g
        nv = ADAM_B2 * v_ref[...] + (1.0 - ADAM_B2) * (g * g)
        g_ref[...] = g
        nm_ref[...] = nm
        nv_ref[...] = nv
        d_ref[...] = -ADAM_LR * ((nm * c1) / (jnp.sqrt(nv * c2) + ADAM_EPS) + ADAM_WD * w_ref[...])

    r = pl.BlockSpec((tr, C), lambda i: (b0 + i, 0))
    ins = [parts, w, m, v] + (list(prev) if prev is not None else [])
    in_specs = [pl.BlockSpec((n, tr, C), lambda i: (0, i, 0)), r, r, r] + ([_any()] * 4 if prev is not None else [])
    aliases = {4 + k: k for k in range(4)} if prev is not None else None
    return _rowcall(body, R, tr, in_specs, [r, r, r, r], [_sds((Rt, C), F32)] * 4, name, aliases=aliases)(*ins)


FWD_HOSTS = {"in_proj": ("w_in", "w_attn_out", "w_pool_mix"), "ff1": ("w_ff1", "w_ssd_out"),
             "ff2": ("w_ff2", "w_pool_out", "w_out")}
W_IN_PARTS = 4
BWD_HOSTS = {"d_w_ff2": ("w_in_0",), "d_a": ("w_in_1",), "d_w_ff1": ("w_in_2",), "d_h2": ("w_in_3",),
             "d_w_in": ("w_ff1", "w_out", "w_ssd_out"), "d_h1": ("w_ff2", "w_attn_out", "w_pool_mix", "w_pool_out")}


def _ride(mmargs, kind, hosts, host, cargo, got, **kw):
    if not cargo or hosts[host][0] not in cargo:
        return mm(*mmargs, **kw)
    out, res = mm(*mmargs, rider=Rider(kind, [cargo[k] for k in hosts[host]]), **kw)
    got.update(zip(hosts[host], res))
    return out


def _epi_relu2(r):
    p = jnp.maximum(r, 0.0)
    return r, p * p


def _epi_relu2_bwd(r, f):
    return (2.0 * jnp.maximum(f, 0.0) * r,)


def _epi_resid(r, x, g):
    return r, x + g * r


def layer_fwd(x, mod, W, sm, tag, cargo=None):
    sh1, sc1, g1, sh2, sc2, g2 = mod
    sv, got = {}, {}
    ride = functools.partial(_ride, kind="gather", hosts=FWD_HOSTS, cargo=cargo, got=got)
    h1 = rms_mod_fwd(x, sm["norm1_w"], sc1, sh1, f"rms1_fwd_{tag}")
    P = ride((h1, W["in"], "nn", F32, f"in_proj_{tag}"), host="in_proj")
    pre, xc = conv_fwd(P, sm["conv_w"], sm["conv_b"], f"conv_fwd_{tag}")
    y, st = ssd_fwd(xc, P, sm["dt_bias"], sm["a_log"], sm["d_skip"], f"ssd_fwd_{tag}")
    ysn = ssd_post_fwd(y, P, sm["ssd_norm_w"], f"ssd_post_fwd_{tag}")
    ys = mm(ysn, W["ssd_out"], "nn", F32, f"ssd_out_{tag}")
    os_, ls_ = [], []
    for gi in range(3):
        o, l = attn_fwd(P, sm["q_norm_w"], sm["k_norm_w"], gi, f"attn_fwd{gi}_{tag}")
        os_.append(o)
        ls_.append(l)
    om = attn_merge_fwd(os_, ls_, f"attn_merge_fwd_{tag}")
    ya = mm(om, W["attn_out"], "nn", F32, f"attn_out_{tag}")
    pooled, ypp = pool_fwd(P, W["pool_mix"], sm["pool_scale"], f"pool_fwd_{tag}")
    yp = mm(ypp, W["pool_out"], "nn", F32, f"pool_out_{tag}")
    m = gate_fwd(P, ys, ya, yp, f"gate_fwd_{tag}")
    mo, x1 = mm(m, W["out"], "nn", F32, f"mix_out_{tag}", extra=(x, g1), epi=_epi_resid, out_dtypes=(F32, F32))
    h2 = rms_mod_fwd(x1, sm["norm2_w"], sc2, sh2, f"rms2_fwd_{tag}")
    f, a = ride((h2, W["ff1"], "nn", F32, f"ff1_{tag}"), host="ff1", epi=_epi_relu2, out_dtypes=(F32, BF16))
    o2, x2 = ride((a, W["ff2"], "nn", F32, f"ff2_{tag}"), host="ff2", extra=(x1, g2), epi=_epi_resid,
                  out_dtypes=(F32, F32))
    sv.update(x=x, h1=h1, P=P, pre=pre, xc=xc, y=y, st=st, ysn=ysn, ys=ys, os=os_, ls=ls_, om=om, ya=ya,
              pooled=pooled, ypp=ypp, yp=yp, m=m, mo=mo, x1=x1, h2=h2, f=f, a=a, o2=o2)
    return x2, sv, got


def layer_bwd(dx2, sv, mod, W, sm, tag, slabs, cargo=None):
    sh1, sc1, g1, sh2, sc2, g2 = mod
    S = dx2.shape[0]
    P = sv["P"]
    gw, gs, got, own, got_own = {}, {}, {}, {}, {}
    ride = functools.partial(_ride, kind="scatter", hosts=BWD_HOSTS, cargo=cargo, got=got)
    ride_own = functools.partial(_ride, kind="scatter", hosts=BWD_HOSTS, cargo=own, got=got_own)
    do2, dg2 = resid_bwd(dx2, g2, sv["o2"], f"resid2_bwd_{tag}")
    gw["ff2"] = ride((sv["a"], do2, "tn", BF16, f"d_w_ff2_{tag}"), host="d_w_ff2")
    df = ride((do2, W["ff2"], "nt", BF16, f"d_f_{tag}"), host="d_a", extra=(sv["f"],), epi=_epi_relu2_bwd)
    gw["ff1"] = ride((sv["h2"], df, "tn", BF16, f"d_w_ff1_{tag}"), host="d_w_ff1")
    dh2 = ride((df, W["ff1"], "nt", F32, f"d_h2_{tag}"), host="d_h2")
    dx1, gs["norm2_w"], dsc2, dsh2 = rms_mod_bwd(sv["x1"], dh2, dx2, sm["norm2_w"], sc2, f"rms2_bwd_{tag}")
    dmo, dg1 = resid_bwd(dx1, g1, sv["mo"], f"resid1_bwd_{tag}")
    gw["out"] = mm(sv["m"], dmo, "tn", BF16, f"d_w_out_{tag}")
    dm = mm(dmo, W["out"], "nt", F32, f"d_m_{tag}")
    dP = lax.empty((S, NP), BF16)
    dP, dys, dya, dyp = gate_bwd(P, sv["ys"], sv["ya"], sv["yp"], dm, dP, f"gate_bwd_{tag}")
    gw["ssd_out"] = mm(sv["ysn"], dys, "tn", BF16, f"d_w_ssd_out_{tag}")
    dysn = mm(dys, W["ssd_out"], "nt", F32, f"d_ysn_{tag}")
    dy, dP, gs["ssd_norm_w"] = ssd_post_bwd(sv["y"], P, dysn, dP, sm["ssd_norm_w"], f"ssd_post_bwd_{tag}")
    dxc, dP, gs["dt_bias"], gs["a_log"], dDl = ssd_bwd(sv["xc"], P, sv["st"], dy, dP, sm["dt_bias"], sm["a_log"],
                                                      sm["d_skip"], f"ssd_bwd_{tag}")
    gs["d_skip"] = dDl.reshape(SSD_HEADS, SSD_HEAD_DIM).sum(axis=1)
    gs["dt_bias"] = gs["dt_bias"][0, :SSD_HEADS]
    gs["a_log"] = gs["a_log"][0, :SSD_HEADS]
    dP, gs["conv_w"], gs["conv_b"] = conv_bwd(dxc, sv["pre"], P, dP, sm["conv_w"], f"conv_bwd_{tag}")
    gw["attn_out"] = mm(sv["om"], dya, "tn", BF16, f"d_w_attn_out_{tag}")
    dom = mm(dya, W["attn_out"], "nt", F32, f"d_om_{tag}")
    mb = attn_merge_bwd(sv["os"], sv["ls"], dom, f"attn_merge_bwd_{tag}")
    dqw = dkw = None
    for gi in range(3):
        dP, a_, b_ = attn_bwd(P, sv["os"][gi], sv["ls"][gi], mb[gi], mb[3 + gi], dP, sm["q_norm_w"], sm["k_norm_w"], gi,
                              f"attn_bwd{gi}_{tag}")
        dqw = a_ if dqw is None else dqw + a_
        dkw = b_ if dkw is None else dkw + b_
    gs["q_norm_w"], gs["k_norm_w"] = dqw, dkw
    gw["pool_out"] = mm(sv["ypp"], dyp, "tn", BF16, f"d_w_pool_out_{tag}")
    dypp = mm(dyp, W["pool_out"], "nt", F32, f"d_ypp_{tag}")
    dP, dwm, gs["pool_scale"] = pool_bwd(dypp, sv["pooled"], dP, W["pool_mix"], sm["pool_scale"], f"pool_bwd_{tag}")
    gw["pool_mix"] = dwm.astype(BF16)
    own.update(slabs(gw))
    gw["in"] = ride_own((sv["h1"], dP, "tn", BF16, f"d_w_in_{tag}"), host="d_w_in")
    dh1 = ride_own((dP, W["in"], "nt", F32, f"d_h1_{tag}"), host="d_h1")
    dx, gs["norm1_w"], dsc1, dsh1 = rms_mod_bwd(sv["x"], dh1, dx1, sm["norm1_w"], sc1, f"rms1_bwd_{tag}")
    dmod = jnp.concatenate([dsh1, dsc1, dg1, dsh2, dsc2, dg2], axis=1)
    return dx, dmod, gs, gw, slabs({"in": gw["in"]}), got, got_own


def _w_in_to_fused(w):
    E = ATTN_HEAD_DIM
    cols = []
    for h in range(ATTN_HEADS):
        for off in (O_Q, O_K, O_V):
            cols.append(w[:, off + h * E: off + (h + 1) * E])
    cols += [w[:, O_XBC:O_DT], w[:, O_G:IN_WIDTH], w[:, O_Z:O_XBC], w[:, O_U:O_G], w[:, O_DT:O_Q],
             jnp.zeros((w.shape[0], DT_W - SSD_HEADS), w.dtype)]
    return jnp.concatenate(cols, axis=1)


def _fused_to_w_in(g):
    E = ATTN_HEAD_DIM
    q = [g[:, h * HEAD_W: h * HEAD_W + E] for h in range(ATTN_HEADS)]
    k = [g[:, h * HEAD_W + E: h * HEAD_W + 2 * E] for h in range(ATTN_HEADS)]
    v = [g[:, h * HEAD_W + 2 * E: (h + 1) * HEAD_W] for h in range(ATTN_HEADS)]
    return jnp.concatenate([g[:, Z_OFF:Z_OFF + Z_W], g[:, XBC_OFF:XBC_OFF + XBC_W], g[:, DT_OFF:DT_OFF + SSD_HEADS]]
                           + q + k + v + [g[:, U_OFF:U_OFF + U_W], g[:, GATE_OFF:GATE_OFF + GATE_W]], axis=1)


def _cols_from_shards(g):
    return jnp.moveaxis(g, 0, 1).reshape(g.shape[1], N_DEV * g.shape[2])


def _cols_to_shards(w):
    R = w.shape[0]
    return jnp.moveaxis(w.reshape(R, N_DEV, w.shape[1] // N_DEV), 1, 0)


BIG = ("w_in", "w_ssd_out", "w_attn_out", "w_pool_mix", "w_pool_out", "w_out", "w_ff1", "w_ff2")
SMALL = ("b_ada", "norm1_w", "norm2_w", "conv_b", "dt_bias", "a_log", "d_skip", "ssd_norm_w", "q_norm_w", "k_norm_w",
         "pool_scale")
WEIGHTS = ("w_ada", "b_ada", "norm1_w", "norm2_w", "w_in", "conv_w", "conv_b", "dt_bias", "a_log", "d_skip",
           "ssd_norm_w", "w_ssd_out", "q_norm_w", "k_norm_w", "w_attn_out", "w_pool_mix", "pool_scale", "w_pool_out",
           "w_out", "w_ff1", "w_ff2")


def _pack(arrs):
    flat = jnp.concatenate([a.reshape(-1).astype(F32) for a in arrs])
    pad = (-flat.shape[0]) % (8 * LANES)
    return jnp.pad(flat, (0, pad)).reshape(-1, LANES)


def _unpack(packed, shapes):
    flat = packed.reshape(-1)
    out, off = [], 0
    for s in shapes:
        n = int(np.prod(s))
        out.append(flat[off:off + n].reshape(s))
        off += n
    return out


def _pad_lanes(v):
    return jnp.pad(v, ((0, 0), (0, LANES - v.shape[1])))[:, None, :]


def _step(x, c, target, p, m_, v_):
    Ld = p["w_in"].shape[0]
    me = 4 * lax.axis_index("x") + 2 * lax.axis_index("y") + lax.axis_index("c")

    def local_blocks(l):
        return {k: p[k][l].astype(BF16) for k in BIG}

    blk = local_blocks(0)
    gathered = all_gather([blk[k] for k in BIG] + [c, p["conv_w"]], "gather_first")
    G = dict(zip(BIG, gathered[:len(BIG)]))
    c_all = gathered[len(BIG)].reshape(N_DEV, D_MODEL)
    conv_w_full = jnp.moveaxis(gathered[len(BIG) + 1], 0, 2).reshape(Ld, 4, XBC_W)

    def layer_weights(G):
        W = {}
        W["in"] = _w_in_to_fused(_cols_from_shards(G["w_in"]))
        W["ssd_out"] = G["w_ssd_out"].reshape(1024, D_MODEL)
        W["attn_out"] = _cols_from_shards(G["w_attn_out"])
        W["pool_mix"] = jnp.moveaxis(G["w_pool_mix"], 0, 1).reshape(4, 256, 256)
        W["pool_out"] = G["w_pool_out"].reshape(1024, D_MODEL)
        W["out"] = G["w_out"].reshape(D_MODEL, D_MODEL)
        W["ff1"] = _cols_from_shards(G["w_ff1"])
        W["ff2"] = G["w_ff2"].reshape(4 * D_MODEL, D_MODEL)
        return W

    def slabs(gw):
        out = {}
        if "in" in gw:
            full = _cols_to_shards(_fused_to_w_in(gw["in"]))
            rows = full.shape[1] // W_IN_PARTS
            for q in range(W_IN_PARTS):
                out[f"w_in_{q}"] = full[:, q * rows:(q + 1) * rows]
            return out
        out["w_ssd_out"] = gw["ssd_out"].reshape(N_DEV, -1, D_MODEL)
        out["w_attn_out"] = _cols_to_shards(gw["attn_out"])
        out["w_pool_mix"] = jnp.moveaxis(gw["pool_mix"].reshape(4, N_DEV, 32, 256), 1, 0)
        out["w_pool_out"] = gw["pool_out"].reshape(N_DEV, -1, D_MODEL)
        out["w_out"] = gw["out"].reshape(N_DEV, -1, D_MODEL)
        out["w_ff1"] = _cols_to_shards(gw["ff1"])
        out["w_ff2"] = gw["ff2"].reshape(N_DEV, -1, D_MODEL)
        return out

    def layer_small(l):
        sm = {k: p[k][l][None, :] for k in ("norm1_w", "norm2_w", "conv_b", "ssd_norm_w", "q_norm_w", "k_norm_w",
                                            "pool_scale")}
        for k in ("dt_bias", "a_log", "d_skip"):
            sm[k] = jnp.pad(p[k][l], (0, LANES - SSD_HEADS))[None, :]
        sm["conv_w"] = conv_w_full[l]
        return sm

    cond = silu_rows(c_all, "cond_silu")
    ncol = p["w_ada"].shape[2]
    b_mine = lax.dynamic_slice_in_dim(p["b_ada"], me * ncol, ncol, axis=1)
    parts = [mm(cond, p["w_ada"][l], "nn", F32, f"ada_{l}") + b_mine[l][None, :] for l in range(Ld)]
    mod_cols = jnp.stack(parts, axis=1)
    mod = all_to_all([mod_cols], "scatter_mod")[0]
    mod = jnp.moveaxis(mod, 0, 1).reshape(Ld, 6 * D_MODEL)

    def mods(l):
        return [mod[l, k * D_MODEL:(k + 1) * D_MODEL][None, :] for k in range(6)]

    saved, Ws, sms = [], [], []
    h = x
    for l in range(Ld):
        Ws.append(layer_weights(G))
        sms.append(layer_small(l))
        h, sv, G = layer_fwd(h, mods(l), Ws[l], sms[l], f"l{l}", local_blocks(l + 1) if l + 1 < Ld else None)
        saved.append(sv)
    lsum, dh = loss_fwd_bwd(h, target, "loss")
    loss = 0.5 / D_MODEL * jnp.sum(lsum)
    gss, dmods, recvs = [None] * Ld, [None] * Ld, [dict() for _ in range(Ld)]
    cargo = None
    for l in reversed(range(Ld)):
        dh, dmods[l], gss[l], _, nxt, got, got_own = layer_bwd(dh, saved[l], mods(l), Ws[l], sms[l], f"l{l}", slabs, cargo)
        recvs[l].update(got_own)
        if cargo is not None:
            recvs[l + 1].update(got)
        cargo = nxt
    dmod = jnp.concatenate(dmods, axis=0)
    dmod_cols = jnp.moveaxis(dmod.reshape(Ld, N_DEV, ncol), 1, 0)
    parts_in = [f"w_in_{q}" for q in range(W_IN_PARTS)]
    last = all_to_all([cargo[k] for k in parts_in] + [dmod_cols], "scatter_last")
    recvs[0].update(zip(parts_in, last[:W_IN_PARTS]))
    dmod_all = last[W_IN_PARTS]

    out_g, out_d, out_m, out_v = {}, {}, {}, {}

    def finish(k, res):
        shp = p[k].shape
        out_g[k], out_d[k], out_m[k], out_v[k] = [t.reshape(shp) for t in res]

    for k in BIG:
        C = p[k].shape[-1]
        rows = int(np.prod(p[k].shape[1:-1]))
        res = None
        for l in reversed(range(Ld)):
            pieces = [(f"{k}_{q}", rows // W_IN_PARTS) for q in range(W_IN_PARTS)] if k == "w_in" else [(k, rows)]
            for q, (key, nrows) in enumerate(pieces):
                res = adamw(recvs[l][key].reshape(N_DEV, nrows, C), p[k].reshape(-1, C), m_[k].reshape(-1, C),
                            v_[k].reshape(-1, C), f"adamw_{key}_l{l}", row0=l * rows + q * nrows, prev=res)
        finish(k, res)
    g_ada = jnp.stack([mm(cond, dmod_all[:, l], "tn", F32, f"d_w_ada_{l}") for l in range(Ld)], axis=0)
    finish("w_ada", adamw(g_ada.reshape(1, -1, ncol), p["w_ada"].reshape(-1, ncol), m_["w_ada"].reshape(-1, ncol),
                          v_["w_ada"].reshape(-1, ncol), "adamw_w_ada"))

    small_g = {k: jnp.stack([gss[l][k].reshape(p[k].shape[1:]) for l in range(Ld)], axis=0) for k in SMALL if k != "b_ada"}
    small_g["b_ada"] = dmod
    conv_g = jnp.stack([gss[l]["conv_w"] for l in range(Ld)], axis=0)
    packed = _pack([small_g[k] for k in SMALL] + [conv_g])
    parts_small = all_gather([packed], "gather_small_grads")[0]
    zeros_conv = jnp.zeros(conv_g.shape, F32)
    res = adamw(parts_small, _pack([p[k] for k in SMALL] + [zeros_conv]), _pack([m_[k] for k in SMALL] + [zeros_conv]),
                _pack([v_[k] for k in SMALL] + [zeros_conv]), "adamw_small")
    shapes = [p[k].shape for k in SMALL] + [conv_g.shape]
    unp = [_unpack(t, shapes) for t in res]
    for i, k in enumerate(SMALL):
        out_g[k], out_d[k], out_m[k], out_v[k] = unp[0][i], unp[1][i], unp[2][i], unp[3][i]
    cw = p["conv_w"].shape[-1]
    conv_mine = lax.dynamic_slice_in_dim(unp[0][len(SMALL)], me * cw, cw, axis=2)
    finish("conv_w", adamw(conv_mine.reshape(1, -1, cw), p["conv_w"].reshape(-1, cw), m_["conv_w"].reshape(-1, cw),
                           v_["conv_w"].reshape(-1, cw), "adamw_conv_w"))

    loss = lax.psum(loss, ("x", "y", "c"))
    return loss, dh, out_g, out_d, out_m, out_v


def kernel(x, c, w_ada, b_ada, norm1_w, norm2_w, w_in, conv_w, conv_b, dt_bias, a_log, d_skip, ssd_norm_w, w_ssd_out, q_norm_w, k_norm_w, w_attn_out, w_pool_mix, pool_scale, w_pool_out, w_out, w_ff1, w_ff2, loss_target, m_w_ada, m_b_ada, m_norm1_w, m_norm2_w, m_w_in, m_conv_w, m_conv_b, m_dt_bias, m_a_log, m_d_skip, m_ssd_norm_w, m_w_ssd_out, m_q_norm_w, m_k_norm_w, m_w_attn_out, m_w_pool_mix, m_pool_scale, m_w_pool_out, m_w_out, m_w_ff1, m_w_ff2, v_w_ada, v_b_ada, v_norm1_w, v_norm2_w, v_w_in, v_conv_w, v_conv_b, v_dt_bias, v_a_log, v_d_skip, v_ssd_norm_w, v_w_ssd_out, v_q_norm_w, v_k_norm_w, v_w_attn_out, v_w_pool_mix, v_pool_scale, v_w_pool_out, v_w_out, v_w_ff1, v_w_ff2):
    p = dict(w_ada=w_ada, b_ada=b_ada, norm1_w=norm1_w, norm2_w=norm2_w, w_in=w_in, conv_w=conv_w, conv_b=conv_b,
             dt_bias=dt_bias, a_log=a_log, d_skip=d_skip, ssd_norm_w=ssd_norm_w, w_ssd_out=w_ssd_out, q_norm_w=q_norm_w,
             k_norm_w=k_norm_w, w_attn_out=w_attn_out, w_pool_mix=w_pool_mix, pool_scale=pool_scale,
             w_pool_out=w_pool_out, w_out=w_out, w_ff1=w_ff1, w_ff2=w_ff2)
    m_ = dict(w_ada=m_w_ada, b_ada=m_b_ada, norm1_w=m_norm1_w, norm2_w=m_norm2_w, w_in=m_w_in, conv_w=m_conv_w,
              conv_b=m_conv_b, dt_bias=m_dt_bias, a_log=m_a_log, d_skip=m_d_skip, ssd_norm_w=m_ssd_norm_w,
              w_ssd_out=m_w_ssd_out, q_norm_w=m_q_norm_w, k_norm_w=m_k_norm_w, w_attn_out=m_w_attn_out,
              w_pool_mix=m_w_pool_mix, pool_scale=m_pool_scale, w_pool_out=m_w_pool_out, w_out=m_w_out, w_ff1=m_w_ff1,
              w_ff2=m_w_ff2)
    v_ = dict(w_ada=v_w_ada, b_ada=v_b_ada, norm1_w=v_norm1_w, norm2_w=v_norm2_w, w_in=v_w_in, conv_w=v_conv_w,
              conv_b=v_conv_b, dt_bias=v_dt_bias, a_log=v_a_log, d_skip=v_d_skip, ssd_norm_w=v_ssd_norm_w,
              w_ssd_out=v_w_ssd_out, q_norm_w=v_q_norm_w, k_norm_w=v_k_norm_w, w_attn_out=v_w_attn_out,
              w_pool_mix=v_w_pool_mix, pool_scale=v_pool_scale, w_pool_out=v_w_pool_out, w_out=v_w_out, w_ff1=v_w_ff1,
              w_ff2=v_w_ff2)
    loss, dx, g, d, nm, nv = _step(x[0], c, loss_target[0], p, m_, v_)
    return (loss, dx[None], *[g[k] for k in WEIGHTS], *[d[k] for k in WEIGHTS], *[nm[k] for k in WEIGHTS],
            *[nv[k] for k in WEIGHTS])
```

```python
import functools
import math

import numpy as np
import jax
import jax.numpy as jnp
from jax import lax
from jax.experimental import pallas as pl
from jax.experimental.pallas import tpu as pltpu

F32 = jnp.float32
BF16 = jnp.bfloat16
MESH = pl.DeviceIdType.MESH
HIGHEST = lax.Precision.HIGHEST

N_DEV = 8
V7X_VMEM_LIMIT_BYTES = 56 * 1024 * 1024
LANES = 128
ROW_TILE = 256

EPS = 1e-6
D_MODEL = 1024
SSD_HEADS = 16
SSD_HEAD_DIM = 64
SSD_CHUNK = 128
SSD_STATE = 128
ATTN_HEADS = 12
ATTN_HEAD_DIM = 128
ATTN_STEPS = 128
DILATIONS = (1, 4, 16)
POOL_WINDOWS = (2, 4, 8, 16)
POOL_HALO = 16
CONV_HALO = 8
NEG = -1e30

ADAM_LR = 0.001
ADAM_B1 = 0.9
ADAM_B2 = 0.999
ADAM_EPS = 1e-08
ADAM_WD = 0.01
ADAM_STEP = 10

QKV_OFF, QKV_W = 0, 4608
XBC_OFF, XBC_W = 4608, 1536
GATE_OFF, GATE_W = 6144, 3072
Z_OFF, Z_W = 9216, 1024
U_OFF, U_W = 10240, 1024
DT_OFF, DT_W = 11264, 256
NP = 11520
IN_WIDTH = 11280
O_Z, O_XBC, O_DT, O_Q, O_K, O_V, O_U, O_G = 0, 1024, 2560, 2576, 4112, 5648, 7184, 8208


def _alibi_slopes(n):
    def pow2(k):
        start = 2.0 ** (-8.0 / k)
        return [start ** (i + 1) for i in range(k)]
    if math.log2(n).is_integer():
        s = pow2(n)
    else:
        c = 2 ** math.floor(math.log2(n))
        s = pow2(c) + pow2(2 * c)[0::2][: n - c]
    return np.sort(np.asarray(s, np.float32))[::-1].copy()


SLOPES = _alibi_slopes(ATTN_HEADS).reshape(3, 4)


def _cp(sem=None):
    return pltpu.CompilerParams(dimension_semantics=sem, vmem_limit_bytes=V7X_VMEM_LIMIT_BYTES)


def _pick(dim, cap):
    if dim <= cap:
        return dim
    t = cap - cap % LANES
    while t >= LANES:
        if dim % t == 0:
            return t
        t -= LANES
    raise ValueError((dim, cap))


def _sigmoid(x):
    return 1.0 / (1.0 + jnp.exp(-x))


def _silu(x):
    return x * _sigmoid(x)


def _dsilu(x):
    s = _sigmoid(x)
    return s * (1.0 + x * (1.0 - s))


def _softplus(x):
    return jnp.maximum(x, 0.0) + jnp.log(1.0 + jnp.exp(-jnp.abs(x)))


def _dot(a, b, dims):
    return lax.dot_general(a, b, (dims, ((), ())), preferred_element_type=F32)


def _nn(a, b):
    return _dot(a, b, ((1,), (0,)))


def _nt(a, b):
    return _dot(a, b, ((1,), (1,)))


def _tn(a, b):
    return _dot(a, b, ((0,), (0,)))


def _iota(shape, axis):
    return lax.broadcasted_iota(jnp.int32, shape, axis)


def _position():
    return lax.axis_index("x"), lax.axis_index("y"), lax.axis_index("c")


def _index(px, py, pc):
    return 4 * px + 2 * py + pc


def _remote(src, dst, send_sems, recv_sems, a, k, to):
    return pltpu.make_async_remote_copy(src_ref=src, dst_ref=dst, send_sem=send_sems.at[a, k], recv_sem=recv_sems.at[a, k],
                                        device_id=to, device_id_type=MESH)


def _gather_plan(srcs, dsts, send_sems, recv_sems, local_sems):
    x, y, c = _position()
    me, sib = (x, y, c), (x, y, 1 - c)
    chips = [(1 - x, y), (x, 1 - y), (1 - x, 1 - y)]
    n = len(srcs)

    def start():
        for a in range(n):
            mine = dsts[a].at[_index(*me)]
            pltpu.make_async_copy(srcs[a], mine, local_sems.at[a]).start()
            _remote(srcs[a], mine, send_sems, recv_sems, a, 0, sib).start()
            for j, chip in enumerate(chips):
                _remote(srcs[a], mine, send_sems, recv_sems, a, 1 + j, (*chip, c)).start()

    def finish():
        for a in range(n):
            for j, chip in enumerate(chips):
                blk = dsts[a].at[_index(*chip, c)]
                _remote(blk, blk, send_sems, recv_sems, a, 1 + j, me).wait_recv()
                _remote(blk, blk, send_sems, recv_sems, a, 4 + j, sib).start()
        for a in range(n):
            mine = dsts[a].at[_index(*me)]
            blk = dsts[a].at[_index(*sib)]
            _remote(blk, blk, send_sems, recv_sems, a, 0, me).wait_recv()
            for j, chip in enumerate(chips):
                blk = dsts[a].at[_index(*chip, 1 - c)]
                _remote(blk, blk, send_sems, recv_sems, a, 4 + j, me).wait_recv()
            for k in range(N_DEV - 1):
                _remote(mine, mine, send_sems, recv_sems, a, k, me).wait_send()
            pltpu.make_async_copy(srcs[a], mine, local_sems.at[a]).wait()

    return start, finish


def _scatter_plan(srcs, dsts, send_sems, recv_sems, local_sems):
    x, y, c = _position()
    me = _index(x, y, c)
    n = len(srcs)
    copies = []
    for a in range(n):
        copies.append(pltpu.make_async_copy(srcs[a].at[me], dsts[a].at[me], local_sems.at[a]))
        for k in range(1, N_DEV):
            px, py, pc = x ^ ((k >> 2) & 1), y ^ ((k >> 1) & 1), c ^ (k & 1)
            copies.append(_remote(srcs[a].at[_index(px, py, pc)], dsts[a].at[me], send_sems, recv_sems, a, k - 1, (px, py, pc)))

    def start():
        for cp in copies:
            cp.start()

    def finish():
        for cp in copies:
            cp.wait()

    return start, finish


class Rider:
    def __init__(self, kind, arrs):
        self.kind = kind
        items = [a if isinstance(a, tuple) else (a, None) for a in arrs]
        self.arrs = [a for a, _ in items]
        self.lead = [i for _, i in items]

    def _shapes(self):
        return [a.shape if i is None else a.shape[1:] for a, i in zip(self.arrs, self.lead)]

    def out_shapes(self):
        if self.kind == "gather":
            return [pltpu.HBM((N_DEV,) + s, a.dtype) for s, a in zip(self._shapes(), self.arrs)]
        return [pltpu.HBM(s, a.dtype) for s, a in zip(self._shapes(), self.arrs)]

    def sems(self):
        n = len(self.arrs)
        return [pltpu.SemaphoreType.DMA((n, N_DEV - 1)), pltpu.SemaphoreType.DMA((n, N_DEV - 1)),
                pltpu.SemaphoreType.DMA((n,))]

    def plan(self, srcs, dsts, sems):
        srcs = [s if i is None else s.at[i] for s, i in zip(srcs, self.lead)]
        return (_gather_plan if self.kind == "gather" else _scatter_plan)(srcs, dsts, *sems)


def _exchange(kind, arrs, name):
    rd = Rider(kind, arrs)
    n = len(arrs)

    def body(*refs):
        start, finish = rd.plan(refs[:n], refs[n:2 * n], refs[2 * n:])
        start()
        finish()

    hbm = pl.BlockSpec(memory_space=pltpu.HBM)
    return pl.pallas_call(
        body, in_specs=[hbm] * n, out_specs=[hbm] * n, out_shape=rd.out_shapes(), scratch_shapes=rd.sems(),
        name=name, compiler_params=pltpu.CompilerParams(has_side_effects=True))(*rd.arrs)


def all_gather(arrs, name):
    return _exchange("gather", arrs, name)


def all_to_all(arrs, name):
    return _exchange("scatter", arrs, name)


def mm(a, b, mode, out_dtype, name, rider=None, extra=(), epi=None, out_dtypes=None, col_blocks=False, tm_cap=1024,
       tn_cap=1280, tk_cap=2048):
    blk = None
    if col_blocks and mode != "tn":
        nb, rows_b, blk = b.shape
        b_shape = (rows_b, nb * blk)
    else:
        b_shape = b.shape
    if mode == "nn":
        (M, K), (K2, N) = a.shape, b_shape
    elif mode == "nt":
        (M, K), (N, K2) = a.shape, b_shape
    else:
        (K, M), (K2, N) = a.shape, b_shape
        if col_blocks:
            blk = N // N_DEV
    assert K == K2, (a.shape, b.shape, mode)
    tm, tn, tk = _pick(M, tm_cap), _pick(N, tn_cap), _pick(K, tk_cap)
    if col_blocks and mode == "nt":
        tk = blk
    elif col_blocks:
        tn = blk
    nk = K // tk
    if mode == "tn":
        a_spec = pl.BlockSpec((tk, tm), lambda i, j, k: (k, i))
    else:
        a_spec = pl.BlockSpec((tm, tk), lambda i, j, k: (i, k))
    if mode == "nt":
        b_spec = (pl.BlockSpec((None, tn, tk), lambda i, j, k: (k, j, 0)) if col_blocks
                  else pl.BlockSpec((tn, tk), lambda i, j, k: (j, k)))
    else:
        b_spec = (pl.BlockSpec((None, tk, tn), lambda i, j, k: (j, k, 0)) if col_blocks and mode == "nn"
                  else pl.BlockSpec((tk, tn), lambda i, j, k: (k, j)))
    dims = {"nn": ((1,), (0,)), "nt": ((1,), (1,)), "tn": ((0,), (0,))}[mode]
    o_spec = pl.BlockSpec((tm, tn), lambda i, j, k: (i, j))
    blocked_out = col_blocks and mode == "tn"
    if blocked_out:
        assert not extra and epi is None
        o_spec = pl.BlockSpec((None, tm, tn), lambda i, j, k: (j, i, 0))
    v_spec = pl.BlockSpec((1, tn), lambda i, j, k: (0, j))
    grid = (M // tm, N // tn, nk)
    out_dtypes = list(out_dtypes) if out_dtypes is not None else [out_dtype]
    nr = len(rider.arrs) if rider is not None else 0
    ne, no = len(extra), len(out_dtypes)
    extra_specs = []
    for e in extra:
        assert e.shape in ((M, N), (1, N)), (e.shape, M, N)
        extra_specs.append(o_spec if e.shape[0] == M and M != 1 else v_spec)

    def body(*refs):
        a_ref, b_ref = refs[0], refs[1]
        e_refs = refs[2:2 + ne]
        srcs = refs[2 + ne:2 + ne + nr]
        o_refs = refs[2 + ne + nr:2 + ne + nr + no]
        dsts = refs[2 + ne + nr + no:2 + ne + 2 * nr + no]
        rest = refs[2 + ne + 2 * nr + no:]
        i, j, k = pl.program_id(0), pl.program_id(1), pl.program_id(2)
        if nr:
            start, finish = rider.plan(srcs, dsts, rest[-3:])

            @pl.when(jnp.logical_and(jnp.logical_and(i == 0, j == 0), k == 0))
            def _():
                start()

        def write(r):
            vals = epi(r, *[e[...] for e in e_refs]) if epi is not None else (r,)
            for o_ref, val, dt in zip(o_refs, vals, out_dtypes):
                o_ref[...] = val.astype(dt)

        prod = _dot(a_ref[...].astype(BF16), b_ref[...].astype(BF16), dims)
        if nk == 1:
            write(prod)
        else:
            acc = rest[0]

            @pl.when(k == 0)
            def _():
                acc[...] = prod

            @pl.when(k > 0)
            def _():
                acc[...] += prod

            @pl.when(k == nk - 1)
            def _():
                write(acc[...])

        if nr:
            @pl.when(jnp.logical_and(jnp.logical_and(i == grid[0] - 1, j == grid[1] - 1), k == nk - 1))
            def _():
                finish()

    hbm = pl.BlockSpec(memory_space=pltpu.HBM)
    out_shape = [jax.ShapeDtypeStruct((N_DEV, M, blk) if blocked_out else (M, N), dt) for dt in out_dtypes]
    scratch = [pltpu.VMEM((tm, tn), F32)] if nk > 1 else []
    if nr:
        res = pl.pallas_call(
            body, grid=grid, in_specs=[a_spec, b_spec] + extra_specs + [hbm] * nr, out_specs=[o_spec] * no + [hbm] * nr,
            out_shape=out_shape + rider.out_shapes(), scratch_shapes=scratch + rider.sems(), name=name,
            compiler_params=pltpu.CompilerParams(dimension_semantics=("arbitrary",) * 3,
                                                 vmem_limit_bytes=V7X_VMEM_LIMIT_BYTES, has_side_effects=True),
        )(a, b, *extra, *rider.arrs)
        outs, got = list(res[:no]), list(res[no:])
        return (outs[0] if no == 1 else outs), got
    res = pl.pallas_call(
        body, grid=grid, in_specs=[a_spec, b_spec] + extra_specs, out_specs=[o_spec] * no, out_shape=out_shape,
        scratch_shapes=scratch, name=name, compiler_params=_cp(("parallel", "parallel", "arbitrary")))(a, b, *extra)
    return res[0] if no == 1 else list(res)


def _rows(tile, w, cb=0):
    return pl.BlockSpec((tile, w), lambda i: (i, cb))


def _vec(shape):
    nd = len(shape)
    return pl.BlockSpec(shape, lambda i: (0,) * nd)


def _any():
    return pl.BlockSpec(memory_space=pl.ANY)


def _sds(shape, dtype):
    return jax.ShapeDtypeStruct(shape, dtype)


def _rowcall(body, n_rows, tile, in_specs, out_specs, out_shape, name, aliases=None):
    return pl.pallas_call(
        body, grid=(n_rows // tile,), in_specs=in_specs, out_specs=out_specs, out_shape=out_shape,
        name=name, input_output_aliases=aliases or {}, compiler_params=_cp(("arbitrary",)))


def _zero_at_first(step, *refs):
    @pl.when(step == 0)
    def _():
        for r in refs:
            r[...] = jnp.zeros_like(r)


def rms_mod_fwd(x, nw, sc, sh, name):
    S, D = x.shape
    T = ROW_TILE

    def body(x_ref, nw_ref, sc_ref, sh_ref, h_ref):
        xv = x_ref[...]
        r = lax.rsqrt(jnp.mean(xv * xv, axis=-1, keepdims=True) + EPS)
        y = xv * r * nw_ref[...]
        h_ref[...] = (y * (1.0 + sc_ref[...]) + sh_ref[...]).astype(BF16)

    return _rowcall(body, S, T, [_rows(T, D), _vec((1, D)), _vec((1, D)), _vec((1, D))], _rows(T, D),
                    _sds((S, D), BF16), name)(x, nw, sc, sh)


def rms_mod_bwd(x, dh, dres, nw, sc, name):
    S, D = x.shape
    T = ROW_TILE

    def body(x_ref, dh_ref, dres_ref, nw_ref, sc_ref, dx_ref, dnw_ref, dsc_ref, dsh_ref):
        _zero_at_first(pl.program_id(0), dnw_ref, dsc_ref, dsh_ref)
        xv = x_ref[...]
        d = dh_ref[...]
        r = lax.rsqrt(jnp.mean(xv * xv, axis=-1, keepdims=True) + EPS)
        xh = xv * r
        nwv = nw_ref[...]
        dsh_ref[...] += jnp.sum(d, axis=0, keepdims=True)
        dsc_ref[...] += jnp.sum(d * (xh * nwv), axis=0, keepdims=True)
        dy = d * (1.0 + sc_ref[...])
        dnw_ref[...] += jnp.sum(dy * xh, axis=0, keepdims=True)
        dxh = dy * nwv
        dx_ref[...] = dres_ref[...] + r * (dxh - xh * jnp.mean(dxh * xh, axis=-1, keepdims=True))

    v = _vec((1, D))
    return _rowcall(body, S, T, [_rows(T, D), _rows(T, D), _rows(T, D), v, v],
                    [_rows(T, D), v, v, v],
                    [_sds((S, D), F32), _sds((1, D), F32), _sds((1, D), F32), _sds((1, D), F32)], name)(x, dh, dres, nw, sc)


def _shift_down(tile, halo, k):
    if k == 0:
        return tile
    hr = halo.shape[0]
    rolled = pltpu.roll(tile, k, 0)
    rh = pltpu.roll(halo, k, 0)
    rid = _iota(halo.shape, 0)
    first = jnp.where(rid < k, rh, rolled[:hr])
    return jnp.concatenate([first, rolled[hr:]], axis=0)


def _shift_up(tile, nxt, k):
    if k == 0:
        return tile
    T = tile.shape[0]
    hr = nxt.shape[0]
    rolled = pltpu.roll(tile, T - k, 0)
    rn = pltpu.roll(nxt, hr - k, 0)
    rid = _iota(nxt.shape, 0)
    last = jnp.where(rid >= hr - k, rn, rolled[T - hr:])
    return jnp.concatenate([rolled[:T - hr], last], axis=0)


CONV_CB = 512


def conv_fwd(P, conv_w, conv_b, name):
    S = P.shape[0]
    T = ROW_TILE
    nj = XBC_W // CONV_CB
    cb0 = XBC_OFF // CONV_CB
    hb = T // CONV_HALO

    def body(x_ref, h_ref, w_ref, b_ref, pre_ref, xc_ref):
        i = pl.program_id(0)
        xb = x_ref[...]
        xh = jnp.where(i > 0, h_ref[...], 0.0)
        acc = b_ref[...] + w_ref[3:4, :] * xb
        for idx in range(3):
            acc = acc + w_ref[idx:idx + 1, :] * _shift_down(xb, xh, 3 - idx)
        pre_ref[...] = acc
        xc_ref[...] = _silu(acc)

    blk = pl.BlockSpec((T, CONV_CB), lambda i, j: (i, j))
    return pl.pallas_call(
        body, grid=(S // T, nj),
        in_specs=[pl.BlockSpec((T, CONV_CB), lambda i, j: (i, cb0 + j)),
                  pl.BlockSpec((CONV_HALO, CONV_CB), lambda i, j: (jnp.maximum(i * hb - 1, 0), cb0 + j)),
                  pl.BlockSpec((4, CONV_CB), lambda i, j: (0, j)),
                  pl.BlockSpec((1, CONV_CB), lambda i, j: (0, j))],
        out_specs=[blk, blk], out_shape=[_sds((S, XBC_W), F32), _sds((S, XBC_W), F32)], name=name,
        compiler_params=_cp(("arbitrary", "arbitrary")))(P, P, conv_w, conv_b)


def conv_bwd(dxc, pre, P, dP, conv_w, name):
    S = P.shape[0]
    T = ROW_TILE
    nt = S // T
    nj = XBC_W // CONV_CB
    cb0 = XBC_OFF // CONV_CB
    hb = T // CONV_HALO

    def body(d_ref, dn_ref, p_ref, pn_ref, x_ref, xh_ref, w_ref, dp_in, dp_ref, dw_ref, db_ref):
        i = pl.program_id(1)
        _zero_at_first(i, dw_ref, db_ref)
        dpre = d_ref[...] * _dsilu(p_ref[...])
        dpn = jnp.where(i < nt - 1, dn_ref[...] * _dsilu(pn_ref[...]), 0.0)
        db_ref[...] += jnp.sum(dpre, axis=0, keepdims=True)
        xb = x_ref[...]
        xh = jnp.where(i > 0, xh_ref[...], 0.0)
        dx = jnp.zeros_like(dpre)
        for idx in range(4):
            k = 3 - idx
            dw_ref[idx:idx + 1, :] += jnp.sum(dpre * _shift_down(xb, xh, k), axis=0, keepdims=True)
            dx = dx + w_ref[idx:idx + 1, :] * _shift_up(dpre, dpn, k)
        dp_ref[...] = dx.astype(BF16)

    cur = pl.BlockSpec((T, CONV_CB), lambda j, i: (i, j))
    nxt = pl.BlockSpec((CONV_HALO, CONV_CB), lambda j, i: (jnp.minimum((i + 1) * hb, S // CONV_HALO - 1), j))
    return pl.pallas_call(
        body, grid=(nj, nt),
        in_specs=[cur, nxt, cur, nxt,
                  pl.BlockSpec((T, CONV_CB), lambda j, i: (i, cb0 + j)),
                  pl.BlockSpec((CONV_HALO, CONV_CB), lambda j, i: (jnp.maximum(i * hb - 1, 0), cb0 + j)),
                  pl.BlockSpec((4, CONV_CB), lambda j, i: (0, j)), _any()],
        out_specs=[pl.BlockSpec((T, CONV_CB), lambda j, i: (i, cb0 + j)),
                   pl.BlockSpec((4, CONV_CB), lambda j, i: (0, j)),
                   pl.BlockSpec((1, CONV_CB), lambda j, i: (0, j))],
        out_shape=[_sds(dP.shape, BF16), _sds((4, XBC_W), F32), _sds((1, XBC_W), F32)],
        input_output_aliases={7: 0}, name=name,
        compiler_params=_cp(("arbitrary", "arbitrary")))(dxc, dxc, pre, pre, P, P, conv_w, dP)


def _col(v, h, lane):
    return jnp.sum(jnp.where(lane == h, v, 0.0), axis=1, keepdims=True)


def _row(v, h, rowi):
    return jnp.sum(jnp.where(rowi == h, v, 0.0), axis=0, keepdims=True)


def _ssd_common(dt_ref, dtb_ref, al_ref):
    L = SSD_CHUNK
    lane = _iota((L, L), 1)
    rowi = _iota((L, L), 0)
    dtr = dt_ref[...] + dtb_ref[...]
    dt = _softplus(dtr)
    A = -jnp.exp(al_ref[...])
    a = dt * A
    tril = (rowi >= lane).astype(F32)
    ac = jnp.dot(tril, a, precision=HIGHEST, preferred_element_type=F32)
    acT = ac.T
    last = _row(ac, L - 1, rowi)
    return dict(lane=lane, rowi=rowi, dtr=dtr, dt=dt, A=A, ac=ac, acT=acT, last=last,
                eac=jnp.exp(ac), fac=jnp.exp(last - ac), cd=jnp.exp(last),
                lo=lane < SSD_HEAD_DIM, causal=rowi >= lane)


def _pair(v, h0, q):
    lane = q["lane"][:v.shape[0]]
    lo = q["lo"][:v.shape[0]]
    return jnp.where(lo, _col(v, h0, lane), _col(v, h0 + 1, lane))


def _decay(q, h):
    seg = _col(q["ac"], h, q["lane"]) - _row(q["acT"], h, q["rowi"])
    return jnp.exp(jnp.where(q["causal"], seg, NEG))


def ssd_fwd(xc, P, dtb, alog, dsk, name):
    S = xc.shape[0]
    L = SSD_CHUNK
    nc = S // L

    def body(xs_ref, b_ref, c_ref, dt_ref, dtb_ref, al_ref, dk_ref, y_ref, st_ref, H):
        @pl.when(pl.program_id(0) == 0)
        def _():
            H[...] = jnp.zeros_like(H)

        st_ref[0] = H[...]
        q = _ssd_common(dt_ref, dtb_ref, al_ref)
        dk = dk_ref[...]
        for g in range(2):
            Bg = b_ref[:, g * L:(g + 1) * L].astype(BF16)
            Cg = c_ref[:, g * L:(g + 1) * L].astype(BF16)
            cb = _nt(Cg, Bg)
            for jj in range(4):
                j = g * 4 + jj
                h0 = 2 * j
                sl = slice(j * L, (j + 1) * L)
                xs_p = xs_ref[:, sl]
                X = xs_p * _pair(q["dt"], h0, q)
                Xb = X.astype(BF16)
                G0 = (cb * _decay(q, h0)).astype(BF16)
                G1 = (cb * _decay(q, h0 + 1)).astype(BF16)
                Yd = jnp.where(q["lo"], _nn(G0, Xb), _nn(G1, Xb))
                Hp = H[:, sl]
                Yo = _nn(Cg, Hp.astype(BF16)) * _pair(q["eac"], h0, q)
                y_ref[:, sl] = Yd + Yo + _pair(dk, h0, q) * xs_p
                Xd = (X * _pair(q["fac"], h0, q)).astype(BF16)
                H[:, sl] = _pair(q["cd"], h0, q) * Hp + _tn(Bg, Xd)

    v = _vec((1, LANES))
    return pl.pallas_call(
        body, grid=(nc,),
        in_specs=[pl.BlockSpec((L, 1024), lambda c: (c, 0)),
                  pl.BlockSpec((L, 256), lambda c: (c, 4)),
                  pl.BlockSpec((L, 256), lambda c: (c, 5)),
                  pl.BlockSpec((L, LANES), lambda c: (c, DT_OFF // LANES)), v, v, v],
        out_specs=[pl.BlockSpec((L, 1024), lambda c: (c, 0)),
                   pl.BlockSpec((1, SSD_STATE, 1024), lambda c: (c, 0, 0))],
        out_shape=[_sds((S, 1024), F32), _sds((nc, SSD_STATE, 1024), F32)],
        scratch_shapes=[pltpu.VMEM((SSD_STATE, 1024), F32)], name=name,
        compiler_params=_cp(("arbitrary",)))(xc, xc, xc, P, dtb, alog, dsk)


def ssd_bwd(xc, P, st, dy, dP, dtb, alog, dsk, name):
    S = xc.shape[0]
    L = SSD_CHUNK
    nc = S // L

    def body(xs_ref, b_ref, c_ref, dt_ref, st_ref, dy_ref, dtb_ref, al_ref, dk_ref, dp_in,
             dxc_ref, dp_ref, ddtb_ref, dal_ref, dD_ref, dH):
        step = pl.program_id(0)

        @pl.when(step == 0)
        def _():
            dH[...] = jnp.zeros_like(dH)

        _zero_at_first(step, ddtb_ref, dal_ref, dD_ref)
        q = _ssd_common(dt_ref, dtb_ref, al_ref)
        lane, rowi, lo = q["lane"], q["rowi"], q["lo"]
        lane1 = lane[:1]
        lo1 = lo[:1]
        dk = dk_ref[...]
        dac = jnp.zeros((L, L), F32)
        dacT = jnp.zeros((L, L), F32)
        dlast = jnp.zeros((1, L), F32)
        ddt = jnp.zeros((L, L), F32)

        def put(h0, s0, s1):
            return jnp.where(lane == h0, s0, 0.0) + jnp.where(lane == h0 + 1, s1, 0.0)

        def put1(h0, s0, s1):
            return jnp.where(lane1 == h0, s0, 0.0) + jnp.where(lane1 == h0 + 1, s1, 0.0)

        def hsum(v):
            return (jnp.sum(jnp.where(lo, v, 0.0), axis=1, keepdims=True),
                    jnp.sum(jnp.where(lo, 0.0, v), axis=1, keepdims=True))

        for g in range(2):
            Bg = b_ref[:, g * L:(g + 1) * L].astype(BF16)
            Cg = c_ref[:, g * L:(g + 1) * L].astype(BF16)
            cb = _nt(Cg, Bg)
            dcb = jnp.zeros((L, L), F32)
            dCg = jnp.zeros((L, L), F32)
            dBg = jnp.zeros((L, L), F32)
            for jj in range(4):
                j = g * 4 + jj
                h0 = 2 * j
                sl = slice(j * L, (j + 1) * L)
                xs_p = xs_ref[:, sl]
                dY = dy_ref[:, sl]
                dYb = dY.astype(BF16)
                dtp = _pair(q["dt"], h0, q)
                X = xs_p * dtp
                e_p = _pair(q["eac"], h0, q)
                f_p = _pair(q["fac"], h0, q)
                cd_p = _pair(q["cd"], h0, q)
                Hp = st_ref[0, :, sl]
                Hpb = Hp.astype(BF16)
                dHn = dH[:, sl]
                dHnb = dHn.astype(BF16)
                dD_ref[:, sl] += jnp.sum(dY * xs_p, axis=0, keepdims=True)
                dxs = _pair(dk, h0, q) * dY
                Yo = _nn(Cg, Hpb) * e_p
                dZb = (dY * e_p).astype(BF16)
                s0, s1 = hsum(dY * Yo)
                dac = dac + put(h0, s0, s1)
                dCg = dCg + _nt(dZb, Hpb)
                dHp = _tn(Cg, dZb) + cd_p * dHn
                t = jnp.sum(dHn * Hp, axis=0, keepdims=True)
                d0 = jnp.sum(jnp.where(lo1, t, 0.0), axis=1, keepdims=True)
                d1 = jnp.sum(jnp.where(lo1, 0.0, t), axis=1, keepdims=True)
                dlast = dlast + put1(h0, d0, d1) * q["cd"]
                Xd = X * f_p
                dXd = _nn(Bg, dHnb)
                dBg = dBg + _nt(Xd.astype(BF16), dHnb)
                dX = dXd * f_p
                s0, s1 = hsum(dXd * Xd)
                dac = dac - put(h0, s0, s1)
                dlast = dlast + put1(h0, jnp.sum(s0, axis=0, keepdims=True), jnp.sum(s1, axis=0, keepdims=True))
                for hh, mask in ((h0, lo), (h0 + 1, jnp.logical_not(lo))):
                    dec = _decay(q, hh)
                    Gm = cb * dec
                    Xh = jnp.where(mask, X, 0.0).astype(BF16)
                    dG = _nt(dYb, Xh)
                    dcb = dcb + dG * dec
                    Q = dG * Gm
                    dac = dac + jnp.where(lane == hh, jnp.sum(Q, axis=1, keepdims=True), 0.0)
                    dacT = dacT - jnp.where(rowi == hh, jnp.sum(Q, axis=0, keepdims=True), 0.0)
                    dX = dX + jnp.where(mask, _tn(Gm.astype(BF16), dYb), 0.0)
                dxs = dxs + dX * dtp
                s0, s1 = hsum(dX * xs_p)
                ddt = ddt + put(h0, s0, s1)
                dxc_ref[:, sl] = dxs
                dH[:, sl] = dHp
            dcbb = dcb.astype(BF16)
            dxc_ref[:, 1024 + g * L:1024 + (g + 1) * L] = dBg + _tn(dcbb, Cg)
            dxc_ref[:, 1280 + g * L:1280 + (g + 1) * L] = dCg + _nn(dcbb, Bg)

        dac = dac + dacT.T + jnp.where(rowi == L - 1, dlast, 0.0)
        triu = (rowi <= lane).astype(F32)
        da = jnp.dot(triu, dac, precision=HIGHEST, preferred_element_type=F32)
        ddt = ddt + da * q["A"]
        dal_ref[...] += jnp.sum(da * q["dt"], axis=0, keepdims=True) * q["A"]
        ddtr = jnp.where(lane < SSD_HEADS, ddt * _sigmoid(q["dtr"]), 0.0)
        ddtb_ref[...] += jnp.sum(ddtr, axis=0, keepdims=True)
        dp_ref[...] = jnp.concatenate([ddtr, jnp.zeros_like(ddtr)], axis=1).astype(BF16)

    v = _vec((1, LANES))
    rev = lambda c: nc - 1 - c
    return pl.pallas_call(
        body, grid=(nc,),
        in_specs=[pl.BlockSpec((L, 1024), lambda c: (rev(c), 0)),
                  pl.BlockSpec((L, 256), lambda c: (rev(c), 4)),
                  pl.BlockSpec((L, 256), lambda c: (rev(c), 5)),
                  pl.BlockSpec((L, LANES), lambda c: (rev(c), DT_OFF // LANES)),
                  pl.BlockSpec((1, SSD_STATE, 1024), lambda c: (rev(c), 0, 0)),
                  pl.BlockSpec((L, 1024), lambda c: (rev(c), 0)), v, v, v, _any()],
        out_specs=[pl.BlockSpec((L, XBC_W), lambda c: (rev(c), 0)),
                   pl.BlockSpec((L, DT_W), lambda c: (rev(c), DT_OFF // DT_W)),
                   v, v, _vec((1, 1024))],
        out_shape=[_sds((S, XBC_W), F32), _sds(dP.shape, BF16), _sds((1, LANES), F32), _sds((1, LANES), F32),
                   _sds((1, 1024), F32)],
        scratch_shapes=[pltpu.VMEM((SSD_STATE, 1024), F32)], input_output_aliases={9: 1}, name=name,
        compiler_params=_cp(("arbitrary",)))(xc, xc, xc, P, st, dy, dtb, alog, dsk, dP)


def ssd_post_fwd(y, P, nw, name):
    S = y.shape[0]
    T = ROW_TILE
    HW = 512

    def body(y_ref, z_ref, w_ref, o_ref):
        g = y_ref[...] * _silu(z_ref[...])
        for k in range(2):
            gk = g[:, k * HW:(k + 1) * HW]
            r = lax.rsqrt(jnp.mean(gk * gk, axis=-1, keepdims=True) + EPS)
            o_ref[:, k * HW:(k + 1) * HW] = (gk * r * w_ref[:, k * HW:(k + 1) * HW]).astype(BF16)

    return _rowcall(body, S, T, [_rows(T, 1024), _rows(T, 1024, Z_OFF // 1024), _vec((1, 1024))], _rows(T, 1024),
                    _sds((S, 1024), BF16), name)(y, P, nw)


def ssd_post_bwd(y, P, dout, dP, nw, name):
    S = y.shape[0]
    T = ROW_TILE
    HW = 512

    def body(y_ref, z_ref, d_ref, w_ref, dp_in, dy_ref, dz_ref, dw_ref):
        _zero_at_first(pl.program_id(0), dw_ref)
        yv = y_ref[...]
        zv = z_ref[...]
        sz = _silu(zv)
        g = yv * sz
        dv = d_ref[...]
        dgs = []
        for k in range(2):
            sl = slice(k * HW, (k + 1) * HW)
            gk = g[:, sl]
            r = lax.rsqrt(jnp.mean(gk * gk, axis=-1, keepdims=True) + EPS)
            gh = gk * r
            dk = dv[:, sl]
            dw_ref[:, sl] += jnp.sum(dk * gh, axis=0, keepdims=True)
            dgn = dk * w_ref[:, sl]
            dgs.append(r * (dgn - gh * jnp.mean(dgn * gh, axis=-1, keepdims=True)))
        dg = jnp.concatenate(dgs, axis=1)
        dy_ref[...] = dg * sz
        dz_ref[...] = (dg * yv * _dsilu(zv)).astype(BF16)

    zc = Z_OFF // 1024
    return _rowcall(body, S, T, [_rows(T, 1024), _rows(T, 1024, zc), _rows(T, 1024), _vec((1, 1024)), _any()],
                    [_rows(T, 1024), _rows(T, 1024, zc), _vec((1, 1024))],
                    [_sds((S, 1024), F32), _sds(dP.shape, BF16), _sds((1, 1024), F32)], name,
                    aliases={4: 1})(y, P, dout, nw, dP)


HEAD_W = 3 * ATTN_HEAD_DIM
HEAD_BLOCKS = NP // HEAD_W


def _slope(gi, h):
    c = [float(v) for v in SLOPES[gi]]
    return jnp.where(h == 0, c[0], jnp.where(h == 1, c[1], jnp.where(h == 2, c[2], c[3])))


def _rmsn(t):
    r = lax.rsqrt(jnp.mean(t * t, axis=-1, keepdims=True) + EPS)
    return t * r, r


def _rms_bwd(th, r, w, dn):
    dth = dn * w
    return r * (dth - th * jnp.mean(dth * th, axis=-1, keepdims=True))


ATT_ROWS = 2048


def _strided(r0, d):
    return pl.ds(r0, ATTN_STEPS, stride=d) if d > 1 else pl.ds(r0, ATTN_STEPS)


def attn_fwd(P, qw, kw, gi, name):
    S = P.shape[0]
    d = DILATIONS[gi]
    B = ATTN_STEPS
    E = ATTN_HEAD_DIM
    BR = B * d
    nq = ATT_ROWS // BR
    nsb = S // ATT_ROWS
    scale = E ** -0.5

    def body(q_ref, k_ref, v_ref, kp_ref, vp_ref, qw_ref, kw_ref, o_ref, l_ref):
        h = pl.program_id(0)
        s = pl.program_id(1)
        slope = _slope(gi, h) * float(d)
        qwv = qw_ref[...]
        kwv = kw_ref[...]
        qi = _iota((B, B), 0)
        kj = _iota((B, B), 1)
        rel_c = (qi - kj).astype(F32)
        rel_p = rel_c + float(B)
        bias_c = jnp.where(qi >= kj, -slope * rel_c, NEG)
        bias_p0 = jnp.where(jnp.logical_and(qi <= kj, s > 0), -slope * rel_p, NEG)
        bias_p = jnp.where(qi <= kj, -slope * rel_p, NEG)
        for j in range(nq):
            for r in range(d):
                rows = _strided(j * BR + r, d)
                if j == 0:
                    kpv, vpv, bp = kp_ref[_strided(r, d), :], vp_ref[_strided(r, d), :], bias_p0
                else:
                    prow = _strided((j - 1) * BR + r, d)
                    kpv, vpv, bp = k_ref[prow, :], v_ref[prow, :], bias_p
                qh, _ = _rmsn(q_ref[rows, :])
                kh, _ = _rmsn(k_ref[rows, :])
                kph, _ = _rmsn(kpv)
                qn = (qh * qwv).astype(BF16)
                kn = (kh * kwv).astype(BF16)
                kpn = (kph * kwv).astype(BF16)
                s_c = _nt(qn, kn) * scale + bias_c
                s_p = _nt(qn, kpn) * scale + bp
                m = jnp.maximum(jnp.max(s_c, axis=1, keepdims=True), jnp.max(s_p, axis=1, keepdims=True))
                e_c = jnp.exp(s_c - m)
                e_p = jnp.exp(s_p - m)
                l = jnp.sum(e_c, axis=1, keepdims=True) + jnp.sum(e_p, axis=1, keepdims=True)
                lse = m + jnp.log(l)
                inv = 1.0 / l
                p_c = (e_c * inv).astype(BF16)
                p_p = (e_p * inv).astype(BF16)
                o_ref[rows, :] = _nn(p_c, v_ref[rows, :].astype(BF16)) + _nn(p_p, vpv.astype(BF16))
                l_ref[rows, :] = jnp.broadcast_to(lse, (B, E))

    def cur(i):
        return pl.BlockSpec((ATT_ROWS, E), lambda h, s: (s, (gi * 4 + h) * 3 + i))

    def prev(i):
        return pl.BlockSpec((BR, E), lambda h, s: (jnp.maximum(s * nq - 1, 0), (gi * 4 + h) * 3 + i))

    wv = pl.BlockSpec((1, E), lambda h, s: (0, 0))
    ob = pl.BlockSpec((ATT_ROWS, E), lambda h, s: (s, h))
    return pl.pallas_call(
        body, grid=(4, nsb), in_specs=[cur(0), cur(1), cur(2), prev(1), prev(2), wv, wv],
        out_specs=[ob, ob], out_shape=[_sds((S, 512), F32), _sds((S, 512), F32)], name=name,
        compiler_params=_cp(("arbitrary", "arbitrary")))(P, P, P, P, P, qw, kw)


def attn_bwd(P, o, lse, do, dlse, dP, qw, kw, gi, name):
    S = P.shape[0]
    d = DILATIONS[gi]
    B = ATTN_STEPS
    E = ATTN_HEAD_DIM
    BR = B * d
    nq = ATT_ROWS // BR
    nsb = S // ATT_ROWS
    nblk = S // BR
    scale = E ** -0.5

    def body(q_ref, k_ref, v_ref, kp_ref, vp_ref, qx_ref, o_ref, l_ref, do_ref, dl_ref, ox_ref, lx_ref, dox_ref, dlx_ref,
             qw_ref, kw_ref, dp_in, dp_ref, dqw_ref, dkw_ref, stage_q, stage_k, stage_v):
        h = pl.program_id(0)
        s = pl.program_id(1)

        @pl.when(jnp.logical_and(h == 0, s == 0))
        def _():
            dqw_ref[...] = jnp.zeros_like(dqw_ref)
            dkw_ref[...] = jnp.zeros_like(dkw_ref)

        slope = _slope(gi, h) * float(d)
        qwv = qw_ref[...]
        kwv = kw_ref[...]
        qi = _iota((B, B), 0)
        kj = _iota((B, B), 1)
        rel_c = (qi - kj).astype(F32)
        rel_p = rel_c + float(B)
        band = qi <= kj
        bias_c = jnp.where(qi >= kj, -slope * rel_c, NEG)
        bias_p = jnp.where(band, -slope * rel_p, NEG)
        bias_p_first = jnp.where(jnp.logical_and(band, s > 0), -slope * rel_p, NEG)
        bias_p_last = jnp.where(jnp.logical_and(band, s < nsb - 1), -slope * rel_p, NEG)
        dqw = jnp.zeros((1, E), F32)
        dkw = jnp.zeros((1, E), F32)
        for j in range(nq):
            for r in range(d):
                rows = _strided(j * BR + r, d)
                if j == 0:
                    kpv, vpv, bB = kp_ref[_strided(r, d), :], vp_ref[_strided(r, d), :], bias_p_first
                else:
                    prow = _strided((j - 1) * BR + r, d)
                    kpv, vpv, bB = k_ref[prow, :], v_ref[prow, :], bias_p
                if j == nq - 1:
                    xr = _strided(r, d)
                    qxv, oxv, lxv, doxv, dlxv, bC = (qx_ref[xr, :], ox_ref[xr, :], lx_ref[xr, :], dox_ref[xr, :],
                                                     dlx_ref[xr, :], bias_p_last)
                else:
                    xr = _strided((j + 1) * BR + r, d)
                    qxv, oxv, lxv, doxv, dlxv, bC = (q_ref[xr, :], o_ref[xr, :], l_ref[xr, :], do_ref[xr, :],
                                                     dl_ref[xr, :], bias_p)
                qh, rq = _rmsn(q_ref[rows, :])
                kh, rk = _rmsn(k_ref[rows, :])
                kph, _ = _rmsn(kpv)
                qxh, _ = _rmsn(qxv)
                qn = (qh * qwv).astype(BF16)
                kn = (kh * kwv).astype(BF16)
                kpn = (kph * kwv).astype(BF16)
                qxn = (qxh * qwv).astype(BF16)
                v = v_ref[rows, :].astype(BF16)
                vp = vpv.astype(BF16)
                lse_c = l_ref[rows, :]
                pA = jnp.exp(_nt(qn, kn) * scale + bias_c - lse_c)
                pB = jnp.exp(_nt(qn, kpn) * scale + bB - lse_c)
                pC = jnp.exp(_nt(qxn, kn) * scale + bC - lxv)
                doc = do_ref[rows, :]
                docb = doc.astype(BF16)
                doxb = doxv.astype(BF16)
                corr_c = jnp.sum(dl_ref[rows, :] - doc * o_ref[rows, :], axis=1, keepdims=True)
                corr_n = jnp.sum(dlxv - doxv * oxv, axis=1, keepdims=True)
                dsA = (pA * (_nt(docb, v) + corr_c)).astype(BF16)
                dsB = (pB * (_nt(docb, vp) + corr_c)).astype(BF16)
                dsC = (pC * (_nt(doxb, v) + corr_n)).astype(BF16)
                dqn = (_nn(dsA, kn) + _nn(dsB, kpn)) * scale
                dkn = (_tn(dsA, qn) + _tn(dsC, qxn)) * scale
                dv = _tn(pA.astype(BF16), docb) + _tn(pC.astype(BF16), doxb)
                dqw = dqw + jnp.sum(dqn * qh, axis=0, keepdims=True)
                dkw = dkw + jnp.sum(dkn * kh, axis=0, keepdims=True)
                stage_q[rows, :] = _rms_bwd(qh, rq, qwv, dqn)
                stage_k[rows, :] = _rms_bwd(kh, rk, kwv, dkn)
                stage_v[rows, :] = dv
        dqw_ref[...] += dqw
        dkw_ref[...] += dkw
        dp_ref[:, 0:E] = stage_q[...].astype(BF16)
        dp_ref[:, E:2 * E] = stage_k[...].astype(BF16)
        dp_ref[:, 2 * E:3 * E] = stage_v[...].astype(BF16)

    def cur(i):
        return pl.BlockSpec((ATT_ROWS, E), lambda h, s: (s, (gi * 4 + h) * 3 + i))

    def prev(i):
        return pl.BlockSpec((BR, E), lambda h, s: (jnp.maximum(s * nq - 1, 0), (gi * 4 + h) * 3 + i))

    nxt_q = pl.BlockSpec((BR, E), lambda h, s: (jnp.minimum((s + 1) * nq, nblk - 1), (gi * 4 + h) * 3))
    ocur = pl.BlockSpec((ATT_ROWS, E), lambda h, s: (s, h))
    onxt = pl.BlockSpec((BR, E), lambda h, s: (jnp.minimum((s + 1) * nq, nblk - 1), h))
    wv = pl.BlockSpec((1, E), lambda h, s: (0, 0))
    return pl.pallas_call(
        body, grid=(4, nsb),
        in_specs=[cur(0), cur(1), cur(2), prev(1), prev(2), nxt_q, ocur, ocur, ocur, ocur, onxt, onxt, onxt, onxt, wv, wv,
                  _any()],
        out_specs=[pl.BlockSpec((ATT_ROWS, HEAD_W), lambda h, s: (s, gi * 4 + h)), wv, wv],
        out_shape=[_sds(dP.shape, BF16), _sds((1, E), F32), _sds((1, E), F32)],
        scratch_shapes=[pltpu.VMEM((ATT_ROWS, E), F32)] * 3,
        input_output_aliases={16: 0}, name=name,
        compiler_params=_cp(("arbitrary", "arbitrary")))(
            P, P, P, P, P, P, o, lse, do, dlse, o, lse, do, dlse, qw, kw, dP)


def attn_merge_fwd(os_, ls_, name):
    S = os_[0].shape[0]
    T = ROW_TILE

    def body(o0, o1, o2, l0, l1, l2, out):
        a, b, c = l0[...], l1[...], l2[...]
        m = jnp.maximum(jnp.maximum(a, b), c)
        ea, eb, ec = jnp.exp(a - m), jnp.exp(b - m), jnp.exp(c - m)
        inv = 1.0 / (ea + eb + ec)
        out[...] = ((ea * o0[...] + eb * o1[...] + ec * o2[...]) * inv).astype(BF16)

    return _rowcall(body, S, T, [_rows(T, 512)] * 6, _rows(T, 512), _sds((S, 512), BF16), name)(*os_, *ls_)


def attn_merge_bwd(os_, ls_, dom, name):
    S = os_[0].shape[0]
    T = ROW_TILE

    def body(o0, o1, o2, l0, l1, l2, d_ref, do0, do1, do2, dl0, dl1, dl2):
        a, b, c = l0[...], l1[...], l2[...]
        m = jnp.maximum(jnp.maximum(a, b), c)
        ea, eb, ec = jnp.exp(a - m), jnp.exp(b - m), jnp.exp(c - m)
        inv = 1.0 / (ea + eb + ec)
        wa, wb, wc = ea * inv, eb * inv, ec * inv
        dv = d_ref[...]
        do0[...] = wa * dv
        do1[...] = wb * dv
        do2[...] = wc * dv
        ga, gb, gc = dv * o0[...], dv * o1[...], dv * o2[...]
        tot = wa * ga + wb * gb + wc * gc
        dl0[...] = wa * (ga - tot)
        dl1[...] = wb * (gb - tot)
        dl2[...] = wc * (gc - tot)

    sp = _rows(T, 512)
    return _rowcall(body, S, T, [sp] * 7, [sp] * 6, [_sds((S, 512), F32)] * 6, name)(*os_, *ls_, dom)


def _counts(i, T, rows, w, offset=0):
    t = i * T + offset + _iota((rows, 1), 0)
    return jnp.minimum(t + 1, w).astype(F32)


def pool_fwd(P, wmix, scale, name):
    S = P.shape[0]
    T = ROW_TILE
    GW = 256
    uc = U_OFF // 1024
    hb = T // POOL_HALO

    def body(u_ref, h_ref, wm_ref, sc_ref, pl_ref, yp_ref):
        i = pl.program_id(0)
        u = u_ref[...]
        ext = jnp.concatenate([jnp.where(i > 0, h_ref[...], 0.0), u], axis=0)
        for g, w in enumerate(POOL_WINDOWS):
            sl = slice(g * GW, (g + 1) * GW)
            s = ext[:, sl]
            k = 1
            while k < w:
                s = s + pltpu.roll(s, k, 0)
                k *= 2
            pooled = (s[POOL_HALO:] / _counts(i, T, T, w) - u[:, sl]).astype(BF16)
            pl_ref[:, sl] = pooled
            yp_ref[:, sl] = (_nn(pooled, wm_ref[g]) * sc_ref[:, sl]).astype(BF16)

    return _rowcall(
        body, S, T,
        [_rows(T, 1024, uc), pl.BlockSpec((POOL_HALO, 1024), lambda i: (jnp.maximum(i * hb - 1, 0), uc)),
         _vec((4, GW, GW)), _vec((1, 1024))],
        [_rows(T, 1024), _rows(T, 1024)], [_sds((S, 1024), BF16), _sds((S, 1024), BF16)], name)(P, P, wmix, scale)


def pool_bwd(dyp, pooled, dP, wmix, scale, name):
    S = dyp.shape[0]
    T = ROW_TILE
    nt = S // T
    GW = 256
    uc = U_OFF // 1024
    hb = T // POOL_HALO
    TE = T + POOL_HALO

    def body(d_ref, dn_ref, p_ref, wm_ref, sc_ref, dp_in, du_ref, dwm_ref, dsc_ref):
        i = pl.program_id(0)
        _zero_at_first(i, dwm_ref, dsc_ref)
        dv = d_ref[...]
        dn = jnp.where(i < nt - 1, dn_ref[...], 0.0)
        for g, w in enumerate(POOL_WINDOWS):
            sl = slice(g * GW, (g + 1) * GW)
            pg = p_ref[:, sl]
            wm = wm_ref[g]
            dsc_ref[:, sl] += jnp.sum(dv[:, sl] * _nn(pg, wm), axis=0, keepdims=True)
            dmb = (dv[:, sl] * sc_ref[:, sl]).astype(BF16)
            dmnb = (dn[:, sl] * sc_ref[:, sl]).astype(BF16)
            dwm_ref[g] += _tn(pg, dmb)
            dpl = _nt(dmb, wm)
            dpln = _nt(dmnb, wm)
            s = jnp.concatenate([dpl / _counts(i, T, T, w), dpln / _counts(i, T, POOL_HALO, w, T)], axis=0)
            k = 1
            while k < w:
                s = s + pltpu.roll(s, TE - k, 0)
                k *= 2
            du_ref[:, sl] = (s[:T] - dpl).astype(BF16)

    return _rowcall(
        body, S, T,
        [_rows(T, 1024), pl.BlockSpec((POOL_HALO, 1024), lambda i: (jnp.minimum((i + 1) * hb, S // POOL_HALO - 1), 0)),
         _rows(T, 1024), _vec((4, GW, GW)), _vec((1, 1024)), _any()],
        [_rows(T, 1024, uc), _vec((4, GW, GW)), _vec((1, 1024))],
        [_sds(dP.shape, BF16), _sds((4, GW, GW), F32), _sds((1, 1024), F32)], name,
        aliases={5: 0})(dyp, dyp, pooled, wmix, scale, dP)


def gate_fwd(P, ys, ya, yp, name):
    S = ys.shape[0]
    T = ROW_TILE
    gc = GATE_OFF // 1024

    def body(g0, g1, g2, a, b, c, out):
        out[...] = (_sigmoid(g0[...]) * a[...] + _sigmoid(g1[...]) * b[...] + _sigmoid(g2[...]) * c[...]).astype(BF16)

    r = _rows(T, 1024)
    return _rowcall(body, S, T, [_rows(T, 1024, gc), _rows(T, 1024, gc + 1), _rows(T, 1024, gc + 2), r, r, r], r,
                    _sds((S, 1024), BF16), name)(P, P, P, ys, ya, yp)


def gate_bwd(P, ys, ya, yp, dm, dP, name):
    S = ys.shape[0]
    T = ROW_TILE
    gc = GATE_OFF // 1024

    def body(g0, g1, g2, a, b, c, d_ref, dp_in, dg_ref, da, db, dc):
        dv = d_ref[...]
        outs = []
        for gr, yr, dyr in ((g0, a, da), (g1, b, db), (g2, c, dc)):
            s = _sigmoid(gr[...])
            dyr[...] = (s * dv).astype(BF16)
            outs.append(dv * yr[...] * s * (1.0 - s))
        dg_ref[...] = jnp.concatenate(outs, axis=1).astype(BF16)

    r = _rows(T, 1024)
    return _rowcall(body, S, T, [_rows(T, 1024, gc), _rows(T, 1024, gc + 1), _rows(T, 1024, gc + 2), r, r, r, r, _any()],
                    [_rows(T, GATE_W, GATE_OFF // GATE_W), r, r, r],
                    [_sds(dP.shape, BF16)] + [_sds((S, 1024), BF16)] * 3, name,
                    aliases={7: 0})(P, P, P, ys, ya, yp, dm, dP)


def resid_bwd(dx, g, y, name):
    S, D = dx.shape
    T = ROW_TILE

    def body(d_ref, g_ref, y_ref, dy_ref, dg_ref):
        _zero_at_first(pl.program_id(0), dg_ref)
        dv = d_ref[...]
        dy_ref[...] = (g_ref[...] * dv).astype(BF16)
        dg_ref[...] += jnp.sum(dv * y_ref[...], axis=0, keepdims=True)

    r = _rows(T, D)
    return _rowcall(body, S, T, [r, _vec((1, D)), r], [r, _vec((1, D))], [_sds((S, D), BF16), _sds((1, D), F32)],
                    name)(dx, g, y)


def loss_fwd_bwd(y, target, name):
    S, D = y.shape
    T = ROW_TILE

    def body(y_ref, t_ref, acc_ref, dy_ref):
        _zero_at_first(pl.program_id(0), acc_ref)
        e = y_ref[...] - t_ref[...]
        acc_ref[...] += jnp.sum(e * e, axis=0, keepdims=True)
        dy_ref[...] = e * (1.0 / D)

    r = _rows(T, D)
    return _rowcall(body, S, T, [r, r], [_vec((1, D)), r], [_sds((1, D), F32), _sds((S, D), F32)], name)(y, target)


def silu_rows(c, name):
    def body(c_ref, o_ref):
        o_ref[...] = _silu(c_ref[...])

    return pl.pallas_call(body, out_shape=_sds(c.shape, F32), name=name)(c)


def adamw(parts, w, m, v, name, row0=0, prev=None):
    n, R, C = parts.shape
    Rt = w.shape[0]
    tr = R
    while tr * C * 4 > (1 << 20) and tr % 16 == 0:
        tr //= 2
    b0 = row0 // tr
    c1 = 1.0 / (1.0 - ADAM_B1 ** ADAM_STEP)
    c2 = 1.0 / (1.0 - ADAM_B2 ** ADAM_STEP)

    def body(p_ref, w_ref, m_ref, v_ref, *rest):
        g_ref, d_ref, nm_ref, nv_ref = rest[-4:]
        g = p_ref[0].astype(F32)
        for k in range(1, n):
            g = g + p_ref[k].astype(F32)
        nm = ADAM_B1 * m_ref[...] + (1.0 - ADAM_B1) * g
        nv = ADAM_B2 * v_ref[...] + (1.0 - ADAM_B2) * (g * g)
        g_ref[...] = g
        nm_ref[...] = nm
        nv_ref[...] = nv
        d_ref[...] = -ADAM_LR * ((nm * c1) / (jnp.sqrt(nv * c2) + ADAM_EPS) + ADAM_WD * w_ref[...])

    r = pl.BlockSpec((tr, C), lambda i: (b0 + i, 0))
    ins = [parts, w, m, v] + (list(prev) if prev is not None else [])
    in_specs = [pl.BlockSpec((n, tr, C), lambda i: (0, i, 0)), r, r, r] + ([_any()] * 4 if prev is not None else [])
    aliases = {4 + k: k for k in range(4)} if prev is not None else None
    return _rowcall(body, R, tr, in_specs, [r, r, r, r], [_sds((Rt, C), F32)] * 4, name, aliases=aliases)(*ins)


FWD_HOSTS = {"in_proj": ("w_in", "w_attn_out", "w_pool_mix"), "ff1": ("w_ff1", "w_ssd_out"),
             "ff2": ("w_ff2", "w_pool_out", "w_out")}
W_IN_PARTS = 4
BWD_HOSTS = {"d_w_ff2": ("w_in_0",), "d_a": ("w_in_1",), "d_w_ff1": ("w_in_2",), "d_h2": ("w_in_3",),
             "d_w_in": ("w_ff1", "w_out", "w_ssd_out"), "d_h1": ("w_ff2", "w_attn_out", "w_pool_mix", "w_pool_out")}


def _ride(mmargs, kind, hosts, host, cargo, got, **kw):
    if not cargo or hosts[host][0] not in cargo:
        return mm(*mmargs, **kw)
    out, res = mm(*mmargs, rider=Rider(kind, [cargo[k] for k in hosts[host]]), **kw)
    got.update(zip(hosts[host], res))
    return out


def _epi_relu2(r):
    p = jnp.maximum(r, 0.0)
    return r, p * p


def _epi_relu2_bwd(r, f):
    return (2.0 * jnp.maximum(f, 0.0) * r,)


def _epi_resid(r, x, g):
    return r, x + g * r


def layer_fwd(x, mod, W, sm, tag, cargo=None):
    sh1, sc1, g1, sh2, sc2, g2 = mod
    sv, got = {}, {}
    ride = functools.partial(_ride, kind="gather", hosts=FWD_HOSTS, cargo=cargo, got=got)
    h1 = rms_mod_fwd(x, sm["norm1_w"], sc1, sh1, f"rms1_fwd_{tag}")
    P = ride((h1, W["in"], "nn", F32, f"in_proj_{tag}"), host="in_proj")
    pre, xc = conv_fwd(P, sm["conv_w"], sm["conv_b"], f"conv_fwd_{tag}")
    y, st = ssd_fwd(xc, P, sm["dt_bias"], sm["a_log"], sm["d_skip"], f"ssd_fwd_{tag}")
    ysn = ssd_post_fwd(y, P, sm["ssd_norm_w"], f"ssd_post_fwd_{tag}")
    ys = mm(ysn, W["ssd_out"], "nn", F32, f"ssd_out_{tag}")
    os_, ls_ = [], []
    for gi in range(3):
        o, l = attn_fwd(P, sm["q_norm_w"], sm["k_norm_w"], gi, f"attn_fwd{gi}_{tag}")
        os_.append(o)
        ls_.append(l)
    om = attn_merge_fwd(os_, ls_, f"attn_merge_fwd_{tag}")
    ya = mm(om, W["attn_out"], "nn", F32, f"attn_out_{tag}")
    pooled, ypp = pool_fwd(P, W["pool_mix"], sm["pool_scale"], f"pool_fwd_{tag}")
    yp = mm(ypp, W["pool_out"], "nn", F32, f"pool_out_{tag}")
    m = gate_fwd(P, ys, ya, yp, f"gate_fwd_{tag}")
    mo, x1 = mm(m, W["out"], "nn", F32, f"mix_out_{tag}", extra=(x, g1), epi=_epi_resid, out_dtypes=(F32, F32))
    h2 = rms_mod_fwd(x1, sm["norm2_w"], sc2, sh2, f"rms2_fwd_{tag}")
    f, a = ride((h2, W["ff1"], "nn", F32, f"ff1_{tag}"), host="ff1", epi=_epi_relu2, out_dtypes=(F32, BF16),
                col_blocks=True)
    o2, x2 = ride((a, W["ff2"], "nn", F32, f"ff2_{tag}"), host="ff2", extra=(x1, g2), epi=_epi_resid,
                  out_dtypes=(F32, F32))
    sv.update(x=x, h1=h1, P=P, pre=pre, xc=xc, y=y, st=st, ysn=ysn, ys=ys, os=os_, ls=ls_, om=om, ya=ya,
              pooled=pooled, ypp=ypp, yp=yp, m=m, mo=mo, x1=x1, h2=h2, f=f, a=a, o2=o2)
    return x2, sv, got


def layer_bwd(dx2, sv, mod, W, sm, tag, slabs, cargo=None):
    sh1, sc1, g1, sh2, sc2, g2 = mod
    S = dx2.shape[0]
    P = sv["P"]
    gw, gs, got, own, got_own = {}, {}, {}, {}, {}
    ride = functools.partial(_ride, kind="scatter", hosts=BWD_HOSTS, cargo=cargo, got=got)
    ride_own = functools.partial(_ride, kind="scatter", hosts=BWD_HOSTS, cargo=own, got=got_own)
    do2, dg2 = resid_bwd(dx2, g2, sv["o2"], f"resid2_bwd_{tag}")
    gw["ff2"] = ride((sv["a"], do2, "tn", BF16, f"d_w_ff2_{tag}"), host="d_w_ff2")
    df = ride((do2, W["ff2"], "nt", BF16, f"d_f_{tag}"), host="d_a", extra=(sv["f"],), epi=_epi_relu2_bwd)
    gw["ff1"] = ride((sv["h2"], df, "tn", BF16, f"d_w_ff1_{tag}"), host="d_w_ff1", col_blocks=True)
    dh2 = ride((df, W["ff1"], "nt", F32, f"d_h2_{tag}"), host="d_h2", col_blocks=True)
    dx1, gs["norm2_w"], dsc2, dsh2 = rms_mod_bwd(sv["x1"], dh2, dx2, sm["norm2_w"], sc2, f"rms2_bwd_{tag}")
    dmo, dg1 = resid_bwd(dx1, g1, sv["mo"], f"resid1_bwd_{tag}")
    gw["out"] = mm(sv["m"], dmo, "tn", BF16, f"d_w_out_{tag}")
    dm = mm(dmo, W["out"], "nt", F32, f"d_m_{tag}")
    dP = lax.empty((S, NP), BF16)
    dP, dys, dya, dyp = gate_bwd(P, sv["ys"], sv["ya"], sv["yp"], dm, dP, f"gate_bwd_{tag}")
    gw["ssd_out"] = mm(sv["ysn"], dys, "tn", BF16, f"d_w_ssd_out_{tag}")
    dysn = mm(dys, W["ssd_out"], "nt", F32, f"d_ysn_{tag}")
    dy, dP, gs["ssd_norm_w"] = ssd_post_bwd(sv["y"], P, dysn, dP, sm["ssd_norm_w"], f"ssd_post_bwd_{tag}")
    dxc, dP, gs["dt_bias"], gs["a_log"], dDl = ssd_bwd(sv["xc"], P, sv["st"], dy, dP, sm["dt_bias"], sm["a_log"],
                                                      sm["d_skip"], f"ssd_bwd_{tag}")
    gs["d_skip"] = dDl.reshape(SSD_HEADS, SSD_HEAD_DIM).sum(axis=1)
    gs["dt_bias"] = gs["dt_bias"][0, :SSD_HEADS]
    gs["a_log"] = gs["a_log"][0, :SSD_HEADS]
    dP, gs["conv_w"], gs["conv_b"] = conv_bwd(dxc, sv["pre"], P, dP, sm["conv_w"], f"conv_bwd_{tag}")
    gw["attn_out"] = mm(sv["om"], dya, "tn", BF16, f"d_w_attn_out_{tag}")
    dom = mm(dya, W["attn_out"], "nt", F32, f"d_om_{tag}")
    mb = attn_merge_bwd(sv["os"], sv["ls"], dom, f"attn_merge_bwd_{tag}")
    dqw = dkw = None
    for gi in range(3):
        dP, a_, b_ = attn_bwd(P, sv["os"][gi], sv["ls"][gi], mb[gi], mb[3 + gi], dP, sm["q_norm_w"], sm["k_norm_w"], gi,
                              f"attn_bwd{gi}_{tag}")
        dqw = a_ if dqw is None else dqw + a_
        dkw = b_ if dkw is None else dkw + b_
    gs["q_norm_w"], gs["k_norm_w"] = dqw, dkw
    gw["pool_out"] = mm(sv["ypp"], dyp, "tn", BF16, f"d_w_pool_out_{tag}")
    dypp = mm(dyp, W["pool_out"], "nt", F32, f"d_ypp_{tag}")
    dP, dwm, gs["pool_scale"] = pool_bwd(dypp, sv["pooled"], dP, W["pool_mix"], sm["pool_scale"], f"pool_bwd_{tag}")
    gw["pool_mix"] = dwm.astype(BF16)
    own.update(slabs(gw))
    gw["in"] = ride_own((sv["h1"], dP, "tn", BF16, f"d_w_in_{tag}"), host="d_w_in")
    dh1 = ride_own((dP, W["in"], "nt", F32, f"d_h1_{tag}"), host="d_h1")
    dx, gs["norm1_w"], dsc1, dsh1 = rms_mod_bwd(sv["x"], dh1, dx1, sm["norm1_w"], sc1, f"rms1_bwd_{tag}")
    dmod = jnp.concatenate([dsh1, dsc1, dg1, dsh2, dsc2, dg2], axis=1)
    return dx, dmod, gs, gw, slabs({"in": gw["in"]}), got, got_own


def _w_in_to_fused(w):
    E = ATTN_HEAD_DIM
    cols = []
    for h in range(ATTN_HEADS):
        for off in (O_Q, O_K, O_V):
            cols.append(w[:, off + h * E: off + (h + 1) * E])
    cols += [w[:, O_XBC:O_DT], w[:, O_G:IN_WIDTH], w[:, O_Z:O_XBC], w[:, O_U:O_G], w[:, O_DT:O_Q],
             jnp.zeros((w.shape[0], DT_W - SSD_HEADS), w.dtype)]
    return jnp.concatenate(cols, axis=1)


def _fused_to_w_in(g):
    E = ATTN_HEAD_DIM
    q = [g[:, h * HEAD_W: h * HEAD_W + E] for h in range(ATTN_HEADS)]
    k = [g[:, h * HEAD_W + E: h * HEAD_W + 2 * E] for h in range(ATTN_HEADS)]
    v = [g[:, h * HEAD_W + 2 * E: (h + 1) * HEAD_W] for h in range(ATTN_HEADS)]
    return jnp.concatenate([g[:, Z_OFF:Z_OFF + Z_W], g[:, XBC_OFF:XBC_OFF + XBC_W], g[:, DT_OFF:DT_OFF + SSD_HEADS]]
                           + q + k + v + [g[:, U_OFF:U_OFF + U_W], g[:, GATE_OFF:GATE_OFF + GATE_W]], axis=1)


def _cols_from_shards(g):
    return jnp.moveaxis(g, 0, 1).reshape(g.shape[1], N_DEV * g.shape[2])


def _cols_to_shards(w):
    R = w.shape[0]
    return jnp.moveaxis(w.reshape(R, N_DEV, w.shape[1] // N_DEV), 1, 0)


BIG = ("w_in", "w_ssd_out", "w_attn_out", "w_pool_mix", "w_pool_out", "w_out", "w_ff1", "w_ff2")
SMALL = ("b_ada", "norm1_w", "norm2_w", "conv_b", "dt_bias", "a_log", "d_skip", "ssd_norm_w", "q_norm_w", "k_norm_w",
         "pool_scale")
WEIGHTS = ("w_ada", "b_ada", "norm1_w", "norm2_w", "w_in", "conv_w", "conv_b", "dt_bias", "a_log", "d_skip",
           "ssd_norm_w", "w_ssd_out", "q_norm_w", "k_norm_w", "w_attn_out", "w_pool_mix", "pool_scale", "w_pool_out",
           "w_out", "w_ff1", "w_ff2")


def _pack(arrs):
    flat = jnp.concatenate([a.reshape(-1).astype(F32) for a in arrs])
    pad = (-flat.shape[0]) % (8 * LANES)
    return jnp.pad(flat, (0, pad)).reshape(-1, LANES)


def _unpack(packed, shapes):
    flat = packed.reshape(-1)
    out, off = [], 0
    for s in shapes:
        n = int(np.prod(s))
        out.append(flat[off:off + n].reshape(s))
        off += n
    return out


def _pad_lanes(v):
    return jnp.pad(v, ((0, 0), (0, LANES - v.shape[1])))[:, None, :]


def _step(x, c, target, p, m_, v_):
    Ld = p["w_in"].shape[0]
    me = 4 * lax.axis_index("x") + 2 * lax.axis_index("y") + lax.axis_index("c")

    local_bf16 = {k: p[k].astype(BF16) for k in BIG}

    def local_blocks(l):
        return {k: (local_bf16[k], l) for k in BIG}

    blk = local_blocks(0)
    gathered = all_gather([blk[k] for k in BIG] + [c, p["conv_w"]], "gather_first")
    G = dict(zip(BIG, gathered[:len(BIG)]))
    c_all = gathered[len(BIG)].reshape(N_DEV, D_MODEL)
    conv_w_full = jnp.moveaxis(gathered[len(BIG) + 1], 0, 2).reshape(Ld, 4, XBC_W)

    def layer_weights(G):
        W = {}
        W["in"] = _w_in_to_fused(_cols_from_shards(G["w_in"]))
        W["ssd_out"] = G["w_ssd_out"].reshape(1024, D_MODEL)
        W["attn_out"] = _cols_from_shards(G["w_attn_out"])
        W["pool_mix"] = jnp.moveaxis(G["w_pool_mix"], 0, 1).reshape(4, 256, 256)
        W["pool_out"] = G["w_pool_out"].reshape(1024, D_MODEL)
        W["out"] = G["w_out"].reshape(D_MODEL, D_MODEL)
        W["ff1"] = G["w_ff1"]
        W["ff2"] = G["w_ff2"].reshape(4 * D_MODEL, D_MODEL)
        return W

    def slabs(gw):
        out = {}
        if "in" in gw:
            full = _cols_to_shards(_fused_to_w_in(gw["in"]))
            rows = full.shape[1] // W_IN_PARTS
            for q in range(W_IN_PARTS):
                out[f"w_in_{q}"] = full[:, q * rows:(q + 1) * rows]
            return out
        out["w_ssd_out"] = gw["ssd_out"].reshape(N_DEV, -1, D_MODEL)
        out["w_attn_out"] = _cols_to_shards(gw["attn_out"])
        out["w_pool_mix"] = jnp.moveaxis(gw["pool_mix"].reshape(4, N_DEV, 32, 256), 1, 0)
        out["w_pool_out"] = gw["pool_out"].reshape(N_DEV, -1, D_MODEL)
        out["w_out"] = gw["out"].reshape(N_DEV, -1, D_MODEL)
        out["w_ff1"] = gw["ff1"]
        out["w_ff2"] = gw["ff2"].reshape(N_DEV, -1, D_MODEL)
        return out

    def layer_small(l):
        sm = {k: p[k][l][None, :] for k in ("norm1_w", "norm2_w", "conv_b", "ssd_norm_w", "q_norm_w", "k_norm_w",
                                            "pool_scale")}
        for k in ("dt_bias", "a_log", "d_skip"):
            sm[k] = jnp.pad(p[k][l], (0, LANES - SSD_HEADS))[None, :]
        sm["conv_w"] = conv_w_full[l]
        return sm

    cond = silu_rows(c_all, "cond_silu")
    ncol = p["w_ada"].shape[2]
    b_mine = lax.dynamic_slice_in_dim(p["b_ada"], me * ncol, ncol, axis=1)
    parts = [mm(cond, p["w_ada"][l], "nn", F32, f"ada_{l}") + b_mine[l][None, :] for l in range(Ld)]
    mod_cols = jnp.stack(parts, axis=1)
    mod = all_to_all([mod_cols], "scatter_mod")[0]
    mod = jnp.moveaxis(mod, 0, 1).reshape(Ld, 6 * D_MODEL)

    def mods(l):
        return [mod[l, k * D_MODEL:(k + 1) * D_MODEL][None, :] for k in range(6)]

    saved, Ws, sms = [], [], []
    h = x
    for l in range(Ld):
        Ws.append(layer_weights(G))
        sms.append(layer_small(l))
        h, sv, G = layer_fwd(h, mods(l), Ws[l], sms[l], f"l{l}", local_blocks(l + 1) if l + 1 < Ld else None)
        saved.append(sv)
    lsum, dh = loss_fwd_bwd(h, target, "loss")
    loss = 0.5 / D_MODEL * jnp.sum(lsum)
    gss, dmods, recvs = [None] * Ld, [None] * Ld, [dict() for _ in range(Ld)]
    cargo = None
    for l in reversed(range(Ld)):
        dh, dmods[l], gss[l], _, nxt, got, got_own = layer_bwd(dh, saved[l], mods(l), Ws[l], sms[l], f"l{l}", slabs, cargo)
        recvs[l].update(got_own)
        if cargo is not None:
            recvs[l + 1].update(got)
        cargo = nxt
    dmod = jnp.concatenate(dmods, axis=0)
    dmod_cols = jnp.moveaxis(dmod.reshape(Ld, N_DEV, ncol), 1, 0)
    parts_in = [f"w_in_{q}" for q in range(W_IN_PARTS)]
    last = all_to_all([cargo[k] for k in parts_in] + [dmod_cols], "scatter_last")
    recvs[0].update(zip(parts_in, last[:W_IN_PARTS]))
    dmod_all = last[W_IN_PARTS]

    out_g, out_d, out_m, out_v = {}, {}, {}, {}

    def finish(k, res):
        shp = p[k].shape
        out_g[k], out_d[k], out_m[k], out_v[k] = [t.reshape(shp) for t in res]

    for k in BIG:
        C = p[k].shape[-1]
        rows = int(np.prod(p[k].shape[1:-1]))
        res = None
        for l in reversed(range(Ld)):
            pieces = [(f"{k}_{q}", rows // W_IN_PARTS) for q in range(W_IN_PARTS)] if k == "w_in" else [(k, rows)]
            for q, (key, nrows) in enumerate(pieces):
                res = adamw(recvs[l][key].reshape(N_DEV, nrows, C), p[k].reshape(-1, C), m_[k].reshape(-1, C),
                            v_[k].reshape(-1, C), f"adamw_{key}_l{l}", row0=l * rows + q * nrows, prev=res)
        finish(k, res)
    g_ada = jnp.stack([mm(cond, dmod_all[:, l], "tn", F32, f"d_w_ada_{l}") for l in range(Ld)], axis=0)
    finish("w_ada", adamw(g_ada.reshape(1, -1, ncol), p["w_ada"].reshape(-1, ncol), m_["w_ada"].reshape(-1, ncol),
                          v_["w_ada"].reshape(-1, ncol), "adamw_w_ada"))

    small_g = {k: jnp.stack([gss[l][k].reshape(p[k].shape[1:]) for l in range(Ld)], axis=0) for k in SMALL if k != "b_ada"}
    small_g["b_ada"] = dmod
    conv_g = jnp.stack([gss[l]["conv_w"] for l in range(Ld)], axis=0)
    packed = _pack([small_g[k] for k in SMALL] + [conv_g])
    parts_small = all_gather([packed], "gather_small_grads")[0]
    zeros_conv = jnp.zeros(conv_g.shape, F32)
    res = adamw(parts_small, _pack([p[k] for k in SMALL] + [zeros_conv]), _pack([m_[k] for k in SMALL] + [zeros_conv]),
                _pack([v_[k] for k in SMALL] + [zeros_conv]), "adamw_small")
    shapes = [p[k].shape for k in SMALL] + [conv_g.shape]
    unp = [_unpack(t, shapes) for t in res]
    for i, k in enumerate(SMALL):
        out_g[k], out_d[k], out_m[k], out_v[k] = unp[0][i], unp[1][i], unp[2][i], unp[3][i]
    cw = p["conv_w"].shape[-1]
    conv_mine = lax.dynamic_slice_in_dim(unp[0][len(SMALL)], me * cw, cw, axis=2)
    finish("conv_w", adamw(conv_mine.reshape(1, -1, cw), p["conv_w"].reshape(-1, cw), m_["conv_w"].reshape(-1, cw),
                           v_["conv_w"].reshape(-1, cw), "adamw_conv_w"))

    loss = lax.psum(loss, ("x", "y", "c"))
    return loss, dh, out_g, out_d, out_m, out_v


def kernel(x, c, w_ada, b_ada, norm1_w, norm2_w, w_in, conv_w, conv_b, dt_bias, a_log, d_skip, ssd_norm_w, w_ssd_out, q_norm_w, k_norm_w, w_attn_out, w_pool_mix, pool_scale, w_pool_out, w_out, w_ff1, w_ff2, loss_target, m_w_ada, m_b_ada, m_norm1_w, m_norm2_w, m_w_in, m_conv_w, m_conv_b, m_dt_bias, m_a_log, m_d_skip, m_ssd_norm_w, m_w_ssd_out, m_q_norm_w, m_k_norm_w, m_w_attn_out, m_w_pool_mix, m_pool_scale, m_w_pool_out, m_w_out, m_w_ff1, m_w_ff2, v_w_ada, v_b_ada, v_norm1_w, v_norm2_w, v_w_in, v_conv_w, v_conv_b, v_dt_bias, v_a_log, v_d_skip, v_ssd_norm_w, v_w_ssd_out, v_q_norm_w, v_k_norm_w, v_w_attn_out, v_w_pool_mix, v_pool_scale, v_w_pool_out, v_w_out, v_w_ff1, v_w_ff2):
    p = dict(w_ada=w_ada, b_ada=b_ada, norm1_w=norm1_w, norm2_w=norm2_w, w_in=w_in, conv_w=conv_w, conv_b=conv_b,
             dt_bias=dt_bias, a_log=a_log, d_skip=d_skip, ssd_norm_w=ssd_norm_w, w_ssd_out=w_ssd_out, q_norm_w=q_norm_w,
             k_norm_w=k_norm_w, w_attn_out=w_attn_out, w_pool_mix=w_pool_mix, pool_scale=pool_scale,
             w_pool_out=w_pool_out, w_out=w_out, w_ff1=w_ff1, w_ff2=w_ff2)
    m_ = dict(w_ada=m_w_ada, b_ada=m_b_ada, norm1_w=m_norm1_w, norm2_w=m_norm2_w, w_in=m_w_in, conv_w=m_conv_w,
              conv_b=m_conv_b, dt_bias=m_dt_bias, a_log=m_a_log, d_skip=m_d_skip, ssd_norm_w=m_ssd_norm_w,
              w_ssd_out=m_w_ssd_out, q_norm_w=m_q_norm_w, k_norm_w=m_k_norm_w, w_attn_out=m_w_attn_out,
              w_pool_mix=m_w_pool_mix, pool_scale=m_pool_scale, w_pool_out=m_w_pool_out, w_out=m_w_out, w_ff1=m_w_ff1,
              w_ff2=m_w_ff2)
    v_ = dict(w_ada=v_w_ada, b_ada=v_b_ada, norm1_w=v_norm1_w, norm2_w=v_norm2_w, w_in=v_w_in, conv_w=v_conv_w,
              conv_b=v_conv_b, dt_bias=v_dt_bias, a_log=v_a_log, d_skip=v_d_skip, ssd_norm_w=v_ssd_norm_w,
              w_ssd_out=v_w_ssd_out, q_norm_w=v_q_norm_w, k_norm_w=v_k_norm_w, w_attn_out=v_w_attn_out,
              w_pool_mix=v_w_pool_mix, pool_scale=v_pool_scale, w_pool_out=v_w_pool_out, w_out=v_w_out, w_ff1=v_w_ff1,
              w_ff2=v_w_ff2)
    loss, dx, g, d, nm, nv = _step(x[0], c, loss_target[0], p, m_, v_)
    return (loss, dx[None], *[g[k] for k in WEIGHTS], *[d[k] for k in WEIGHTS], *[nm[k] for k in WEIGHTS],
            *[nv[k] for k in WEIGHTS])
```

```python
import functools
import math

import numpy as np
import jax
import jax.numpy as jnp
from jax import lax
from jax.experimental import pallas as pl
from jax.experimental.pallas import tpu as pltpu

F32 = jnp.float32
BF16 = jnp.bfloat16
MESH = pl.DeviceIdType.MESH
HIGHEST = lax.Precision.HIGHEST

N_DEV = 8
V7X_VMEM_LIMIT_BYTES = 56 * 1024 * 1024
LANES = 128
ROW_TILE = 256

EPS = 1e-6
D_MODEL = 1024
SSD_HEADS = 16
SSD_HEAD_DIM = 64
SSD_CHUNK = 128
SSD_STATE = 128
ATTN_HEADS = 12
ATTN_HEAD_DIM = 128
ATTN_STEPS = 128
DILATIONS = (1, 4, 16)
POOL_WINDOWS = (2, 4, 8, 16)
POOL_HALO = 16
CONV_HALO = 8
NEG = -1e30

ADAM_LR = 0.001
ADAM_B1 = 0.9
ADAM_B2 = 0.999
ADAM_EPS = 1e-08
ADAM_WD = 0.01
ADAM_STEP = 10

QKV_OFF, QKV_W = 0, 4608
XBC_OFF, XBC_W = 4608, 1536
GATE_OFF, GATE_W = 6144, 3072
Z_OFF, Z_W = 9216, 1024
U_OFF, U_W = 10240, 1024
DT_OFF, DT_W = 11264, 256
NP = 11520
IN_WIDTH = 11280
O_Z, O_XBC, O_DT, O_Q, O_K, O_V, O_U, O_G = 0, 1024, 2560, 2576, 4112, 5648, 7184, 8208


def _alibi_slopes(n):
    def pow2(k):
        start = 2.0 ** (-8.0 / k)
        return [start ** (i + 1) for i in range(k)]
    if math.log2(n).is_integer():
        s = pow2(n)
    else:
        c = 2 ** math.floor(math.log2(n))
        s = pow2(c) + pow2(2 * c)[0::2][: n - c]
    return np.sort(np.asarray(s, np.float32))[::-1].copy()


SLOPES = _alibi_slopes(ATTN_HEADS).reshape(3, 4)


def _cp(sem=None):
    return pltpu.CompilerParams(dimension_semantics=sem, vmem_limit_bytes=V7X_VMEM_LIMIT_BYTES)


def _pick(dim, cap):
    if dim <= cap:
        return dim
    t = cap - cap % LANES
    while t >= LANES:
        if dim % t == 0:
            return t
        t -= LANES
    raise ValueError((dim, cap))


def _sigmoid(x):
    return 1.0 / (1.0 + jnp.exp(-x))


def _silu(x):
    return x * _sigmoid(x)


def _dsilu(x):
    s = _sigmoid(x)
    return s * (1.0 + x * (1.0 - s))


def _softplus(x):
    return jnp.maximum(x, 0.0) + jnp.log(1.0 + jnp.exp(-jnp.abs(x)))


def _dot(a, b, dims):
    return lax.dot_general(a, b, (dims, ((), ())), preferred_element_type=F32)


def _nn(a, b):
    return _dot(a, b, ((1,), (0,)))


def _nt(a, b):
    return _dot(a, b, ((1,), (1,)))


def _tn(a, b):
    return _dot(a, b, ((0,), (0,)))


def _iota(shape, axis):
    return lax.broadcasted_iota(jnp.int32, shape, axis)


def _position():
    return lax.axis_index("x"), lax.axis_index("y"), lax.axis_index("c")


def _index(px, py, pc):
    return 4 * px + 2 * py + pc


def _remote(src, dst, send_sems, recv_sems, a, k, to):
    return pltpu.make_async_remote_copy(src_ref=src, dst_ref=dst, send_sem=send_sems.at[a, k], recv_sem=recv_sems.at[a, k],
                                        device_id=to, device_id_type=MESH)


def _gather_plan(srcs, dsts, send_sems, recv_sems, local_sems):
    x, y, c = _position()
    me, sib = (x, y, c), (x, y, 1 - c)
    chips = [(1 - x, y), (x, 1 - y), (1 - x, 1 - y)]
    n = len(srcs)

    def start():
        for a in range(n):
            mine = dsts[a].at[_index(*me)]
            pltpu.make_async_copy(srcs[a], mine, local_sems.at[a]).start()
            _remote(srcs[a], mine, send_sems, recv_sems, a, 0, sib).start()
            for j, chip in enumerate(chips):
                _remote(srcs[a], mine, send_sems, recv_sems, a, 1 + j, (*chip, c)).start()

    def finish():
        for a in range(n):
            for j, chip in enumerate(chips):
                blk = dsts[a].at[_index(*chip, c)]
                _remote(blk, blk, send_sems, recv_sems, a, 1 + j, me).wait_recv()
                _remote(blk, blk, send_sems, recv_sems, a, 4 + j, sib).start()
        for a in range(n):
            mine = dsts[a].at[_index(*me)]
            blk = dsts[a].at[_index(*sib)]
            _remote(blk, blk, send_sems, recv_sems, a, 0, me).wait_recv()
            for j, chip in enumerate(chips):
                blk = dsts[a].at[_index(*chip, 1 - c)]
                _remote(blk, blk, send_sems, recv_sems, a, 4 + j, me).wait_recv()
            for k in range(N_DEV - 1):
                _remote(mine, mine, send_sems, recv_sems, a, k, me).wait_send()
            pltpu.make_async_copy(srcs[a], mine, local_sems.at[a]).wait()

    return start, finish


def _scatter_plan(srcs, dsts, send_sems, recv_sems, local_sems):
    x, y, c = _position()
    me = _index(x, y, c)
    n = len(srcs)
    copies = []
    for a in range(n):
        copies.append(pltpu.make_async_copy(srcs[a].at[me], dsts[a].at[me], local_sems.at[a]))
        for k in range(1, N_DEV):
            px, py, pc = x ^ ((k >> 2) & 1), y ^ ((k >> 1) & 1), c ^ (k & 1)
            copies.append(_remote(srcs[a].at[_index(px, py, pc)], dsts[a].at[me], send_sems, recv_sems, a, k - 1, (px, py, pc)))

    def start():
        for cp in copies:
            cp.start()

    def finish():
        for cp in copies:
            cp.wait()

    return start, finish


class Rider:
    def __init__(self, kind, arrs):
        self.kind = kind
        items = [a if isinstance(a, tuple) else (a, None) for a in arrs]
        self.arrs = [a for a, _ in items]
        self.lead = [i for _, i in items]

    def _shapes(self):
        return [a.shape if i is None else a.shape[1:] for a, i in zip(self.arrs, self.lead)]

    def out_shapes(self):
        if self.kind == "gather":
            return [pltpu.HBM((N_DEV,) + s, a.dtype) for s, a in zip(self._shapes(), self.arrs)]
        return [pltpu.HBM(s, a.dtype) for s, a in zip(self._shapes(), self.arrs)]

    def sems(self):
        n = len(self.arrs)
        return [pltpu.SemaphoreType.DMA((n, N_DEV - 1)), pltpu.SemaphoreType.DMA((n, N_DEV - 1)),
                pltpu.SemaphoreType.DMA((n,))]

    def plan(self, srcs, dsts, sems):
        srcs = [s if i is None else s.at[i] for s, i in zip(srcs, self.lead)]
        return (_gather_plan if self.kind == "gather" else _scatter_plan)(srcs, dsts, *sems)


def _pcall(body, grid, in_specs, out_specs, out_shape, name, ins, scratch=(), aliases=None, rider=None):
    in_specs, out_specs, out_shape, scratch = list(in_specs), list(out_specs), list(out_shape), list(scratch)
    sem = ("arbitrary",) * len(grid)
    if rider is None:
        res = pl.pallas_call(body, grid=grid, in_specs=in_specs, out_specs=out_specs, out_shape=out_shape,
                             scratch_shapes=scratch, name=name, input_output_aliases=aliases or {},
                             compiler_params=_cp(sem))(*ins)
        return list(res), []
    n_in, n_out, n_sc, nr = len(in_specs), len(out_shape), len(scratch), len(rider.arrs)

    def wrapped(*refs):
        own_in, srcs = refs[:n_in], refs[n_in:n_in + nr]
        own_out, dsts = refs[n_in + nr:n_in + nr + n_out], refs[n_in + nr + n_out:n_in + 2 * nr + n_out]
        own_sc, sems = refs[n_in + 2 * nr + n_out:n_in + 2 * nr + n_out + n_sc], refs[n_in + 2 * nr + n_out + n_sc:]
        ids = [pl.program_id(ax) for ax in range(len(grid))]
        first = functools.reduce(jnp.logical_and, [i == 0 for i in ids])
        last = functools.reduce(jnp.logical_and, [i == g - 1 for i, g in zip(ids, grid)])
        start, finish = rider.plan(srcs, dsts, sems)
        pl.when(first)(start)
        body(*own_in, *own_out, *own_sc)
        pl.when(last)(finish)

    hbm = pl.BlockSpec(memory_space=pltpu.HBM)
    res = pl.pallas_call(
        wrapped, grid=grid, in_specs=in_specs + [hbm] * nr, out_specs=out_specs + [hbm] * nr,
        out_shape=out_shape + rider.out_shapes(), scratch_shapes=scratch + rider.sems(), name=name,
        input_output_aliases=aliases or {},
        compiler_params=pltpu.CompilerParams(dimension_semantics=sem, vmem_limit_bytes=V7X_VMEM_LIMIT_BYTES,
                                             has_side_effects=True))(*ins, *rider.arrs)
    return list(res[:n_out]), list(res[n_out:])


def _exchange(kind, arrs, name):
    rd = Rider(kind, arrs)
    n = len(arrs)

    def body(*refs):
        start, finish = rd.plan(refs[:n], refs[n:2 * n], refs[2 * n:])
        start()
        finish()

    hbm = pl.BlockSpec(memory_space=pltpu.HBM)
    return pl.pallas_call(
        body, in_specs=[hbm] * n, out_specs=[hbm] * n, out_shape=rd.out_shapes(), scratch_shapes=rd.sems(),
        name=name, compiler_params=pltpu.CompilerParams(has_side_effects=True))(*rd.arrs)


def all_gather(arrs, name):
    return _exchange("gather", arrs, name)


def all_to_all(arrs, name):
    return _exchange("scatter", arrs, name)


def mm(a, b, mode, out_dtype, name, rider=None, extra=(), epi=None, out_dtypes=None, col_blocks=False, tm_cap=1024,
       tn_cap=1280, tk_cap=2048):
    blk = None
    if col_blocks and mode != "tn":
        nb, rows_b, blk = b.shape
        b_shape = (rows_b, nb * blk)
    else:
        b_shape = b.shape
    if mode == "nn":
        (M, K), (K2, N) = a.shape, b_shape
    elif mode == "nt":
        (M, K), (N, K2) = a.shape, b_shape
    else:
        (K, M), (K2, N) = a.shape, b_shape
        if col_blocks:
            blk = N // N_DEV
    assert K == K2, (a.shape, b.shape, mode)
    tm, tn, tk = _pick(M, tm_cap), _pick(N, tn_cap), _pick(K, tk_cap)
    if col_blocks and mode == "nt":
        tk = blk
    elif col_blocks:
        tn = blk
    nk = K // tk
    if mode == "tn":
        a_spec = pl.BlockSpec((tk, tm), lambda i, j, k: (k, i))
    else:
        a_spec = pl.BlockSpec((tm, tk), lambda i, j, k: (i, k))
    if mode == "nt":
        b_spec = (pl.BlockSpec((None, tn, tk), lambda i, j, k: (k, j, 0)) if col_blocks
                  else pl.BlockSpec((tn, tk), lambda i, j, k: (j, k)))
    else:
        b_spec = (pl.BlockSpec((None, tk, tn), lambda i, j, k: (j, k, 0)) if col_blocks and mode == "nn"
                  else pl.BlockSpec((tk, tn), lambda i, j, k: (k, j)))
    dims = {"nn": ((1,), (0,)), "nt": ((1,), (1,)), "tn": ((0,), (0,))}[mode]
    o_spec = pl.BlockSpec((tm, tn), lambda i, j, k: (i, j))
    blocked_out = col_blocks and mode == "tn"
    if blocked_out:
        assert not extra and epi is None
        o_spec = pl.BlockSpec((None, tm, tn), lambda i, j, k: (j, i, 0))
    v_spec = pl.BlockSpec((1, tn), lambda i, j, k: (0, j))
    grid = (M // tm, N // tn, nk)
    out_dtypes = list(out_dtypes) if out_dtypes is not None else [out_dtype]
    nr = len(rider.arrs) if rider is not None else 0
    ne, no = len(extra), len(out_dtypes)
    extra_specs = []
    for e in extra:
        assert e.shape in ((M, N), (1, N)), (e.shape, M, N)
        extra_specs.append(o_spec if e.shape[0] == M and M != 1 else v_spec)

    def body(*refs):
        a_ref, b_ref = refs[0], refs[1]
        e_refs = refs[2:2 + ne]
        srcs = refs[2 + ne:2 + ne + nr]
        o_refs = refs[2 + ne + nr:2 + ne + nr + no]
        dsts = refs[2 + ne + nr + no:2 + ne + 2 * nr + no]
        rest = refs[2 + ne + 2 * nr + no:]
        i, j, k = pl.program_id(0), pl.program_id(1), pl.program_id(2)
        if nr:
            start, finish = rider.plan(srcs, dsts, rest[-3:])

            @pl.when(jnp.logical_and(jnp.logical_and(i == 0, j == 0), k == 0))
            def _():
                start()

        def write(r):
            vals = epi(r, *[e[...] for e in e_refs]) if epi is not None else (r,)
            for o_ref, val, dt in zip(o_refs, vals, out_dtypes):
                o_ref[...] = val.astype(dt)

        prod = _dot(a_ref[...].astype(BF16), b_ref[...].astype(BF16), dims)
        if nk == 1:
            write(prod)
        else:
            acc = rest[0]

            @pl.when(k == 0)
            def _():
                acc[...] = prod

            @pl.when(k > 0)
            def _():
                acc[...] += prod

            @pl.when(k == nk - 1)
            def _():
                write(acc[...])

        if nr:
            @pl.when(jnp.logical_and(jnp.logical_and(i == grid[0] - 1, j == grid[1] - 1), k == nk - 1))
            def _():
                finish()

    hbm = pl.BlockSpec(memory_space=pltpu.HBM)
    out_shape = [jax.ShapeDtypeStruct((N_DEV, M, blk) if blocked_out else (M, N), dt) for dt in out_dtypes]
    scratch = [pltpu.VMEM((tm, tn), F32)] if nk > 1 else []
    if nr:
        res = pl.pallas_call(
            body, grid=grid, in_specs=[a_spec, b_spec] + extra_specs + [hbm] * nr, out_specs=[o_spec] * no + [hbm] * nr,
            out_shape=out_shape + rider.out_shapes(), scratch_shapes=scratch + rider.sems(), name=name,
            compiler_params=pltpu.CompilerParams(dimension_semantics=("arbitrary",) * 3,
                                                 vmem_limit_bytes=V7X_VMEM_LIMIT_BYTES, has_side_effects=True),
        )(a, b, *extra, *rider.arrs)
        outs, got = list(res[:no]), list(res[no:])
        return (outs[0] if no == 1 else outs), got
    res = pl.pallas_call(
        body, grid=grid, in_specs=[a_spec, b_spec] + extra_specs, out_specs=[o_spec] * no, out_shape=out_shape,
        scratch_shapes=scratch, name=name, compiler_params=_cp(("parallel", "parallel", "arbitrary")))(a, b, *extra)
    return res[0] if no == 1 else list(res)


def _rows(tile, w, cb=0):
    return pl.BlockSpec((tile, w), lambda i: (i, cb))


def _vec(shape):
    nd = len(shape)
    return pl.BlockSpec(shape, lambda i: (0,) * nd)


def _any():
    return pl.BlockSpec(memory_space=pl.ANY)


def _sds(shape, dtype):
    return jax.ShapeDtypeStruct(shape, dtype)


def _rowcall(body, n_rows, tile, in_specs, out_specs, out_shape, name, aliases=None):
    return pl.pallas_call(
        body, grid=(n_rows // tile,), in_specs=in_specs, out_specs=out_specs, out_shape=out_shape,
        name=name, input_output_aliases=aliases or {}, compiler_params=_cp(("arbitrary",)))


def _zero_at_first(step, *refs):
    @pl.when(step == 0)
    def _():
        for r in refs:
            r[...] = jnp.zeros_like(r)


def rms_mod_fwd(x, nw, sc, sh, name):
    S, D = x.shape
    T = ROW_TILE

    def body(x_ref, nw_ref, sc_ref, sh_ref, h_ref):
        xv = x_ref[...]
        r = lax.rsqrt(jnp.mean(xv * xv, axis=-1, keepdims=True) + EPS)
        y = xv * r * nw_ref[...]
        h_ref[...] = (y * (1.0 + sc_ref[...]) + sh_ref[...]).astype(BF16)

    return _rowcall(body, S, T, [_rows(T, D), _vec((1, D)), _vec((1, D)), _vec((1, D))], _rows(T, D),
                    _sds((S, D), BF16), name)(x, nw, sc, sh)


def rms_mod_bwd(x, dh, dres, nw, sc, name):
    S, D = x.shape
    T = ROW_TILE

    def body(x_ref, dh_ref, dres_ref, nw_ref, sc_ref, dx_ref, dnw_ref, dsc_ref, dsh_ref):
        _zero_at_first(pl.program_id(0), dnw_ref, dsc_ref, dsh_ref)
        xv = x_ref[...]
        d = dh_ref[...]
        r = lax.rsqrt(jnp.mean(xv * xv, axis=-1, keepdims=True) + EPS)
        xh = xv * r
        nwv = nw_ref[...]
        dsh_ref[...] += jnp.sum(d, axis=0, keepdims=True)
        dsc_ref[...] += jnp.sum(d * (xh * nwv), axis=0, keepdims=True)
        dy = d * (1.0 + sc_ref[...])
        dnw_ref[...] += jnp.sum(dy * xh, axis=0, keepdims=True)
        dxh = dy * nwv
        dx_ref[...] = dres_ref[...] + r * (dxh - xh * jnp.mean(dxh * xh, axis=-1, keepdims=True))

    v = _vec((1, D))
    return _rowcall(body, S, T, [_rows(T, D), _rows(T, D), _rows(T, D), v, v],
                    [_rows(T, D), v, v, v],
                    [_sds((S, D), F32), _sds((1, D), F32), _sds((1, D), F32), _sds((1, D), F32)], name)(x, dh, dres, nw, sc)


def _shift_down(tile, halo, k):
    if k == 0:
        return tile
    hr = halo.shape[0]
    rolled = pltpu.roll(tile, k, 0)
    rh = pltpu.roll(halo, k, 0)
    rid = _iota(halo.shape, 0)
    first = jnp.where(rid < k, rh, rolled[:hr])
    return jnp.concatenate([first, rolled[hr:]], axis=0)


def _shift_up(tile, nxt, k):
    if k == 0:
        return tile
    T = tile.shape[0]
    hr = nxt.shape[0]
    rolled = pltpu.roll(tile, T - k, 0)
    rn = pltpu.roll(nxt, hr - k, 0)
    rid = _iota(nxt.shape, 0)
    last = jnp.where(rid >= hr - k, rn, rolled[T - hr:])
    return jnp.concatenate([rolled[:T - hr], last], axis=0)


CONV_CB = 512


def conv_fwd(P, conv_w, conv_b, name):
    S = P.shape[0]
    T = ROW_TILE
    nj = XBC_W // CONV_CB
    cb0 = XBC_OFF // CONV_CB
    hb = T // CONV_HALO

    def body(x_ref, h_ref, w_ref, b_ref, pre_ref, xc_ref):
        i = pl.program_id(0)
        xb = x_ref[...]
        xh = jnp.where(i > 0, h_ref[...], 0.0)
        acc = b_ref[...] + w_ref[3:4, :] * xb
        for idx in range(3):
            acc = acc + w_ref[idx:idx + 1, :] * _shift_down(xb, xh, 3 - idx)
        pre_ref[...] = acc
        xc_ref[...] = _silu(acc)

    blk = pl.BlockSpec((T, CONV_CB), lambda i, j: (i, j))
    return pl.pallas_call(
        body, grid=(S // T, nj),
        in_specs=[pl.BlockSpec((T, CONV_CB), lambda i, j: (i, cb0 + j)),
                  pl.BlockSpec((CONV_HALO, CONV_CB), lambda i, j: (jnp.maximum(i * hb - 1, 0), cb0 + j)),
                  pl.BlockSpec((4, CONV_CB), lambda i, j: (0, j)),
                  pl.BlockSpec((1, CONV_CB), lambda i, j: (0, j))],
        out_specs=[blk, blk], out_shape=[_sds((S, XBC_W), F32), _sds((S, XBC_W), F32)], name=name,
        compiler_params=_cp(("arbitrary", "arbitrary")))(P, P, conv_w, conv_b)


def conv_bwd(dxc, pre, P, dP, conv_w, name):
    S = P.shape[0]
    T = ROW_TILE
    nt = S // T
    nj = XBC_W // CONV_CB
    cb0 = XBC_OFF // CONV_CB
    hb = T // CONV_HALO

    def body(d_ref, dn_ref, p_ref, pn_ref, x_ref, xh_ref, w_ref, dp_in, dp_ref, dw_ref, db_ref):
        i = pl.program_id(1)
        _zero_at_first(i, dw_ref, db_ref)
        dpre = d_ref[...] * _dsilu(p_ref[...])
        dpn = jnp.where(i < nt - 1, dn_ref[...] * _dsilu(pn_ref[...]), 0.0)
        db_ref[...] += jnp.sum(dpre, axis=0, keepdims=True)
        xb = x_ref[...]
        xh = jnp.where(i > 0, xh_ref[...], 0.0)
        dx = jnp.zeros_like(dpre)
        for idx in range(4):
            k = 3 - idx
            dw_ref[idx:idx + 1, :] += jnp.sum(dpre * _shift_down(xb, xh, k), axis=0, keepdims=True)
            dx = dx + w_ref[idx:idx + 1, :] * _shift_up(dpre, dpn, k)
        dp_ref[...] = dx.astype(BF16)

    cur = pl.BlockSpec((T, CONV_CB), lambda j, i: (i, j))
    nxt = pl.BlockSpec((CONV_HALO, CONV_CB), lambda j, i: (jnp.minimum((i + 1) * hb, S // CONV_HALO - 1), j))
    return pl.pallas_call(
        body, grid=(nj, nt),
        in_specs=[cur, nxt, cur, nxt,
                  pl.BlockSpec((T, CONV_CB), lambda j, i: (i, cb0 + j)),
                  pl.BlockSpec((CONV_HALO, CONV_CB), lambda j, i: (jnp.maximum(i * hb - 1, 0), cb0 + j)),
                  pl.BlockSpec((4, CONV_CB), lambda j, i: (0, j)), _any()],
        out_specs=[pl.BlockSpec((T, CONV_CB), lambda j, i: (i, cb0 + j)),
                   pl.BlockSpec((4, CONV_CB), lambda j, i: (0, j)),
                   pl.BlockSpec((1, CONV_CB), lambda j, i: (0, j))],
        out_shape=[_sds(dP.shape, BF16), _sds((4, XBC_W), F32), _sds((1, XBC_W), F32)],
        input_output_aliases={7: 0}, name=name,
        compiler_params=_cp(("arbitrary", "arbitrary")))(dxc, dxc, pre, pre, P, P, conv_w, dP)


def _col(v, h, lane):
    return jnp.sum(jnp.where(lane == h, v, 0.0), axis=1, keepdims=True)


def _row(v, h, rowi):
    return jnp.sum(jnp.where(rowi == h, v, 0.0), axis=0, keepdims=True)


def _ssd_common(dt_ref, dtb_ref, al_ref):
    L = SSD_CHUNK
    lane = _iota((L, L), 1)
    rowi = _iota((L, L), 0)
    dtr = dt_ref[...] + dtb_ref[...]
    dt = _softplus(dtr)
    A = -jnp.exp(al_ref[...])
    a = dt * A
    tril = (rowi >= lane).astype(F32)
    ac = jnp.dot(tril, a, precision=HIGHEST, preferred_element_type=F32)
    acT = ac.T
    last = _row(ac, L - 1, rowi)
    return dict(lane=lane, rowi=rowi, dtr=dtr, dt=dt, A=A, ac=ac, acT=acT, last=last,
                eac=jnp.exp(ac), fac=jnp.exp(last - ac), cd=jnp.exp(last),
                lo=lane < SSD_HEAD_DIM, causal=rowi >= lane)


def _pair(v, h0, q):
    lane = q["lane"][:v.shape[0]]
    lo = q["lo"][:v.shape[0]]
    return jnp.where(lo, _col(v, h0, lane), _col(v, h0 + 1, lane))


def _decay(q, h):
    seg = _col(q["ac"], h, q["lane"]) - _row(q["acT"], h, q["rowi"])
    return jnp.exp(jnp.where(q["causal"], seg, NEG))


def ssd_fwd(xc, P, dtb, alog, dsk, name, rider=None):
    S = xc.shape[0]
    L = SSD_CHUNK
    nc = S // L

    def body(xs_ref, b_ref, c_ref, dt_ref, dtb_ref, al_ref, dk_ref, y_ref, st_ref, H):
        @pl.when(pl.program_id(0) == 0)
        def _():
            H[...] = jnp.zeros_like(H)

        st_ref[0] = H[...]
        q = _ssd_common(dt_ref, dtb_ref, al_ref)
        dk = dk_ref[...]
        for g in range(2):
            Bg = b_ref[:, g * L:(g + 1) * L].astype(BF16)
            Cg = c_ref[:, g * L:(g + 1) * L].astype(BF16)
            cb = _nt(Cg, Bg)
            for jj in range(4):
                j = g * 4 + jj
                h0 = 2 * j
                sl = slice(j * L, (j + 1) * L)
                xs_p = xs_ref[:, sl]
                X = xs_p * _pair(q["dt"], h0, q)
                Xb = X.astype(BF16)
                G0 = (cb * _decay(q, h0)).astype(BF16)
                G1 = (cb * _decay(q, h0 + 1)).astype(BF16)
                Yd = jnp.where(q["lo"], _nn(G0, Xb), _nn(G1, Xb))
                Hp = H[:, sl]
                Yo = _nn(Cg, Hp.astype(BF16)) * _pair(q["eac"], h0, q)
                y_ref[:, sl] = Yd + Yo + _pair(dk, h0, q) * xs_p
                Xd = (X * _pair(q["fac"], h0, q)).astype(BF16)
                H[:, sl] = _pair(q["cd"], h0, q) * Hp + _tn(Bg, Xd)

    v = _vec((1, LANES))
    (y, st), got = _pcall(
        body, (nc,),
        [pl.BlockSpec((L, 1024), lambda c: (c, 0)),
         pl.BlockSpec((L, 256), lambda c: (c, 4)),
         pl.BlockSpec((L, 256), lambda c: (c, 5)),
         pl.BlockSpec((L, LANES), lambda c: (c, DT_OFF // LANES)), v, v, v],
        [pl.BlockSpec((L, 1024), lambda c: (c, 0)),
         pl.BlockSpec((1, SSD_STATE, 1024), lambda c: (c, 0, 0))],
        [_sds((S, 1024), F32), _sds((nc, SSD_STATE, 1024), F32)], name, (xc, xc, xc, P, dtb, alog, dsk),
        scratch=[pltpu.VMEM((SSD_STATE, 1024), F32)], rider=rider)
    return y, st, got


def ssd_bwd(xc, P, st, dy, dP, dtb, alog, dsk, name, rider=None):
    S = xc.shape[0]
    L = SSD_CHUNK
    nc = S // L

    def body(xs_ref, b_ref, c_ref, dt_ref, st_ref, dy_ref, dtb_ref, al_ref, dk_ref, dp_in,
             dxc_ref, dp_ref, ddtb_ref, dal_ref, dD_ref, dH):
        step = pl.program_id(0)

        @pl.when(step == 0)
        def _():
            dH[...] = jnp.zeros_like(dH)

        _zero_at_first(step, ddtb_ref, dal_ref, dD_ref)
        q = _ssd_common(dt_ref, dtb_ref, al_ref)
        lane, rowi, lo = q["lane"], q["rowi"], q["lo"]
        lane1 = lane[:1]
        lo1 = lo[:1]
        dk = dk_ref[...]
        dac = jnp.zeros((L, L), F32)
        dacT = jnp.zeros((L, L), F32)
        dlast = jnp.zeros((1, L), F32)
        ddt = jnp.zeros((L, L), F32)

        def put(h0, s0, s1):
            return jnp.where(lane == h0, s0, 0.0) + jnp.where(lane == h0 + 1, s1, 0.0)

        def put1(h0, s0, s1):
            return jnp.where(lane1 == h0, s0, 0.0) + jnp.where(lane1 == h0 + 1, s1, 0.0)

        def hsum(v):
            return (jnp.sum(jnp.where(lo, v, 0.0), axis=1, keepdims=True),
                    jnp.sum(jnp.where(lo, 0.0, v), axis=1, keepdims=True))

        for g in range(2):
            Bg = b_ref[:, g * L:(g + 1) * L].astype(BF16)
            Cg = c_ref[:, g * L:(g + 1) * L].astype(BF16)
            cb = _nt(Cg, Bg)
            dcb = jnp.zeros((L, L), F32)
            dCg = jnp.zeros((L, L), F32)
            dBg = jnp.zeros((L, L), F32)
            for jj in range(4):
                j = g * 4 + jj
                h0 = 2 * j
                sl = slice(j * L, (j + 1) * L)
                xs_p = xs_ref[:, sl]
                dY = dy_ref[:, sl]
                dYb = dY.astype(BF16)
                dtp = _pair(q["dt"], h0, q)
                X = xs_p * dtp
                e_p = _pair(q["eac"], h0, q)
                f_p = _pair(q["fac"], h0, q)
                cd_p = _pair(q["cd"], h0, q)
                Hp = st_ref[0, :, sl]
                Hpb = Hp.astype(BF16)
                dHn = dH[:, sl]
                dHnb = dHn.astype(BF16)
                dD_ref[:, sl] += jnp.sum(dY * xs_p, axis=0, keepdims=True)
                dxs = _pair(dk, h0, q) * dY
                Yo = _nn(Cg, Hpb) * e_p
                dZb = (dY * e_p).astype(BF16)
                s0, s1 = hsum(dY * Yo)
                dac = dac + put(h0, s0, s1)
                dCg = dCg + _nt(dZb, Hpb)
                dHp = _tn(Cg, dZb) + cd_p * dHn
                t = jnp.sum(dHn * Hp, axis=0, keepdims=True)
                d0 = jnp.sum(jnp.where(lo1, t, 0.0), axis=1, keepdims=True)
                d1 = jnp.sum(jnp.where(lo1, 0.0, t), axis=1, keepdims=True)
                dlast = dlast + put1(h0, d0, d1) * q["cd"]
                Xd = X * f_p
                dXd = _nn(Bg, dHnb)
                dBg = dBg + _nt(Xd.astype(BF16), dHnb)
                dX = dXd * f_p
                s0, s1 = hsum(dXd * Xd)
                dac = dac - put(h0, s0, s1)
                dlast = dlast + put1(h0, jnp.sum(s0, axis=0, keepdims=True), jnp.sum(s1, axis=0, keepdims=True))
                for hh, mask in ((h0, lo), (h0 + 1, jnp.logical_not(lo))):
                    dec = _decay(q, hh)
                    Gm = cb * dec
                    Xh = jnp.where(mask, X, 0.0).astype(BF16)
                    dG = _nt(dYb, Xh)
                    dcb = dcb + dG * dec
                    Q = dG * Gm
                    dac = dac + jnp.where(lane == hh, jnp.sum(Q, axis=1, keepdims=True), 0.0)
                    dacT = dacT - jnp.where(rowi == hh, jnp.sum(Q, axis=0, keepdims=True), 0.0)
                    dX = dX + jnp.where(mask, _tn(Gm.astype(BF16), dYb), 0.0)
                dxs = dxs + dX * dtp
                s0, s1 = hsum(dX * xs_p)
                ddt = ddt + put(h0, s0, s1)
                dxc_ref[:, sl] = dxs
                dH[:, sl] = dHp
            dcbb = dcb.astype(BF16)
            dxc_ref[:, 1024 + g * L:1024 + (g + 1) * L] = dBg + _tn(dcbb, Cg)
            dxc_ref[:, 1280 + g * L:1280 + (g + 1) * L] = dCg + _nn(dcbb, Bg)

        dac = dac + dacT.T + jnp.where(rowi == L - 1, dlast, 0.0)
        triu = (rowi <= lane).astype(F32)
        da = jnp.dot(triu, dac, precision=HIGHEST, preferred_element_type=F32)
        ddt = ddt + da * q["A"]
        dal_ref[...] += jnp.sum(da * q["dt"], axis=0, keepdims=True) * q["A"]
        ddtr = jnp.where(lane < SSD_HEADS, ddt * _sigmoid(q["dtr"]), 0.0)
        ddtb_ref[...] += jnp.sum(ddtr, axis=0, keepdims=True)
        dp_ref[...] = jnp.concatenate([ddtr, jnp.zeros_like(ddtr)], axis=1).astype(BF16)

    v = _vec((1, LANES))
    rev = lambda c: nc - 1 - c
    res, got = _pcall(
        body, (nc,),
        [pl.BlockSpec((L, 1024), lambda c: (rev(c), 0)),
         pl.BlockSpec((L, 256), lambda c: (rev(c), 4)),
         pl.BlockSpec((L, 256), lambda c: (rev(c), 5)),
         pl.BlockSpec((L, LANES), lambda c: (rev(c), DT_OFF // LANES)),
         pl.BlockSpec((1, SSD_STATE, 1024), lambda c: (rev(c), 0, 0)),
         pl.BlockSpec((L, 1024), lambda c: (rev(c), 0)), v, v, v, _any()],
        [pl.BlockSpec((L, XBC_W), lambda c: (rev(c), 0)),
         pl.BlockSpec((L, DT_W), lambda c: (rev(c), DT_OFF // DT_W)),
         v, v, _vec((1, 1024))],
        [_sds((S, XBC_W), F32), _sds(dP.shape, BF16), _sds((1, LANES), F32), _sds((1, LANES), F32),
         _sds((1, 1024), F32)], name, (xc, xc, xc, P, st, dy, dtb, alog, dsk, dP),
        scratch=[pltpu.VMEM((SSD_STATE, 1024), F32)], aliases={9: 1}, rider=rider)
    return (*res, got)


def ssd_post_fwd(y, P, nw, name):
    S = y.shape[0]
    T = ROW_TILE
    HW = 512

    def body(y_ref, z_ref, w_ref, o_ref):
        g = y_ref[...] * _silu(z_ref[...])
        for k in range(2):
            gk = g[:, k * HW:(k + 1) * HW]
            r = lax.rsqrt(jnp.mean(gk * gk, axis=-1, keepdims=True) + EPS)
            o_ref[:, k * HW:(k + 1) * HW] = (gk * r * w_ref[:, k * HW:(k + 1) * HW]).astype(BF16)

    return _rowcall(body, S, T, [_rows(T, 1024), _rows(T, 1024, Z_OFF // 1024), _vec((1, 1024))], _rows(T, 1024),
                    _sds((S, 1024), BF16), name)(y, P, nw)


def ssd_post_bwd(y, P, dout, dP, nw, name):
    S = y.shape[0]
    T = ROW_TILE
    HW = 512

    def body(y_ref, z_ref, d_ref, w_ref, dp_in, dy_ref, dz_ref, dw_ref):
        _zero_at_first(pl.program_id(0), dw_ref)
        yv = y_ref[...]
        zv = z_ref[...]
        sz = _silu(zv)
        g = yv * sz
        dv = d_ref[...]
        dgs = []
        for k in range(2):
            sl = slice(k * HW, (k + 1) * HW)
            gk = g[:, sl]
            r = lax.rsqrt(jnp.mean(gk * gk, axis=-1, keepdims=True) + EPS)
            gh = gk * r
            dk = dv[:, sl]
            dw_ref[:, sl] += jnp.sum(dk * gh, axis=0, keepdims=True)
            dgn = dk * w_ref[:, sl]
            dgs.append(r * (dgn - gh * jnp.mean(dgn * gh, axis=-1, keepdims=True)))
        dg = jnp.concatenate(dgs, axis=1)
        dy_ref[...] = dg * sz
        dz_ref[...] = (dg * yv * _dsilu(zv)).astype(BF16)

    zc = Z_OFF // 1024
    return _rowcall(body, S, T, [_rows(T, 1024), _rows(T, 1024, zc), _rows(T, 1024), _vec((1, 1024)), _any()],
                    [_rows(T, 1024), _rows(T, 1024, zc), _vec((1, 1024))],
                    [_sds((S, 1024), F32), _sds(dP.shape, BF16), _sds((1, 1024), F32)], name,
                    aliases={4: 1})(y, P, dout, nw, dP)


HEAD_W = 3 * ATTN_HEAD_DIM
HEAD_BLOCKS = NP // HEAD_W


def _slope(gi, h):
    c = [float(v) for v in SLOPES[gi]]
    return jnp.where(h == 0, c[0], jnp.where(h == 1, c[1], jnp.where(h == 2, c[2], c[3])))


def _rmsn(t):
    r = lax.rsqrt(jnp.mean(t * t, axis=-1, keepdims=True) + EPS)
    return t * r, r


def _rms_bwd(th, r, w, dn):
    dth = dn * w
    return r * (dth - th * jnp.mean(dth * th, axis=-1, keepdims=True))


ATT_ROWS = 2048


def _strided(r0, d):
    return pl.ds(r0, ATTN_STEPS, stride=d) if d > 1 else pl.ds(r0, ATTN_STEPS)


def attn_fwd(P, qw, kw, gi, name, rider=None):
    S = P.shape[0]
    d = DILATIONS[gi]
    B = ATTN_STEPS
    E = ATTN_HEAD_DIM
    BR = B * d
    nq = ATT_ROWS // BR
    nsb = S // ATT_ROWS
    scale = E ** -0.5

    def body(q_ref, k_ref, v_ref, kp_ref, vp_ref, qw_ref, kw_ref, o_ref, l_ref):
        h = pl.program_id(0)
        s = pl.program_id(1)
        slope = _slope(gi, h) * float(d)
        qwv = qw_ref[...]
        kwv = kw_ref[...]
        qi = _iota((B, B), 0)
        kj = _iota((B, B), 1)
        rel_c = (qi - kj).astype(F32)
        rel_p = rel_c + float(B)
        bias_c = jnp.where(qi >= kj, -slope * rel_c, NEG)
        bias_p0 = jnp.where(jnp.logical_and(qi <= kj, s > 0), -slope * rel_p, NEG)
        bias_p = jnp.where(qi <= kj, -slope * rel_p, NEG)
        for j in range(nq):
            for r in range(d):
                rows = _strided(j * BR + r, d)
                if j == 0:
                    kpv, vpv, bp = kp_ref[_strided(r, d), :], vp_ref[_strided(r, d), :], bias_p0
                else:
                    prow = _strided((j - 1) * BR + r, d)
                    kpv, vpv, bp = k_ref[prow, :], v_ref[prow, :], bias_p
                qh, _ = _rmsn(q_ref[rows, :])
                kh, _ = _rmsn(k_ref[rows, :])
                kph, _ = _rmsn(kpv)
                qn = (qh * qwv).astype(BF16)
                kn = (kh * kwv).astype(BF16)
                kpn = (kph * kwv).astype(BF16)
                s_c = _nt(qn, kn) * scale + bias_c
                s_p = _nt(qn, kpn) * scale + bp
                m = jnp.maximum(jnp.max(s_c, axis=1, keepdims=True), jnp.max(s_p, axis=1, keepdims=True))
                e_c = jnp.exp(s_c - m)
                e_p = jnp.exp(s_p - m)
                l = jnp.sum(e_c, axis=1, keepdims=True) + jnp.sum(e_p, axis=1, keepdims=True)
                lse = m + jnp.log(l)
                inv = 1.0 / l
                p_c = (e_c * inv).astype(BF16)
                p_p = (e_p * inv).astype(BF16)
                o_ref[rows, :] = _nn(p_c, v_ref[rows, :].astype(BF16)) + _nn(p_p, vpv.astype(BF16))
                l_ref[rows, :] = jnp.broadcast_to(lse, (B, E))

    def cur(i):
        return pl.BlockSpec((ATT_ROWS, E), lambda h, s: (s, (gi * 4 + h) * 3 + i))

    def prev(i):
        return pl.BlockSpec((BR, E), lambda h, s: (jnp.maximum(s * nq - 1, 0), (gi * 4 + h) * 3 + i))

    wv = pl.BlockSpec((1, E), lambda h, s: (0, 0))
    ob = pl.BlockSpec((ATT_ROWS, E), lambda h, s: (s, h))
    (o, l), got = _pcall(body, (4, nsb), [cur(0), cur(1), cur(2), prev(1), prev(2), wv, wv], [ob, ob],
                         [_sds((S, 512), F32), _sds((S, 512), F32)], name, (P, P, P, P, P, qw, kw), rider=rider)
    return o, l, got


def attn_bwd(P, o, lse, do, dlse, dP, qw, kw, gi, name):
    S = P.shape[0]
    d = DILATIONS[gi]
    B = ATTN_STEPS
    E = ATTN_HEAD_DIM
    BR = B * d
    nq = ATT_ROWS // BR
    nsb = S // ATT_ROWS
    nblk = S // BR
    scale = E ** -0.5

    def body(q_ref, k_ref, v_ref, kp_ref, vp_ref, qx_ref, o_ref, l_ref, do_ref, dl_ref, ox_ref, lx_ref, dox_ref, dlx_ref,
             qw_ref, kw_ref, dp_in, dp_ref, dqw_ref, dkw_ref, stage_q, stage_k, stage_v):
        h = pl.program_id(0)
        s = pl.program_id(1)

        @pl.when(jnp.logical_and(h == 0, s == 0))
        def _():
            dqw_ref[...] = jnp.zeros_like(dqw_ref)
            dkw_ref[...] = jnp.zeros_like(dkw_ref)

        slope = _slope(gi, h) * float(d)
        qwv = qw_ref[...]
        kwv = kw_ref[...]
        qi = _iota((B, B), 0)
        kj = _iota((B, B), 1)
        rel_c = (qi - kj).astype(F32)
        rel_p = rel_c + float(B)
        band = qi <= kj
        bias_c = jnp.where(qi >= kj, -slope * rel_c, NEG)
        bias_p = jnp.where(band, -slope * rel_p, NEG)
        bias_p_first = jnp.where(jnp.logical_and(band, s > 0), -slope * rel_p, NEG)
        bias_p_last = jnp.where(jnp.logical_and(band, s < nsb - 1), -slope * rel_p, NEG)
        dqw = jnp.zeros((1, E), F32)
        dkw = jnp.zeros((1, E), F32)
        for j in range(nq):
            for r in range(d):
                rows = _strided(j * BR + r, d)
                if j == 0:
                    kpv, vpv, bB = kp_ref[_strided(r, d), :], vp_ref[_strided(r, d), :], bias_p_first
                else:
                    prow = _strided((j - 1) * BR + r, d)
                    kpv, vpv, bB = k_ref[prow, :], v_ref[prow, :], bias_p
                if j == nq - 1:
                    xr = _strided(r, d)
                    qxv, oxv, lxv, doxv, dlxv, bC = (qx_ref[xr, :], ox_ref[xr, :], lx_ref[xr, :], dox_ref[xr, :],
                                                     dlx_ref[xr, :], bias_p_last)
                else:
                    xr = _strided((j + 1) * BR + r, d)
                    qxv, oxv, lxv, doxv, dlxv, bC = (q_ref[xr, :], o_ref[xr, :], l_ref[xr, :], do_ref[xr, :],
                                                     dl_ref[xr, :], bias_p)
                qh, rq = _rmsn(q_ref[rows, :])
                kh, rk = _rmsn(k_ref[rows, :])
                kph, _ = _rmsn(kpv)
                qxh, _ = _rmsn(qxv)
                qn = (qh * qwv).astype(BF16)
                kn = (kh * kwv).astype(BF16)
                kpn = (kph * kwv).astype(BF16)
                qxn = (qxh * qwv).astype(BF16)
                v = v_ref[rows, :].astype(BF16)
                vp = vpv.astype(BF16)
                lse_c = l_ref[rows, :]
                pA = jnp.exp(_nt(qn, kn) * scale + bias_c - lse_c)
                pB = jnp.exp(_nt(qn, kpn) * scale + bB - lse_c)
                pC = jnp.exp(_nt(qxn, kn) * scale + bC - lxv)
                doc = do_ref[rows, :]
                docb = doc.astype(BF16)
                doxb = doxv.astype(BF16)
                corr_c = jnp.sum(dl_ref[rows, :] - doc * o_ref[rows, :], axis=1, keepdims=True)
                corr_n = jnp.sum(dlxv - doxv * oxv, axis=1, keepdims=True)
                dsA = (pA * (_nt(docb, v) + corr_c)).astype(BF16)
                dsB = (pB * (_nt(docb, vp) + corr_c)).astype(BF16)
                dsC = (pC * (_nt(doxb, v) + corr_n)).astype(BF16)
                dqn = (_nn(dsA, kn) + _nn(dsB, kpn)) * scale
                dkn = (_tn(dsA, qn) + _tn(dsC, qxn)) * scale
                dv = _tn(pA.astype(BF16), docb) + _tn(pC.astype(BF16), doxb)
                dqw = dqw + jnp.sum(dqn * qh, axis=0, keepdims=True)
                dkw = dkw + jnp.sum(dkn * kh, axis=0, keepdims=True)
                stage_q[rows, :] = _rms_bwd(qh, rq, qwv, dqn)
                stage_k[rows, :] = _rms_bwd(kh, rk, kwv, dkn)
                stage_v[rows, :] = dv
        dqw_ref[...] += dqw
        dkw_ref[...] += dkw
        dp_ref[:, 0:E] = stage_q[...].astype(BF16)
        dp_ref[:, E:2 * E] = stage_k[...].astype(BF16)
        dp_ref[:, 2 * E:3 * E] = stage_v[...].astype(BF16)

    def cur(i):
        return pl.BlockSpec((ATT_ROWS, E), lambda h, s: (s, (gi * 4 + h) * 3 + i))

    def prev(i):
        return pl.BlockSpec((BR, E), lambda h, s: (jnp.maximum(s * nq - 1, 0), (gi * 4 + h) * 3 + i))

    nxt_q = pl.BlockSpec((BR, E), lambda h, s: (jnp.minimum((s + 1) * nq, nblk - 1), (gi * 4 + h) * 3))
    ocur = pl.BlockSpec((ATT_ROWS, E), lambda h, s: (s, h))
    onxt = pl.BlockSpec((BR, E), lambda h, s: (jnp.minimum((s + 1) * nq, nblk - 1), h))
    wv = pl.BlockSpec((1, E), lambda h, s: (0, 0))
    return pl.pallas_call(
        body, grid=(4, nsb),
        in_specs=[cur(0), cur(1), cur(2), prev(1), prev(2), nxt_q, ocur, ocur, ocur, ocur, onxt, onxt, onxt, onxt, wv, wv,
                  _any()],
        out_specs=[pl.BlockSpec((ATT_ROWS, HEAD_W), lambda h, s: (s, gi * 4 + h)), wv, wv],
        out_shape=[_sds(dP.shape, BF16), _sds((1, E), F32), _sds((1, E), F32)],
        scratch_shapes=[pltpu.VMEM((ATT_ROWS, E), F32)] * 3,
        input_output_aliases={16: 0}, name=name,
        compiler_params=_cp(("arbitrary", "arbitrary")))(
            P, P, P, P, P, P, o, lse, do, dlse, o, lse, do, dlse, qw, kw, dP)


def attn_merge_fwd(os_, ls_, name):
    S = os_[0].shape[0]
    T = ROW_TILE

    def body(o0, o1, o2, l0, l1, l2, out):
        a, b, c = l0[...], l1[...], l2[...]
        m = jnp.maximum(jnp.maximum(a, b), c)
        ea, eb, ec = jnp.exp(a - m), jnp.exp(b - m), jnp.exp(c - m)
        inv = 1.0 / (ea + eb + ec)
        out[...] = ((ea * o0[...] + eb * o1[...] + ec * o2[...]) * inv).astype(BF16)

    return _rowcall(body, S, T, [_rows(T, 512)] * 6, _rows(T, 512), _sds((S, 512), BF16), name)(*os_, *ls_)


def attn_merge_bwd(os_, ls_, dom, name):
    S = os_[0].shape[0]
    T = ROW_TILE

    def body(o0, o1, o2, l0, l1, l2, d_ref, do0, do1, do2, dl0, dl1, dl2):
        a, b, c = l0[...], l1[...], l2[...]
        m = jnp.maximum(jnp.maximum(a, b), c)
        ea, eb, ec = jnp.exp(a - m), jnp.exp(b - m), jnp.exp(c - m)
        inv = 1.0 / (ea + eb + ec)
        wa, wb, wc = ea * inv, eb * inv, ec * inv
        dv = d_ref[...]
        do0[...] = wa * dv
        do1[...] = wb * dv
        do2[...] = wc * dv
        ga, gb, gc = dv * o0[...], dv * o1[...], dv * o2[...]
        tot = wa * ga + wb * gb + wc * gc
        dl0[...] = wa * (ga - tot)
        dl1[...] = wb * (gb - tot)
        dl2[...] = wc * (gc - tot)

    sp = _rows(T, 512)
    return _rowcall(body, S, T, [sp] * 7, [sp] * 6, [_sds((S, 512), F32)] * 6, name)(*os_, *ls_, dom)


def _counts(i, T, rows, w, offset=0):
    t = i * T + offset + _iota((rows, 1), 0)
    return jnp.minimum(t + 1, w).astype(F32)


def pool_fwd(P, wmix, scale, name):
    S = P.shape[0]
    T = ROW_TILE
    GW = 256
    uc = U_OFF // 1024
    hb = T // POOL_HALO

    def body(u_ref, h_ref, wm_ref, sc_ref, pl_ref, yp_ref):
        i = pl.program_id(0)
        u = u_ref[...]
        ext = jnp.concatenate([jnp.where(i > 0, h_ref[...], 0.0), u], axis=0)
        for g, w in enumerate(POOL_WINDOWS):
            sl = slice(g * GW, (g + 1) * GW)
            s = ext[:, sl]
            k = 1
            while k < w:
                s = s + pltpu.roll(s, k, 0)
                k *= 2
            pooled = (s[POOL_HALO:] / _counts(i, T, T, w) - u[:, sl]).astype(BF16)
            pl_ref[:, sl] = pooled
            yp_ref[:, sl] = (_nn(pooled, wm_ref[g]) * sc_ref[:, sl]).astype(BF16)

    return _rowcall(
        body, S, T,
        [_rows(T, 1024, uc), pl.BlockSpec((POOL_HALO, 1024), lambda i: (jnp.maximum(i * hb - 1, 0), uc)),
         _vec((4, GW, GW)), _vec((1, 1024))],
        [_rows(T, 1024), _rows(T, 1024)], [_sds((S, 1024), BF16), _sds((S, 1024), BF16)], name)(P, P, wmix, scale)


def pool_bwd(dyp, pooled, dP, wmix, scale, name):
    S = dyp.shape[0]
    T = ROW_TILE
    nt = S // T
    GW = 256
    uc = U_OFF // 1024
    hb = T // POOL_HALO
    TE = T + POOL_HALO

    def body(d_ref, dn_ref, p_ref, wm_ref, sc_ref, dp_in, du_ref, dwm_ref, dsc_ref):
        i = pl.program_id(0)
        _zero_at_first(i, dwm_ref, dsc_ref)
        dv = d_ref[...]
        dn = jnp.where(i < nt - 1, dn_ref[...], 0.0)
        for g, w in enumerate(POOL_WINDOWS):
            sl = slice(g * GW, (g + 1) * GW)
            pg = p_ref[:, sl]
            wm = wm_ref[g]
            dsc_ref[:, sl] += jnp.sum(dv[:, sl] * _nn(pg, wm), axis=0, keepdims=True)
            dmb = (dv[:, sl] * sc_ref[:, sl]).astype(BF16)
            dmnb = (dn[:, sl] * sc_ref[:, sl]).astype(BF16)
            dwm_ref[g] += _tn(pg, dmb)
            dpl = _nt(dmb, wm)
            dpln = _nt(dmnb, wm)
            s = jnp.concatenate([dpl / _counts(i, T, T, w), dpln / _counts(i, T, POOL_HALO, w, T)], axis=0)
            k = 1
            while k < w:
                s = s + pltpu.roll(s, TE - k, 0)
                k *= 2
            du_ref[:, sl] = (s[:T] - dpl).astype(BF16)

    return _rowcall(
        body, S, T,
        [_rows(T, 1024), pl.BlockSpec((POOL_HALO, 1024), lambda i: (jnp.minimum((i + 1) * hb, S // POOL_HALO - 1), 0)),
         _rows(T, 1024), _vec((4, GW, GW)), _vec((1, 1024)), _any()],
        [_rows(T, 1024, uc), _vec((4, GW, GW)), _vec((1, 1024))],
        [_sds(dP.shape, BF16), _sds((4, GW, GW), F32), _sds((1, 1024), F32)], name,
        aliases={5: 0})(dyp, dyp, pooled, wmix, scale, dP)


def gate_fwd(P, ys, ya, yp, name):
    S = ys.shape[0]
    T = ROW_TILE
    gc = GATE_OFF // 1024

    def body(g0, g1, g2, a, b, c, out):
        out[...] = (_sigmoid(g0[...]) * a[...] + _sigmoid(g1[...]) * b[...] + _sigmoid(g2[...]) * c[...]).astype(BF16)

    r = _rows(T, 1024)
    return _rowcall(body, S, T, [_rows(T, 1024, gc), _rows(T, 1024, gc + 1), _rows(T, 1024, gc + 2), r, r, r], r,
                    _sds((S, 1024), BF16), name)(P, P, P, ys, ya, yp)


def gate_bwd(P, ys, ya, yp, dm, dP, name):
    S = ys.shape[0]
    T = ROW_TILE
    gc = GATE_OFF // 1024

    def body(g0, g1, g2, a, b, c, d_ref, dp_in, dg_ref, da, db, dc):
        dv = d_ref[...]
        outs = []
        for gr, yr, dyr in ((g0, a, da), (g1, b, db), (g2, c, dc)):
            s = _sigmoid(gr[...])
            dyr[...] = (s * dv).astype(BF16)
            outs.append(dv * yr[...] * s * (1.0 - s))
        dg_ref[...] = jnp.concatenate(outs, axis=1).astype(BF16)

    r = _rows(T, 1024)
    return _rowcall(body, S, T, [_rows(T, 1024, gc), _rows(T, 1024, gc + 1), _rows(T, 1024, gc + 2), r, r, r, r, _any()],
                    [_rows(T, GATE_W, GATE_OFF // GATE_W), r, r, r],
                    [_sds(dP.shape, BF16)] + [_sds((S, 1024), BF16)] * 3, name,
                    aliases={7: 0})(P, P, P, ys, ya, yp, dm, dP)


def resid_bwd(dx, g, y, name):
    S, D = dx.shape
    T = ROW_TILE

    def body(d_ref, g_ref, y_ref, dy_ref, dg_ref):
        _zero_at_first(pl.program_id(0), dg_ref)
        dv = d_ref[...]
        dy_ref[...] = (g_ref[...] * dv).astype(BF16)
        dg_ref[...] += jnp.sum(dv * y_ref[...], axis=0, keepdims=True)

    r = _rows(T, D)
    return _rowcall(body, S, T, [r, _vec((1, D)), r], [r, _vec((1, D))], [_sds((S, D), BF16), _sds((1, D), F32)],
                    name)(dx, g, y)


def loss_fwd_bwd(y, target, name):
    S, D = y.shape
    T = ROW_TILE

    def body(y_ref, t_ref, acc_ref, dy_ref):
        _zero_at_first(pl.program_id(0), acc_ref)
        e = y_ref[...] - t_ref[...]
        acc_ref[...] += jnp.sum(e * e, axis=0, keepdims=True)
        dy_ref[...] = e * (1.0 / D)

    r = _rows(T, D)
    return _rowcall(body, S, T, [r, r], [_vec((1, D)), r], [_sds((1, D), F32), _sds((S, D), F32)], name)(y, target)


def silu_rows(c, name):
    def body(c_ref, o_ref):
        o_ref[...] = _silu(c_ref[...])

    return pl.pallas_call(body, out_shape=_sds(c.shape, F32), name=name)(c)


def adamw(parts, w, m, v, name, row0=0, prev=None):
    n, R, C = parts.shape
    Rt = w.shape[0]
    tr = R
    while tr * C * 4 > (1 << 20) and tr % 16 == 0:
        tr //= 2
    b0 = row0 // tr
    c1 = 1.0 / (1.0 - ADAM_B1 ** ADAM_STEP)
    c2 = 1.0 / (1.0 - ADAM_B2 ** ADAM_STEP)

    def body(p_ref, w_ref, m_ref, v_ref, *rest):
        g_ref, d_ref, nm_ref, nv_ref = rest[-4:]
        g = p_ref[0].astype(F32)
        for k in range(1, n):
            g = g + p_ref[k].astype(F32)
        nm = ADAM_B1 * m_ref[...] + (1.0 - ADAM_B1) * g
        nv = ADAM_B2 * v_ref[...] + (1.0 - ADAM_B2) * (g * g)
        g_ref[...] = g
        nm_ref[...] = nm
        nv_ref[...] = nv
        d_ref[...] = -ADAM_LR * ((nm * c1) / (jnp.sqrt(nv * c2) + ADAM_EPS) + ADAM_WD * w_ref[...])

    r = pl.BlockSpec((tr, C), lambda i: (b0 + i, 0))
    ins = [parts, w, m, v] + (list(prev) if prev is not None else [])
    in_specs = [pl.BlockSpec((n, tr, C), lambda i: (0, i, 0)), r, r, r] + ([_any()] * 4 if prev is not None else [])
    aliases = {4 + k: k for k in range(4)} if prev is not None else None
    return _rowcall(body, R, tr, in_specs, [r, r, r, r], [_sds((Rt, C), F32)] * 4, name, aliases=aliases)(*ins)


FWD_HOSTS = {"in_proj": ("w_in",), "attn_fwd0": ("w_ff1",), "attn_fwd1": ("w_ff2",),
             "attn_fwd2": ("w_ssd_out", "w_attn_out", "w_pool_mix", "w_pool_out", "w_out")}
W_IN_PARTS = 4
BWD_HOSTS = {"ssd_bwd": ("w_in_0", "w_in_1", "w_in_2"), "d_h2": ("w_in_3",),
             "d_w_in": ("w_ff1", "w_out", "w_ssd_out"), "d_h1": ("w_ff2", "w_attn_out", "w_pool_mix", "w_pool_out")}


def _cargo_rider(kind, hosts, host, cargo):
    if not cargo or hosts[host][0] not in cargo:
        return None
    return Rider(kind, [cargo[k] for k in hosts[host]])


def _ride(mmargs, kind, hosts, host, cargo, got, **kw):
    rider = _cargo_rider(kind, hosts, host, cargo) if host in hosts else None
    if rider is None:
        return mm(*mmargs, **kw)
    out, res = mm(*mmargs, rider=rider, **kw)
    got.update(zip(hosts[host], res))
    return out


def _epi_relu2(r):
    p = jnp.maximum(r, 0.0)
    return r, p * p


def _epi_relu2_bwd(r, f):
    return (2.0 * jnp.maximum(f, 0.0) * r,)


def _epi_resid(r, x, g):
    return r, x + g * r


def layer_fwd(x, mod, W, sm, tag, cargo=None):
    sh1, sc1, g1, sh2, sc2, g2 = mod
    sv, got = {}, {}
    ride = functools.partial(_ride, kind="gather", hosts=FWD_HOSTS, cargo=cargo, got=got)
    h1 = rms_mod_fwd(x, sm["norm1_w"], sc1, sh1, f"rms1_fwd_{tag}")
    P = ride((h1, W["in"], "nn", F32, f"in_proj_{tag}"), host="in_proj")
    pre, xc = conv_fwd(P, sm["conv_w"], sm["conv_b"], f"conv_fwd_{tag}")
    y, st, _ = ssd_fwd(xc, P, sm["dt_bias"], sm["a_log"], sm["d_skip"], f"ssd_fwd_{tag}")
    ysn = ssd_post_fwd(y, P, sm["ssd_norm_w"], f"ssd_post_fwd_{tag}")
    ys = mm(ysn, W["ssd_out"], "nn", F32, f"ssd_out_{tag}")
    os_, ls_ = [], []
    for gi in range(3):
        host = f"attn_fwd{gi}"
        o, l, res = attn_fwd(P, sm["q_norm_w"], sm["k_norm_w"], gi, f"attn_fwd{gi}_{tag}",
                             rider=_cargo_rider("gather", FWD_HOSTS, host, cargo))
        got.update(zip(FWD_HOSTS[host], res))
        os_.append(o)
        ls_.append(l)
    om = attn_merge_fwd(os_, ls_, f"attn_merge_fwd_{tag}")
    ya = mm(om, W["attn_out"], "nn", F32, f"attn_out_{tag}")
    pooled, ypp = pool_fwd(P, W["pool_mix"], sm["pool_scale"], f"pool_fwd_{tag}")
    yp = mm(ypp, W["pool_out"], "nn", F32, f"pool_out_{tag}")
    m = gate_fwd(P, ys, ya, yp, f"gate_fwd_{tag}")
    mo, x1 = mm(m, W["out"], "nn", F32, f"mix_out_{tag}", extra=(x, g1), epi=_epi_resid, out_dtypes=(F32, F32))
    h2 = rms_mod_fwd(x1, sm["norm2_w"], sc2, sh2, f"rms2_fwd_{tag}")
    f, a = ride((h2, W["ff1"], "nn", F32, f"ff1_{tag}"), host="ff1", epi=_epi_relu2, out_dtypes=(F32, BF16),
                col_blocks=True)
    o2, x2 = ride((a, W["ff2"], "nn", F32, f"ff2_{tag}"), host="ff2", extra=(x1, g2), epi=_epi_resid,
                  out_dtypes=(F32, F32))
    sv.update(x=x, h1=h1, P=P, pre=pre, xc=xc, y=y, st=st, ysn=ysn, ys=ys, os=os_, ls=ls_, om=om, ya=ya,
              pooled=pooled, ypp=ypp, yp=yp, m=m, mo=mo, x1=x1, h2=h2, f=f, a=a, o2=o2)
    return x2, sv, got


def layer_bwd(dx2, sv, mod, W, sm, tag, slabs, cargo=None):
    sh1, sc1, g1, sh2, sc2, g2 = mod
    S = dx2.shape[0]
    P = sv["P"]
    gw, gs, got, own, got_own = {}, {}, {}, {}, {}
    ride = functools.partial(_ride, kind="scatter", hosts=BWD_HOSTS, cargo=cargo, got=got)
    ride_own = functools.partial(_ride, kind="scatter", hosts=BWD_HOSTS, cargo=own, got=got_own)
    do2, dg2 = resid_bwd(dx2, g2, sv["o2"], f"resid2_bwd_{tag}")
    gw["ff2"] = ride((sv["a"], do2, "tn", BF16, f"d_w_ff2_{tag}"), host="d_w_ff2")
    df = ride((do2, W["ff2"], "nt", BF16, f"d_f_{tag}"), host="d_a", extra=(sv["f"],), epi=_epi_relu2_bwd)
    gw["ff1"] = ride((sv["h2"], df, "tn", BF16, f"d_w_ff1_{tag}"), host="d_w_ff1", col_blocks=True)
    dh2 = ride((df, W["ff1"], "nt", F32, f"d_h2_{tag}"), host="d_h2", col_blocks=True)
    dx1, gs["norm2_w"], dsc2, dsh2 = rms_mod_bwd(sv["x1"], dh2, dx2, sm["norm2_w"], sc2, f"rms2_bwd_{tag}")
    dmo, dg1 = resid_bwd(dx1, g1, sv["mo"], f"resid1_bwd_{tag}")
    gw["out"] = mm(sv["m"], dmo, "tn", BF16, f"d_w_out_{tag}")
    dm = mm(dmo, W["out"], "nt", F32, f"d_m_{tag}")
    dP = lax.empty((S, NP), BF16)
    dP, dys, dya, dyp = gate_bwd(P, sv["ys"], sv["ya"], sv["yp"], dm, dP, f"gate_bwd_{tag}")
    gw["ssd_out"] = mm(sv["ysn"], dys, "tn", BF16, f"d_w_ssd_out_{tag}")
    dysn = mm(dys, W["ssd_out"], "nt", F32, f"d_ysn_{tag}")
    dy, dP, gs["ssd_norm_w"] = ssd_post_bwd(sv["y"], P, dysn, dP, sm["ssd_norm_w"], f"ssd_post_bwd_{tag}")
    dxc, dP, gs["dt_bias"], gs["a_log"], dDl, res = ssd_bwd(
        sv["xc"], P, sv["st"], dy, dP, sm["dt_bias"], sm["a_log"], sm["d_skip"], f"ssd_bwd_{tag}",
        rider=_cargo_rider("scatter", BWD_HOSTS, "ssd_bwd", cargo))
    got.update(zip(BWD_HOSTS["ssd_bwd"], res))
    gs["d_skip"] = dDl.reshape(SSD_HEADS, SSD_HEAD_DIM).sum(axis=1)
    gs["dt_bias"] = gs["dt_bias"][0, :SSD_HEADS]
    gs["a_log"] = gs["a_log"][0, :SSD_HEADS]
    dP, gs["conv_w"], gs["conv_b"] = conv_bwd(dxc, sv["pre"], P, dP, sm["conv_w"], f"conv_bwd_{tag}")
    gw["attn_out"] = mm(sv["om"], dya, "tn", BF16, f"d_w_attn_out_{tag}")
    dom = mm(dya, W["attn_out"], "nt", F32, f"d_om_{tag}")
    mb = attn_merge_bwd(sv["os"], sv["ls"], dom, f"attn_merge_bwd_{tag}")
    dqw = dkw = None
    for gi in range(3):
        dP, a_, b_ = attn_bwd(P, sv["os"][gi], sv["ls"][gi], mb[gi], mb[3 + gi], dP, sm["q_norm_w"], sm["k_norm_w"], gi,
                              f"attn_bwd{gi}_{tag}")
        dqw = a_ if dqw is None else dqw + a_
        dkw = b_ if dkw is None else dkw + b_
    gs["q_norm_w"], gs["k_norm_w"] = dqw, dkw
    gw["pool_out"] = mm(sv["ypp"], dyp, "tn", BF16, f"d_w_pool_out_{tag}")
    dypp = mm(dyp, W["pool_out"], "nt", F32, f"d_ypp_{tag}")
    dP, dwm, gs["pool_scale"] = pool_bwd(dypp, sv["pooled"], dP, W["pool_mix"], sm["pool_scale"], f"pool_bwd_{tag}")
    gw["pool_mix"] = dwm.astype(BF16)
    own.update(slabs(gw))
    gw["in"] = ride_own((sv["h1"], dP, "tn", BF16, f"d_w_in_{tag}"), host="d_w_in")
    dh1 = ride_own((dP, W["in"], "nt", F32, f"d_h1_{tag}"), host="d_h1")
    dx, gs["norm1_w"], dsc1, dsh1 = rms_mod_bwd(sv["x"], dh1, dx1, sm["norm1_w"], sc1, f"rms1_bwd_{tag}")
    dmod = jnp.concatenate([dsh1, dsc1, dg1, dsh2, dsc2, dg2], axis=1)
    return dx, dmod, gs, gw, slabs({"in": gw["in"]}), got, got_own


def _w_in_to_fused(w):
    E = ATTN_HEAD_DIM
    cols = []
    for h in range(ATTN_HEADS):
        for off in (O_Q, O_K, O_V):
            cols.append(w[:, off + h * E: off + (h + 1) * E])
    cols += [w[:, O_XBC:O_DT], w[:, O_G:IN_WIDTH], w[:, O_Z:O_XBC], w[:, O_U:O_G], w[:, O_DT:O_Q],
             jnp.zeros((w.shape[0], DT_W - SSD_HEADS), w.dtype)]
    return jnp.concatenate(cols, axis=1)


def _fused_to_w_in(g):
    E = ATTN_HEAD_DIM
    q = [g[:, h * HEAD_W: h * HEAD_W + E] for h in range(ATTN_HEADS)]
    k = [g[:, h * HEAD_W + E: h * HEAD_W + 2 * E] for h in range(ATTN_HEADS)]
    v = [g[:, h * HEAD_W + 2 * E: (h + 1) * HEAD_W] for h in range(ATTN_HEADS)]
    return jnp.concatenate([g[:, Z_OFF:Z_OFF + Z_W], g[:, XBC_OFF:XBC_OFF + XBC_W], g[:, DT_OFF:DT_OFF + SSD_HEADS]]
                           + q + k + v + [g[:, U_OFF:U_OFF + U_W], g[:, GATE_OFF:GATE_OFF + GATE_W]], axis=1)


def _cols_from_shards(g):
    return jnp.moveaxis(g, 0, 1).reshape(g.shape[1], N_DEV * g.shape[2])


def _cols_to_shards(w):
    R = w.shape[0]
    return jnp.moveaxis(w.reshape(R, N_DEV, w.shape[1] // N_DEV), 1, 0)


BIG = ("w_in", "w_ssd_out", "w_attn_out", "w_pool_mix", "w_pool_out", "w_out", "w_ff1", "w_ff2")
SMALL = ("b_ada", "norm1_w", "norm2_w", "conv_b", "dt_bias", "a_log", "d_skip", "ssd_norm_w", "q_norm_w", "k_norm_w",
         "pool_scale")
WEIGHTS = ("w_ada", "b_ada", "norm1_w", "norm2_w", "w_in", "conv_w", "conv_b", "dt_bias", "a_log", "d_skip",
           "ssd_norm_w", "w_ssd_out", "q_norm_w", "k_norm_w", "w_attn_out", "w_pool_mix", "pool_scale", "w_pool_out",
           "w_out", "w_ff1", "w_ff2")


def _pack(arrs):
    flat = jnp.concatenate([a.reshape(-1).astype(F32) for a in arrs])
    pad = (-flat.shape[0]) % (8 * LANES)
    return jnp.pad(flat, (0, pad)).reshape(-1, LANES)


def _unpack(packed, shapes):
    flat = packed.reshape(-1)
    out, off = [], 0
    for s in shapes:
        n = int(np.prod(s))
        out.append(flat[off:off + n].reshape(s))
        off += n
    return out


def _pad_lanes(v):
    return jnp.pad(v, ((0, 0), (0, LANES - v.shape[1])))[:, None, :]


def _step(x, c, target, p, m_, v_):
    Ld = p["w_in"].shape[0]
    me = 4 * lax.axis_index("x") + 2 * lax.axis_index("y") + lax.axis_index("c")

    local_bf16 = {k: p[k].astype(BF16) for k in BIG}

    def local_blocks(l):
        return {k: (local_bf16[k], l) for k in BIG}

    blk = local_blocks(0)
    gathered = all_gather([blk[k] for k in BIG] + [c, p["conv_w"]], "gather_first")
    G = dict(zip(BIG, gathered[:len(BIG)]))
    c_all = gathered[len(BIG)].reshape(N_DEV, D_MODEL)
    conv_w_full = jnp.moveaxis(gathered[len(BIG) + 1], 0, 2).reshape(Ld, 4, XBC_W)

    def layer_weights(G):
        W = {}
        W["in"] = _w_in_to_fused(_cols_from_shards(G["w_in"]))
        W["ssd_out"] = G["w_ssd_out"].reshape(1024, D_MODEL)
        W["attn_out"] = _cols_from_shards(G["w_attn_out"])
        W["pool_mix"] = jnp.moveaxis(G["w_pool_mix"], 0, 1).reshape(4, 256, 256)
        W["pool_out"] = G["w_pool_out"].reshape(1024, D_MODEL)
        W["out"] = G["w_out"].reshape(D_MODEL, D_MODEL)
        W["ff1"] = G["w_ff1"]
        W["ff2"] = G["w_ff2"].reshape(4 * D_MODEL, D_MODEL)
        return W

    def slabs(gw):
        out = {}
        if "in" in gw:
            full = _cols_to_shards(_fused_to_w_in(gw["in"]))
            rows = full.shape[1] // W_IN_PARTS
            for q in range(W_IN_PARTS):
                out[f"w_in_{q}"] = full[:, q * rows:(q + 1) * rows]
            return out
        out["w_ssd_out"] = gw["ssd_out"].reshape(N_DEV, -1, D_MODEL)
        out["w_attn_out"] = _cols_to_shards(gw["attn_out"])
        out["w_pool_mix"] = jnp.moveaxis(gw["pool_mix"].reshape(4, N_DEV, 32, 256), 1, 0)
        out["w_pool_out"] = gw["pool_out"].reshape(N_DEV, -1, D_MODEL)
        out["w_out"] = gw["out"].reshape(N_DEV, -1, D_MODEL)
        out["w_ff1"] = gw["ff1"]
        out["w_ff2"] = gw["ff2"].reshape(N_DEV, -1, D_MODEL)
        return out

    def layer_small(l):
        sm = {k: p[k][l][None, :] for k in ("norm1_w", "norm2_w", "conv_b", "ssd_norm_w", "q_norm_w", "k_norm_w",
                                            "pool_scale")}
        for k in ("dt_bias", "a_log", "d_skip"):
            sm[k] = jnp.pad(p[k][l], (0, LANES - SSD_HEADS))[None, :]
        sm["conv_w"] = conv_w_full[l]
        return sm

    cond = silu_rows(c_all, "cond_silu")
    ncol = p["w_ada"].shape[2]
    b_mine = lax.dynamic_slice_in_dim(p["b_ada"], me * ncol, ncol, axis=1)
    parts = [mm(cond, p["w_ada"][l], "nn", F32, f"ada_{l}") + b_mine[l][None, :] for l in range(Ld)]
    mod_cols = jnp.stack(parts, axis=1)
    mod = all_to_all([mod_cols], "scatter_mod")[0]
    mod = jnp.moveaxis(mod, 0, 1).reshape(Ld, 6 * D_MODEL)

    def mods(l):
        return [mod[l, k * D_MODEL:(k + 1) * D_MODEL][None, :] for k in range(6)]

    saved, Ws, sms = [], [], []
    h = x
    for l in range(Ld):
        Ws.append(layer_weights(G))
        sms.append(layer_small(l))
        h, sv, G = layer_fwd(h, mods(l), Ws[l], sms[l], f"l{l}", local_blocks(l + 1) if l + 1 < Ld else None)
        saved.append(sv)
    lsum, dh = loss_fwd_bwd(h, target, "loss")
    loss = 0.5 / D_MODEL * jnp.sum(lsum)
    gss, dmods, recvs = [None] * Ld, [None] * Ld, [dict() for _ in range(Ld)]
    cargo = None
    for l in reversed(range(Ld)):
        dh, dmods[l], gss[l], _, nxt, got, got_own = layer_bwd(dh, saved[l], mods(l), Ws[l], sms[l], f"l{l}", slabs, cargo)
        recvs[l].update(got_own)
        if cargo is not None:
            recvs[l + 1].update(got)
        cargo = nxt
    dmod = jnp.concatenate(dmods, axis=0)
    dmod_cols = jnp.moveaxis(dmod.reshape(Ld, N_DEV, ncol), 1, 0)
    parts_in = [f"w_in_{q}" for q in range(W_IN_PARTS)]
    last = all_to_all([cargo[k] for k in parts_in] + [dmod_cols], "scatter_last")
    recvs[0].update(zip(parts_in, last[:W_IN_PARTS]))
    dmod_all = last[W_IN_PARTS]

    out_g, out_d, out_m, out_v = {}, {}, {}, {}

    def finish(k, res):
        shp = p[k].shape
        out_g[k], out_d[k], out_m[k], out_v[k] = [t.reshape(shp) for t in res]

    for k in BIG:
        C = p[k].shape[-1]
        rows = int(np.prod(p[k].shape[1:-1]))
        res = None
        for l in reversed(range(Ld)):
            pieces = [(f"{k}_{q}", rows // W_IN_PARTS) for q in range(W_IN_PARTS)] if k == "w_in" else [(k, rows)]
            for q, (key, nrows) in enumerate(pieces):
                res = adamw(recvs[l][key].reshape(N_DEV, nrows, C), p[k].reshape(-1, C), m_[k].reshape(-1, C),
                            v_[k].reshape(-1, C), f"adamw_{key}_l{l}", row0=l * rows + q * nrows, prev=res)
        finish(k, res)
    g_ada = jnp.stack([mm(cond, dmod_all[:, l], "tn", F32, f"d_w_ada_{l}") for l in range(Ld)], axis=0)
    finish("w_ada", adamw(g_ada.reshape(1, -1, ncol), p["w_ada"].reshape(-1, ncol), m_["w_ada"].reshape(-1, ncol),
                          v_["w_ada"].reshape(-1, ncol), "adamw_w_ada"))

    small_g = {k: jnp.stack([gss[l][k].reshape(p[k].shape[1:]) for l in range(Ld)], axis=0) for k in SMALL if k != "b_ada"}
    small_g["b_ada"] = dmod
    conv_g = jnp.stack([gss[l]["conv_w"] for l in range(Ld)], axis=0)
    packed = _pack([small_g[k] for k in SMALL] + [conv_g])
    parts_small = all_gather([packed], "gather_small_grads")[0]
    zeros_conv = jnp.zeros(conv_g.shape, F32)
    res = adamw(parts_small, _pack([p[k] for k in SMALL] + [zeros_conv]), _pack([m_[k] for k in SMALL] + [zeros_conv]),
                _pack([v_[k] for k in SMALL] + [zeros_conv]), "adamw_small")
    shapes = [p[k].shape for k in SMALL] + [conv_g.shape]
    unp = [_unpack(t, shapes) for t in res]
    for i, k in enumerate(SMALL):
        out_g[k], out_d[k], out_m[k], out_v[k] = unp[0][i], unp[1][i], unp[2][i], unp[3][i]
    cw = p["conv_w"].shape[-1]
    conv_mine = lax.dynamic_slice_in_dim(unp[0][len(SMALL)], me * cw, cw, axis=2)
    finish("conv_w", adamw(conv_mine.reshape(1, -1, cw), p["conv_w"].reshape(-1, cw), m_["conv_w"].reshape(-1, cw),
                           v_["conv_w"].reshape(-1, cw), "adamw_conv_w"))

    loss = lax.psum(loss, ("x", "y", "c"))
    return loss, dh, out_g, out_d, out_m, out_v


def kernel(x, c, w_ada, b_ada, norm1_w, norm2_w, w_in, conv_w, conv_b, dt_bias, a_log, d_skip, ssd_norm_w, w_ssd_out, q_norm_w, k_norm_w, w_attn_out, w_pool_mix, pool_scale, w_pool_out, w_out, w_ff1, w_ff2, loss_target, m_w_ada, m_b_ada, m_norm1_w, m_norm2_w, m_w_in, m_conv_w, m_conv_b, m_dt_bias, m_a_log, m_d_skip, m_ssd_norm_w, m_w_ssd_out, m_q_norm_w, m_k_norm_w, m_w_attn_out, m_w_pool_mix, m_pool_scale, m_w_pool_out, m_w_out, m_w_ff1, m_w_ff2, v_w_ada, v_b_ada, v_norm1_w, v_norm2_w, v_w_in, v_conv_w, v_conv_b, v_dt_bias, v_a_log, v_d_skip, v_ssd_norm_w, v_w_ssd_out, v_q_norm_w, v_k_norm_w, v_w_attn_out, v_w_pool_mix, v_pool_scale, v_w_pool_out, v_w_out, v_w_ff1, v_w_ff2):
    p = dict(w_ada=w_ada, b_ada=b_ada, norm1_w=norm1_w, norm2_w=norm2_w, w_in=w_in, conv_w=conv_w, conv_b=conv_b,
             dt_bias=dt_bias, a_log=a_log, d_skip=d_skip, ssd_norm_w=ssd_norm_w, w_ssd_out=w_ssd_out, q_norm_w=q_norm_w,
             k_norm_w=k_norm_w, w_attn_out=w_attn_out, w_pool_mix=w_pool_mix, pool_scale=pool_scale,
             w_pool_out=w_pool_out, w_out=w_out, w_ff1=w_ff1, w_ff2=w_ff2)
    m_ = dict(w_ada=m_w_ada, b_ada=m_b_ada, norm1_w=m_norm1_w, norm2_w=m_norm2_w, w_in=m_w_in, conv_w=m_conv_w,
              conv_b=m_conv_b, dt_bias=m_dt_bias, a_log=m_a_log, d_skip=m_d_skip, ssd_norm_w=m_ssd_norm_w,
              w_ssd_out=m_w_ssd_out, q_norm_w=m_q_norm_w, k_norm_w=m_k_norm_w, w_attn_out=m_w_attn_out,
              w_pool_mix=m_w_pool_mix, pool_scale=m_pool_scale, w_pool_out=m_w_pool_out, w_out=m_w_out, w_ff1=m_w_ff1,
              w_ff2=m_w_ff2)
    v_ = dict(w_ada=v_w_ada, b_ada=v_b_ada, norm1_w=v_norm1_w, norm2_w=v_norm2_w, w_in=v_w_in, conv_w=v_conv_w,
              conv_b=v_conv_b, dt_bias=v_dt_bias, a_log=v_a_log, d_skip=v_d_skip, ssd_norm_w=v_ssd_norm_w,
              w_ssd_out=v_w_ssd_out, q_norm_w=v_q_norm_w, k_norm_w=v_k_norm_w, w_attn_out=v_w_attn_out,
              w_pool_mix=v_w_pool_mix, pool_scale=v_pool_scale, w_pool_out=v_w_pool_out, w_out=v_w_out, w_ff1=v_w_ff1,
              w_ff2=v_w_ff2)
    loss, dx, g, d, nm, nv = _step(x[0], c, loss_target[0], p, m_, v_)
    return (loss, dx[None], *[g[k] for k in WEIGHTS], *[d[k] for k in WEIGHTS], *[nm[k] for k in WEIGHTS],
            *[nv[k] for k in WEIGHTS])
```

```python
import functools
import math

import numpy as np
import jax
import jax.numpy as jnp
from jax import lax
from jax.experimental import pallas as pl
from jax.experimental.pallas import tpu as pltpu

F32 = jnp.float32
BF16 = jnp.bfloat16
MESH = pl.DeviceIdType.MESH
HIGHEST = lax.Precision.HIGHEST

N_DEV = 8
V7X_VMEM_LIMIT_BYTES = 56 * 1024 * 1024
LANES = 128
ROW_TILE = 512

EPS = 1e-6
D_MODEL = 1024
SSD_HEADS = 16
SSD_HEAD_DIM = 64
SSD_CHUNK = 128
SSD_STATE = 128
ATTN_HEADS = 12
ATTN_HEAD_DIM = 128
ATTN_STEPS = 128
DILATIONS = (1, 4, 16)
POOL_WINDOWS = (2, 4, 8, 16)
POOL_HALO = 16
CONV_HALO = 8
NEG = -1e30

ADAM_LR = 0.001
ADAM_B1 = 0.9
ADAM_B2 = 0.999
ADAM_EPS = 1e-08
ADAM_WD = 0.01
ADAM_STEP = 10

QKV_OFF, QKV_W = 0, 4608
XBC_OFF, XBC_W = 4608, 1536
GATE_OFF, GATE_W = 6144, 3072
Z_OFF, Z_W = 9216, 1024
U_OFF, U_W = 10240, 1024
DT_OFF, DT_W = 11264, 256
NP = 11520
IN_WIDTH = 11280
O_Z, O_XBC, O_DT, O_Q, O_K, O_V, O_U, O_G = 0, 1024, 2560, 2576, 4112, 5648, 7184, 8208


def _alibi_slopes(n):
    def pow2(k):
        start = 2.0 ** (-8.0 / k)
        return [start ** (i + 1) for i in range(k)]
    if math.log2(n).is_integer():
        s = pow2(n)
    else:
        c = 2 ** math.floor(math.log2(n))
        s = pow2(c) + pow2(2 * c)[0::2][: n - c]
    return np.sort(np.asarray(s, np.float32))[::-1].copy()


SLOPES = _alibi_slopes(ATTN_HEADS).reshape(3, 4)


def _cp(sem=None):
    return pltpu.CompilerParams(dimension_semantics=sem, vmem_limit_bytes=V7X_VMEM_LIMIT_BYTES)


def _pick(dim, cap):
    if dim <= cap:
        return dim
    t = cap - cap % LANES
    while t >= LANES:
        if dim % t == 0:
            return t
        t -= LANES
    raise ValueError((dim, cap))


def _sigmoid(x):
    return 1.0 / (1.0 + jnp.exp(-x))


def _silu(x):
    return x * _sigmoid(x)


def _dsilu(x):
    s = _sigmoid(x)
    return s * (1.0 + x * (1.0 - s))


def _softplus(x):
    return jnp.maximum(x, 0.0) + jnp.log(1.0 + jnp.exp(-jnp.abs(x)))


def _dot(a, b, dims):
    return lax.dot_general(a, b, (dims, ((), ())), preferred_element_type=F32)


def _nn(a, b):
    return _dot(a, b, ((1,), (0,)))


def _nt(a, b):
    return _dot(a, b, ((1,), (1,)))


def _tn(a, b):
    return _dot(a, b, ((0,), (0,)))


def _iota(shape, axis):
    return lax.broadcasted_iota(jnp.int32, shape, axis)


def _position():
    return lax.axis_index("x"), lax.axis_index("y"), lax.axis_index("c")


def _index(px, py, pc):
    return 4 * px + 2 * py + pc


def _remote(src, dst, send_sems, recv_sems, a, k, to):
    return pltpu.make_async_remote_copy(src_ref=src, dst_ref=dst, send_sem=send_sems.at[a, k], recv_sem=recv_sems.at[a, k],
                                        device_id=to, device_id_type=MESH)


def _gather_plan(srcs, dsts, send_sems, recv_sems, local_sems):
    x, y, c = _position()
    me, sib = (x, y, c), (x, y, 1 - c)
    chips = [(1 - x, y), (x, 1 - y), (1 - x, 1 - y)]
    n = len(srcs)

    def start():
        for a in range(n):
            mine = dsts[a].at[_index(*me)]
            pltpu.make_async_copy(srcs[a], mine, local_sems.at[a]).start()
            _remote(srcs[a], mine, send_sems, recv_sems, a, 0, sib).start()
            for j, chip in enumerate(chips):
                _remote(srcs[a], mine, send_sems, recv_sems, a, 1 + j, (*chip, c)).start()

    def finish():
        for a in range(n):
            for j, chip in enumerate(chips):
                blk = dsts[a].at[_index(*chip, c)]
                _remote(blk, blk, send_sems, recv_sems, a, 1 + j, me).wait_recv()
                _remote(blk, blk, send_sems, recv_sems, a, 4 + j, sib).start()
        for a in range(n):
            mine = dsts[a].at[_index(*me)]
            blk = dsts[a].at[_index(*sib)]
            _remote(blk, blk, send_sems, recv_sems, a, 0, me).wait_recv()
            for j, chip in enumerate(chips):
                blk = dsts[a].at[_index(*chip, 1 - c)]
                _remote(blk, blk, send_sems, recv_sems, a, 4 + j, me).wait_recv()
            for k in range(N_DEV - 1):
                _remote(mine, mine, send_sems, recv_sems, a, k, me).wait_send()
            pltpu.make_async_copy(srcs[a], mine, local_sems.at[a]).wait()

    return start, finish


def _scatter_plan(srcs, dsts, send_sems, recv_sems, local_sems):
    x, y, c = _position()
    me = _index(x, y, c)
    n = len(srcs)
    copies = []
    for a in range(n):
        copies.append(pltpu.make_async_copy(srcs[a].at[me], dsts[a].at[me], local_sems.at[a]))
        for k in range(1, N_DEV):
            px, py, pc = x ^ ((k >> 2) & 1), y ^ ((k >> 1) & 1), c ^ (k & 1)
            copies.append(_remote(srcs[a].at[_index(px, py, pc)], dsts[a].at[me], send_sems, recv_sems, a, k - 1, (px, py, pc)))

    def start():
        for cp in copies:
            cp.start()

    def finish():
        for cp in copies:
            cp.wait()

    return start, finish


class Rider:
    def __init__(self, kind, arrs):
        self.kind = kind
        items = [a if isinstance(a, tuple) else (a, None) for a in arrs]
        self.arrs = [a for a, _ in items]
        self.lead = [i for _, i in items]

    def _shapes(self):
        return [a.shape if i is None else a.shape[1:] for a, i in zip(self.arrs, self.lead)]

    def out_shapes(self):
        if self.kind == "gather":
            return [pltpu.HBM((N_DEV,) + s, a.dtype) for s, a in zip(self._shapes(), self.arrs)]
        return [pltpu.HBM(s, a.dtype) for s, a in zip(self._shapes(), self.arrs)]

    def sems(self):
        n = len(self.arrs)
        return [pltpu.SemaphoreType.DMA((n, N_DEV - 1)), pltpu.SemaphoreType.DMA((n, N_DEV - 1)),
                pltpu.SemaphoreType.DMA((n,))]

    def plan(self, srcs, dsts, sems):
        srcs = [s if i is None else s.at[i] for s, i in zip(srcs, self.lead)]
        return (_gather_plan if self.kind == "gather" else _scatter_plan)(srcs, dsts, *sems)


def _pcall(body, grid, in_specs, out_specs, out_shape, name, ins, scratch=(), aliases=None, rider=None):
    in_specs, out_specs, out_shape, scratch = list(in_specs), list(out_specs), list(out_shape), list(scratch)
    sem = ("arbitrary",) * len(grid)
    if rider is None:
        res = pl.pallas_call(body, grid=grid, in_specs=in_specs, out_specs=out_specs, out_shape=out_shape,
                             scratch_shapes=scratch, name=name, input_output_aliases=aliases or {},
                             compiler_params=_cp(sem))(*ins)
        return list(res), []
    n_in, n_out, n_sc, nr = len(in_specs), len(out_shape), len(scratch), len(rider.arrs)

    def wrapped(*refs):
        own_in, srcs = refs[:n_in], refs[n_in:n_in + nr]
        own_out, dsts = refs[n_in + nr:n_in + nr + n_out], refs[n_in + nr + n_out:n_in + 2 * nr + n_out]
        own_sc, sems = refs[n_in + 2 * nr + n_out:n_in + 2 * nr + n_out + n_sc], refs[n_in + 2 * nr + n_out + n_sc:]
        ids = [pl.program_id(ax) for ax in range(len(grid))]
        first = functools.reduce(jnp.logical_and, [i == 0 for i in ids])
        last = functools.reduce(jnp.logical_and, [i == g - 1 for i, g in zip(ids, grid)])
        start, finish = rider.plan(srcs, dsts, sems)
        pl.when(first)(start)
        body(*own_in, *own_out, *own_sc)
        pl.when(last)(finish)

    hbm = pl.BlockSpec(memory_space=pltpu.HBM)
    res = pl.pallas_call(
        wrapped, grid=grid, in_specs=in_specs + [hbm] * nr, out_specs=out_specs + [hbm] * nr,
        out_shape=out_shape + rider.out_shapes(), scratch_shapes=scratch + rider.sems(), name=name,
        input_output_aliases=aliases or {},
        compiler_params=pltpu.CompilerParams(dimension_semantics=sem, vmem_limit_bytes=V7X_VMEM_LIMIT_BYTES,
                                             has_side_effects=True))(*ins, *rider.arrs)
    return list(res[:n_out]), list(res[n_out:])


def _exchange(kind, arrs, name):
    rd = Rider(kind, arrs)
    n = len(arrs)

    def body(*refs):
        start, finish = rd.plan(refs[:n], refs[n:2 * n], refs[2 * n:])
        start()
        finish()

    hbm = pl.BlockSpec(memory_space=pltpu.HBM)
    return pl.pallas_call(
        body, in_specs=[hbm] * n, out_specs=[hbm] * n, out_shape=rd.out_shapes(), scratch_shapes=rd.sems(),
        name=name, compiler_params=pltpu.CompilerParams(has_side_effects=True))(*rd.arrs)


def all_gather(arrs, name):
    return _exchange("gather", arrs, name)


def all_to_all(arrs, name):
    return _exchange("scatter", arrs, name)


def mm(a, b, mode, out_dtype, name, rider=None, extra=(), epi=None, out_dtypes=None, col_blocks=False, tm_cap=1024,
       tn_cap=1280, tk_cap=2048):
    blk = None
    if col_blocks and mode != "tn":
        nb, rows_b, blk = b.shape
        b_shape = (rows_b, nb * blk)
    else:
        b_shape = b.shape
    if mode == "nn":
        (M, K), (K2, N) = a.shape, b_shape
    elif mode == "nt":
        (M, K), (N, K2) = a.shape, b_shape
    else:
        (K, M), (K2, N) = a.shape, b_shape
        if col_blocks:
            blk = N // N_DEV
    assert K == K2, (a.shape, b.shape, mode)
    tm, tn, tk = _pick(M, tm_cap), _pick(N, tn_cap), _pick(K, tk_cap)
    if col_blocks and mode == "nt":
        tk = blk
    elif col_blocks:
        tn = blk
    nk = K // tk
    if mode == "tn":
        a_spec = pl.BlockSpec((tk, tm), lambda i, j, k: (k, i))
    else:
        a_spec = pl.BlockSpec((tm, tk), lambda i, j, k: (i, k))
    if mode == "nt":
        b_spec = (pl.BlockSpec((None, tn, tk), lambda i, j, k: (k, j, 0)) if col_blocks
                  else pl.BlockSpec((tn, tk), lambda i, j, k: (j, k)))
    else:
        b_spec = (pl.BlockSpec((None, tk, tn), lambda i, j, k: (j, k, 0)) if col_blocks and mode == "nn"
                  else pl.BlockSpec((tk, tn), lambda i, j, k: (k, j)))
    dims = {"nn": ((1,), (0,)), "nt": ((1,), (1,)), "tn": ((0,), (0,))}[mode]
    o_spec = pl.BlockSpec((tm, tn), lambda i, j, k: (i, j))
    blocked_out = col_blocks and mode == "tn"
    if blocked_out:
        assert not extra and epi is None
        o_spec = pl.BlockSpec((None, tm, tn), lambda i, j, k: (j, i, 0))
    v_spec = pl.BlockSpec((1, tn), lambda i, j, k: (0, j))
    grid = (M // tm, N // tn, nk)
    out_dtypes = list(out_dtypes) if out_dtypes is not None else [out_dtype]
    nr = len(rider.arrs) if rider is not None else 0
    ne, no = len(extra), len(out_dtypes)
    extra_specs = []
    for e in extra:
        assert e.shape in ((M, N), (1, N)), (e.shape, M, N)
        extra_specs.append(o_spec if e.shape[0] == M and M != 1 else v_spec)

    def body(*refs):
        a_ref, b_ref = refs[0], refs[1]
        e_refs = refs[2:2 + ne]
        srcs = refs[2 + ne:2 + ne + nr]
        o_refs = refs[2 + ne + nr:2 + ne + nr + no]
        dsts = refs[2 + ne + nr + no:2 + ne + 2 * nr + no]
        rest = refs[2 + ne + 2 * nr + no:]
        i, j, k = pl.program_id(0), pl.program_id(1), pl.program_id(2)
        if nr:
            start, finish = rider.plan(srcs, dsts, rest[-3:])

            @pl.when(jnp.logical_and(jnp.logical_and(i == 0, j == 0), k == 0))
            def _():
                start()

        def write(r):
            vals = epi(r, *[e[...] for e in e_refs]) if epi is not None else (r,)
            for o_ref, val, dt in zip(o_refs, vals, out_dtypes):
                o_ref[...] = val.astype(dt)

        prod = _dot(a_ref[...].astype(BF16), b_ref[...].astype(BF16), dims)
        if nk == 1:
            write(prod)
        else:
            acc = rest[0]

            @pl.when(k == 0)
            def _():
                acc[...] = prod

            @pl.when(k > 0)
            def _():
                acc[...] += prod

            @pl.when(k == nk - 1)
            def _():
                write(acc[...])

        if nr:
            @pl.when(jnp.logical_and(jnp.logical_and(i == grid[0] - 1, j == grid[1] - 1), k == nk - 1))
            def _():
                finish()

    hbm = pl.BlockSpec(memory_space=pltpu.HBM)
    out_shape = [jax.ShapeDtypeStruct((N_DEV, M, blk) if blocked_out else (M, N), dt) for dt in out_dtypes]
    scratch = [pltpu.VMEM((tm, tn), F32)] if nk > 1 else []
    if nr:
        res = pl.pallas_call(
            body, grid=grid, in_specs=[a_spec, b_spec] + extra_specs + [hbm] * nr, out_specs=[o_spec] * no + [hbm] * nr,
            out_shape=out_shape + rider.out_shapes(), scratch_shapes=scratch + rider.sems(), name=name,
            compiler_params=pltpu.CompilerParams(dimension_semantics=("arbitrary",) * 3,
                                                 vmem_limit_bytes=V7X_VMEM_LIMIT_BYTES, has_side_effects=True),
        )(a, b, *extra, *rider.arrs)
        outs, got = list(res[:no]), list(res[no:])
        return (outs[0] if no == 1 else outs), got
    res = pl.pallas_call(
        body, grid=grid, in_specs=[a_spec, b_spec] + extra_specs, out_specs=[o_spec] * no, out_shape=out_shape,
        scratch_shapes=scratch, name=name, compiler_params=_cp(("parallel", "parallel", "arbitrary")))(a, b, *extra)
    return res[0] if no == 1 else list(res)


def _rows(tile, w, cb=0):
    return pl.BlockSpec((tile, w), lambda i: (i, cb))


def _vec(shape):
    nd = len(shape)
    return pl.BlockSpec(shape, lambda i: (0,) * nd)


def _any():
    return pl.BlockSpec(memory_space=pl.ANY)


def _sds(shape, dtype):
    return jax.ShapeDtypeStruct(shape, dtype)


def _rowcall(body, n_rows, tile, in_specs, out_specs, out_shape, name, aliases=None):
    return pl.pallas_call(
        body, grid=(n_rows // tile,), in_specs=in_specs, out_specs=out_specs, out_shape=out_shape,
        name=name, input_output_aliases=aliases or {}, compiler_params=_cp(("arbitrary",)))


def _zero_at_first(step, *refs):
    @pl.when(step == 0)
    def _():
        for r in refs:
            r[...] = jnp.zeros_like(r)


def rms_mod_fwd(x, nw, sc, sh, name):
    S, D = x.shape
    T = ROW_TILE

    def body(x_ref, nw_ref, sc_ref, sh_ref, h_ref):
        xv = x_ref[...]
        r = lax.rsqrt(jnp.mean(xv * xv, axis=-1, keepdims=True) + EPS)
        y = xv * r * nw_ref[...]
        h_ref[...] = (y * (1.0 + sc_ref[...]) + sh_ref[...]).astype(BF16)

    return _rowcall(body, S, T, [_rows(T, D), _vec((1, D)), _vec((1, D)), _vec((1, D))], _rows(T, D),
                    _sds((S, D), BF16), name)(x, nw, sc, sh)


def rms_mod_bwd(x, dh, dres, nw, sc, name):
    S, D = x.shape
    T = ROW_TILE

    def body(x_ref, dh_ref, dres_ref, nw_ref, sc_ref, dx_ref, dnw_ref, dsc_ref, dsh_ref):
        _zero_at_first(pl.program_id(0), dnw_ref, dsc_ref, dsh_ref)
        xv = x_ref[...]
        d = dh_ref[...]
        r = lax.rsqrt(jnp.mean(xv * xv, axis=-1, keepdims=True) + EPS)
        xh = xv * r
        nwv = nw_ref[...]
        dsh_ref[...] += jnp.sum(d, axis=0, keepdims=True)
        dsc_ref[...] += jnp.sum(d * (xh * nwv), axis=0, keepdims=True)
        dy = d * (1.0 + sc_ref[...])
        dnw_ref[...] += jnp.sum(dy * xh, axis=0, keepdims=True)
        dxh = dy * nwv
        dx_ref[...] = dres_ref[...] + r * (dxh - xh * jnp.mean(dxh * xh, axis=-1, keepdims=True))

    v = _vec((1, D))
    return _rowcall(body, S, T, [_rows(T, D), _rows(T, D), _rows(T, D), v, v],
                    [_rows(T, D), v, v, v],
                    [_sds((S, D), F32), _sds((1, D), F32), _sds((1, D), F32), _sds((1, D), F32)], name)(x, dh, dres, nw, sc)


def _shift_down(tile, halo, k):
    if k == 0:
        return tile
    hr = halo.shape[0]
    rolled = pltpu.roll(tile, k, 0)
    rh = pltpu.roll(halo, k, 0)
    rid = _iota(halo.shape, 0)
    first = jnp.where(rid < k, rh, rolled[:hr])
    return jnp.concatenate([first, rolled[hr:]], axis=0)


def _shift_up(tile, nxt, k):
    if k == 0:
        return tile
    T = tile.shape[0]
    hr = nxt.shape[0]
    rolled = pltpu.roll(tile, T - k, 0)
    rn = pltpu.roll(nxt, hr - k, 0)
    rid = _iota(nxt.shape, 0)
    last = jnp.where(rid >= hr - k, rn, rolled[T - hr:])
    return jnp.concatenate([rolled[:T - hr], last], axis=0)


CONV_CB = 512


def conv_fwd(P, conv_w, conv_b, name):
    S = P.shape[0]
    T = ROW_TILE
    nj = XBC_W // CONV_CB
    cb0 = XBC_OFF // CONV_CB
    hb = T // CONV_HALO

    def body(x_ref, h_ref, w_ref, b_ref, pre_ref, xc_ref):
        i = pl.program_id(0)
        xb = x_ref[...]
        xh = jnp.where(i > 0, h_ref[...], 0.0)
        acc = b_ref[...] + w_ref[3:4, :] * xb
        for idx in range(3):
            acc = acc + w_ref[idx:idx + 1, :] * _shift_down(xb, xh, 3 - idx)
        pre_ref[...] = acc
        xc_ref[...] = _silu(acc)

    blk = pl.BlockSpec((T, CONV_CB), lambda i, j: (i, j))
    return pl.pallas_call(
        body, grid=(S // T, nj),
        in_specs=[pl.BlockSpec((T, CONV_CB), lambda i, j: (i, cb0 + j)),
                  pl.BlockSpec((CONV_HALO, CONV_CB), lambda i, j: (jnp.maximum(i * hb - 1, 0), cb0 + j)),
                  pl.BlockSpec((4, CONV_CB), lambda i, j: (0, j)),
                  pl.BlockSpec((1, CONV_CB), lambda i, j: (0, j))],
        out_specs=[blk, blk], out_shape=[_sds((S, XBC_W), F32), _sds((S, XBC_W), F32)], name=name,
        compiler_params=_cp(("arbitrary", "arbitrary")))(P, P, conv_w, conv_b)


def conv_bwd(dxc, pre, P, dP, conv_w, name):
    S = P.shape[0]
    T = ROW_TILE
    nt = S // T
    nj = XBC_W // CONV_CB
    cb0 = XBC_OFF // CONV_CB
    hb = T // CONV_HALO

    def body(d_ref, dn_ref, p_ref, pn_ref, x_ref, xh_ref, w_ref, dp_in, dp_ref, dw_ref, db_ref):
        i = pl.program_id(1)
        _zero_at_first(i, dw_ref, db_ref)
        dpre = d_ref[...] * _dsilu(p_ref[...])
        dpn = jnp.where(i < nt - 1, dn_ref[...] * _dsilu(pn_ref[...]), 0.0)
        db_ref[...] += jnp.sum(dpre, axis=0, keepdims=True)
        xb = x_ref[...]
        xh = jnp.where(i > 0, xh_ref[...], 0.0)
        dx = jnp.zeros_like(dpre)
        for idx in range(4):
            k = 3 - idx
            dw_ref[idx:idx + 1, :] += jnp.sum(dpre * _shift_down(xb, xh, k), axis=0, keepdims=True)
            dx = dx + w_ref[idx:idx + 1, :] * _shift_up(dpre, dpn, k)
        dp_ref[...] = dx.astype(BF16)

    cur = pl.BlockSpec((T, CONV_CB), lambda j, i: (i, j))
    nxt = pl.BlockSpec((CONV_HALO, CONV_CB), lambda j, i: (jnp.minimum((i + 1) * hb, S // CONV_HALO - 1), j))
    return pl.pallas_call(
        body, grid=(nj, nt),
        in_specs=[cur, nxt, cur, nxt,
                  pl.BlockSpec((T, CONV_CB), lambda j, i: (i, cb0 + j)),
                  pl.BlockSpec((CONV_HALO, CONV_CB), lambda j, i: (jnp.maximum(i * hb - 1, 0), cb0 + j)),
                  pl.BlockSpec((4, CONV_CB), lambda j, i: (0, j)), _any()],
        out_specs=[pl.BlockSpec((T, CONV_CB), lambda j, i: (i, cb0 + j)),
                   pl.BlockSpec((4, CONV_CB), lambda j, i: (0, j)),
                   pl.BlockSpec((1, CONV_CB), lambda j, i: (0, j))],
        out_shape=[_sds(dP.shape, BF16), _sds((4, XBC_W), F32), _sds((1, XBC_W), F32)],
        input_output_aliases={7: 0}, name=name,
        compiler_params=_cp(("arbitrary", "arbitrary")))(dxc, dxc, pre, pre, P, P, conv_w, dP)


def _col(v, h, lane):
    return jnp.sum(jnp.where(lane == h, v, 0.0), axis=1, keepdims=True)


def _row(v, h, rowi):
    return jnp.sum(jnp.where(rowi == h, v, 0.0), axis=0, keepdims=True)


def _ssd_common(dt_ref, dtb_ref, al_ref):
    L = SSD_CHUNK
    lane = _iota((L, L), 1)
    rowi = _iota((L, L), 0)
    dtr = dt_ref[...] + dtb_ref[...]
    dt = _softplus(dtr)
    A = -jnp.exp(al_ref[...])
    a = dt * A
    tril = (rowi >= lane).astype(F32)
    ac = jnp.dot(tril, a, precision=HIGHEST, preferred_element_type=F32)
    acT = ac.T
    last = _row(ac, L - 1, rowi)
    return dict(lane=lane, rowi=rowi, dtr=dtr, dt=dt, A=A, ac=ac, acT=acT, last=last,
                eac=jnp.exp(ac), fac=jnp.exp(last - ac), cd=jnp.exp(last),
                lo=lane < SSD_HEAD_DIM, causal=rowi >= lane)


def _pair(v, h0, q):
    lane = q["lane"][:v.shape[0]]
    lo = q["lo"][:v.shape[0]]
    return jnp.where(lo, _col(v, h0, lane), _col(v, h0 + 1, lane))


def _decay(q, h):
    seg = _col(q["ac"], h, q["lane"]) - _row(q["acT"], h, q["rowi"])
    return jnp.exp(jnp.where(q["causal"], seg, NEG))


def ssd_fwd(xc, P, dtb, alog, dsk, name, rider=None):
    S = xc.shape[0]
    L = SSD_CHUNK
    nc = S // L

    def body(xs_ref, b_ref, c_ref, dt_ref, dtb_ref, al_ref, dk_ref, y_ref, st_ref, H):
        @pl.when(pl.program_id(0) == 0)
        def _():
            H[...] = jnp.zeros_like(H)

        st_ref[0] = H[...]
        q = _ssd_common(dt_ref, dtb_ref, al_ref)
        dk = dk_ref[...]
        for g in range(2):
            Bg = b_ref[:, g * L:(g + 1) * L].astype(BF16)
            Cg = c_ref[:, g * L:(g + 1) * L].astype(BF16)
            cb = _nt(Cg, Bg)
            for jj in range(4):
                j = g * 4 + jj
                h0 = 2 * j
                sl = slice(j * L, (j + 1) * L)
                xs_p = xs_ref[:, sl]
                X = xs_p * _pair(q["dt"], h0, q)
                Xb = X.astype(BF16)
                G0 = (cb * _decay(q, h0)).astype(BF16)
                G1 = (cb * _decay(q, h0 + 1)).astype(BF16)
                Yd = jnp.where(q["lo"], _nn(G0, Xb), _nn(G1, Xb))
                Hp = H[:, sl]
                Yo = _nn(Cg, Hp.astype(BF16)) * _pair(q["eac"], h0, q)
                y_ref[:, sl] = Yd + Yo + _pair(dk, h0, q) * xs_p
                Xd = (X * _pair(q["fac"], h0, q)).astype(BF16)
                H[:, sl] = _pair(q["cd"], h0, q) * Hp + _tn(Bg, Xd)

    v = _vec((1, LANES))
    (y, st), got = _pcall(
        body, (nc,),
        [pl.BlockSpec((L, 1024), lambda c: (c, 0)),
         pl.BlockSpec((L, 256), lambda c: (c, 4)),
         pl.BlockSpec((L, 256), lambda c: (c, 5)),
         pl.BlockSpec((L, LANES), lambda c: (c, DT_OFF // LANES)), v, v, v],
        [pl.BlockSpec((L, 1024), lambda c: (c, 0)),
         pl.BlockSpec((1, SSD_STATE, 1024), lambda c: (c, 0, 0))],
        [_sds((S, 1024), F32), _sds((nc, SSD_STATE, 1024), F32)], name, (xc, xc, xc, P, dtb, alog, dsk),
        scratch=[pltpu.VMEM((SSD_STATE, 1024), F32)], rider=rider)
    return y, st, got


def ssd_bwd(xc, P, st, dy, dP, dtb, alog, dsk, name, rider=None):
    S = xc.shape[0]
    L = SSD_CHUNK
    nc = S // L

    def body(xs_ref, b_ref, c_ref, dt_ref, st_ref, dy_ref, dtb_ref, al_ref, dk_ref, dp_in,
             dxc_ref, dp_ref, ddtb_ref, dal_ref, dD_ref, dH):
        step = pl.program_id(0)

        @pl.when(step == 0)
        def _():
            dH[...] = jnp.zeros_like(dH)

        _zero_at_first(step, ddtb_ref, dal_ref, dD_ref)
        q = _ssd_common(dt_ref, dtb_ref, al_ref)
        lane, rowi, lo = q["lane"], q["rowi"], q["lo"]
        lane1 = lane[:1]
        lo1 = lo[:1]
        dk = dk_ref[...]
        dac = jnp.zeros((L, L), F32)
        dacT = jnp.zeros((L, L), F32)
        dlast = jnp.zeros((1, L), F32)
        ddt = jnp.zeros((L, L), F32)

        def put(h0, s0, s1):
            return jnp.where(lane == h0, s0, 0.0) + jnp.where(lane == h0 + 1, s1, 0.0)

        def put1(h0, s0, s1):
            return jnp.where(lane1 == h0, s0, 0.0) + jnp.where(lane1 == h0 + 1, s1, 0.0)

        def hsum(v):
            return (jnp.sum(jnp.where(lo, v, 0.0), axis=1, keepdims=True),
                    jnp.sum(jnp.where(lo, 0.0, v), axis=1, keepdims=True))

        for g in range(2):
            Bg = b_ref[:, g * L:(g + 1) * L].astype(BF16)
            Cg = c_ref[:, g * L:(g + 1) * L].astype(BF16)
            cb = _nt(Cg, Bg)
            dcb = jnp.zeros((L, L), F32)
            dCg = jnp.zeros((L, L), F32)
            dBg = jnp.zeros((L, L), F32)
            for jj in range(4):
                j = g * 4 + jj
                h0 = 2 * j
                sl = slice(j * L, (j + 1) * L)
                xs_p = xs_ref[:, sl]
                dY = dy_ref[:, sl]
                dYb = dY.astype(BF16)
                dtp = _pair(q["dt"], h0, q)
                X = xs_p * dtp
                e_p = _pair(q["eac"], h0, q)
                f_p = _pair(q["fac"], h0, q)
                cd_p = _pair(q["cd"], h0, q)
                Hp = st_ref[0, :, sl]
                Hpb = Hp.astype(BF16)
                dHn = dH[:, sl]
                dHnb = dHn.astype(BF16)
                dD_ref[:, sl] += jnp.sum(dY * xs_p, axis=0, keepdims=True)
                dxs = _pair(dk, h0, q) * dY
                Yo = _nn(Cg, Hpb) * e_p
                dZb = (dY * e_p).astype(BF16)
                s0, s1 = hsum(dY * Yo)
                dac = dac + put(h0, s0, s1)
                dCg = dCg + _nt(dZb, Hpb)
                dHp = _tn(Cg, dZb) + cd_p * dHn
                t = jnp.sum(dHn * Hp, axis=0, keepdims=True)
                d0 = jnp.sum(jnp.where(lo1, t, 0.0), axis=1, keepdims=True)
                d1 = jnp.sum(jnp.where(lo1, 0.0, t), axis=1, keepdims=True)
                dlast = dlast + put1(h0, d0, d1) * q["cd"]
                Xd = X * f_p
                dXd = _nn(Bg, dHnb)
                dBg = dBg + _nt(Xd.astype(BF16), dHnb)
                dX = dXd * f_p
                s0, s1 = hsum(dXd * Xd)
                dac = dac - put(h0, s0, s1)
                dlast = dlast + put1(h0, jnp.sum(s0, axis=0, keepdims=True), jnp.sum(s1, axis=0, keepdims=True))
                for hh, mask in ((h0, lo), (h0 + 1, jnp.logical_not(lo))):
                    dec = _decay(q, hh)
                    Gm = cb * dec
                    Xh = jnp.where(mask, X, 0.0).astype(BF16)
                    dG = _nt(dYb, Xh)
                    dcb = dcb + dG * dec
                    Q = dG * Gm
                    dac = dac + jnp.where(lane == hh, jnp.sum(Q, axis=1, keepdims=True), 0.0)
                    dacT = dacT - jnp.where(rowi == hh, jnp.sum(Q, axis=0, keepdims=True), 0.0)
                    dX = dX + jnp.where(mask, _tn(Gm.astype(BF16), dYb), 0.0)
                dxs = dxs + dX * dtp
                s0, s1 = hsum(dX * xs_p)
                ddt = ddt + put(h0, s0, s1)
                dxc_ref[:, sl] = dxs
                dH[:, sl] = dHp
            dcbb = dcb.astype(BF16)
            dxc_ref[:, 1024 + g * L:1024 + (g + 1) * L] = dBg + _tn(dcbb, Cg)
            dxc_ref[:, 1280 + g * L:1280 + (g + 1) * L] = dCg + _nn(dcbb, Bg)

        dac = dac + dacT.T + jnp.where(rowi == L - 1, dlast, 0.0)
        triu = (rowi <= lane).astype(F32)
        da = jnp.dot(triu, dac, precision=HIGHEST, preferred_element_type=F32)
        ddt = ddt + da * q["A"]
        dal_ref[...] += jnp.sum(da * q["dt"], axis=0, keepdims=True) * q["A"]
        ddtr = jnp.where(lane < SSD_HEADS, ddt * _sigmoid(q["dtr"]), 0.0)
        ddtb_ref[...] += jnp.sum(ddtr, axis=0, keepdims=True)
        dp_ref[...] = jnp.concatenate([ddtr, jnp.zeros_like(ddtr)], axis=1).astype(BF16)

    v = _vec((1, LANES))
    rev = lambda c: nc - 1 - c
    res, got = _pcall(
        body, (nc,),
        [pl.BlockSpec((L, 1024), lambda c: (rev(c), 0)),
         pl.BlockSpec((L, 256), lambda c: (rev(c), 4)),
         pl.BlockSpec((L, 256), lambda c: (rev(c), 5)),
         pl.BlockSpec((L, LANES), lambda c: (rev(c), DT_OFF // LANES)),
         pl.BlockSpec((1, SSD_STATE, 1024), lambda c: (rev(c), 0, 0)),
         pl.BlockSpec((L, 1024), lambda c: (rev(c), 0)), v, v, v, _any()],
        [pl.BlockSpec((L, XBC_W), lambda c: (rev(c), 0)),
         pl.BlockSpec((L, DT_W), lambda c: (rev(c), DT_OFF // DT_W)),
         v, v, _vec((1, 1024))],
        [_sds((S, XBC_W), F32), _sds(dP.shape, BF16), _sds((1, LANES), F32), _sds((1, LANES), F32),
         _sds((1, 1024), F32)], name, (xc, xc, xc, P, st, dy, dtb, alog, dsk, dP),
        scratch=[pltpu.VMEM((SSD_STATE, 1024), F32)], aliases={9: 1}, rider=rider)
    return (*res, got)


def ssd_post_fwd(y, P, nw, name):
    S = y.shape[0]
    T = ROW_TILE
    HW = 512

    def body(y_ref, z_ref, w_ref, o_ref):
        g = y_ref[...] * _silu(z_ref[...])
        for k in range(2):
            gk = g[:, k * HW:(k + 1) * HW]
            r = lax.rsqrt(jnp.mean(gk * gk, axis=-1, keepdims=True) + EPS)
            o_ref[:, k * HW:(k + 1) * HW] = (gk * r * w_ref[:, k * HW:(k + 1) * HW]).astype(BF16)

    return _rowcall(body, S, T, [_rows(T, 1024), _rows(T, 1024, Z_OFF // 1024), _vec((1, 1024))], _rows(T, 1024),
                    _sds((S, 1024), BF16), name)(y, P, nw)


def ssd_post_bwd(y, P, dout, dP, nw, name):
    S = y.shape[0]
    T = ROW_TILE
    HW = 512

    def body(y_ref, z_ref, d_ref, w_ref, dp_in, dy_ref, dz_ref, dw_ref):
        _zero_at_first(pl.program_id(0), dw_ref)
        yv = y_ref[...]
        zv = z_ref[...]
        sz = _silu(zv)
        g = yv * sz
        dv = d_ref[...]
        dgs = []
        for k in range(2):
            sl = slice(k * HW, (k + 1) * HW)
            gk = g[:, sl]
            r = lax.rsqrt(jnp.mean(gk * gk, axis=-1, keepdims=True) + EPS)
            gh = gk * r
            dk = dv[:, sl]
            dw_ref[:, sl] += jnp.sum(dk * gh, axis=0, keepdims=True)
            dgn = dk * w_ref[:, sl]
            dgs.append(r * (dgn - gh * jnp.mean(dgn * gh, axis=-1, keepdims=True)))
        dg = jnp.concatenate(dgs, axis=1)
        dy_ref[...] = dg * sz
        dz_ref[...] = (dg * yv * _dsilu(zv)).astype(BF16)

    zc = Z_OFF // 1024
    return _rowcall(body, S, T, [_rows(T, 1024), _rows(T, 1024, zc), _rows(T, 1024), _vec((1, 1024)), _any()],
                    [_rows(T, 1024), _rows(T, 1024, zc), _vec((1, 1024))],
                    [_sds((S, 1024), F32), _sds(dP.shape, BF16), _sds((1, 1024), F32)], name,
                    aliases={4: 1})(y, P, dout, nw, dP)


HEAD_W = 3 * ATTN_HEAD_DIM
HEAD_BLOCKS = NP // HEAD_W


def _slope(gi, h):
    c = [float(v) for v in SLOPES[gi]]
    return jnp.where(h == 0, c[0], jnp.where(h == 1, c[1], jnp.where(h == 2, c[2], c[3])))


def _rmsn(t):
    r = lax.rsqrt(jnp.mean(t * t, axis=-1, keepdims=True) + EPS)
    return t * r, r


def _rms_bwd(th, r, w, dn):
    dth = dn * w
    return r * (dth - th * jnp.mean(dth * th, axis=-1, keepdims=True))


ATT_ROWS = 2048


def _strided(r0, d):
    return pl.ds(r0, ATTN_STEPS, stride=d) if d > 1 else pl.ds(r0, ATTN_STEPS)


def attn_fwd(P, qw, kw, gi, name, rider=None):
    S = P.shape[0]
    d = DILATIONS[gi]
    B = ATTN_STEPS
    E = ATTN_HEAD_DIM
    BR = B * d
    nq = ATT_ROWS // BR
    nsb = S // ATT_ROWS
    scale = E ** -0.5

    def body(q_ref, k_ref, v_ref, kp_ref, vp_ref, qw_ref, kw_ref, o_ref, l_ref):
        h = pl.program_id(0)
        s = pl.program_id(1)
        slope = _slope(gi, h) * float(d)
        qwv = qw_ref[...]
        kwv = kw_ref[...]
        qi = _iota((B, B), 0)
        kj = _iota((B, B), 1)
        rel_c = (qi - kj).astype(F32)
        rel_p = rel_c + float(B)
        bias_c = jnp.where(qi >= kj, -slope * rel_c, NEG)
        bias_p0 = jnp.where(jnp.logical_and(qi <= kj, s > 0), -slope * rel_p, NEG)
        bias_p = jnp.where(qi <= kj, -slope * rel_p, NEG)
        for j in range(nq):
            for r in range(d):
                rows = _strided(j * BR + r, d)
                if j == 0:
                    kpv, vpv, bp = kp_ref[_strided(r, d), :], vp_ref[_strided(r, d), :], bias_p0
                else:
                    prow = _strided((j - 1) * BR + r, d)
                    kpv, vpv, bp = k_ref[prow, :], v_ref[prow, :], bias_p
                qh, _ = _rmsn(q_ref[rows, :])
                kh, _ = _rmsn(k_ref[rows, :])
                kph, _ = _rmsn(kpv)
                qn = (qh * qwv).astype(BF16)
                kn = (kh * kwv).astype(BF16)
                kpn = (kph * kwv).astype(BF16)
                s_c = _nt(qn, kn) * scale + bias_c
                s_p = _nt(qn, kpn) * scale + bp
                m = jnp.maximum(jnp.max(s_c, axis=1, keepdims=True), jnp.max(s_p, axis=1, keepdims=True))
                e_c = jnp.exp(s_c - m)
                e_p = jnp.exp(s_p - m)
                l = jnp.sum(e_c, axis=1, keepdims=True) + jnp.sum(e_p, axis=1, keepdims=True)
                lse = m + jnp.log(l)
                inv = 1.0 / l
                p_c = (e_c * inv).astype(BF16)
                p_p = (e_p * inv).astype(BF16)
                o_ref[rows, :] = _nn(p_c, v_ref[rows, :].astype(BF16)) + _nn(p_p, vpv.astype(BF16))
                l_ref[rows, :] = jnp.broadcast_to(lse, (B, E))

    def cur(i):
        return pl.BlockSpec((ATT_ROWS, E), lambda h, s: (s, (gi * 4 + h) * 3 + i))

    def prev(i):
        return pl.BlockSpec((BR, E), lambda h, s: (jnp.maximum(s * nq - 1, 0), (gi * 4 + h) * 3 + i))

    wv = pl.BlockSpec((1, E), lambda h, s: (0, 0))
    ob = pl.BlockSpec((ATT_ROWS, E), lambda h, s: (s, h))
    (o, l), got = _pcall(body, (4, nsb), [cur(0), cur(1), cur(2), prev(1), prev(2), wv, wv], [ob, ob],
                         [_sds((S, 512), F32), _sds((S, 512), F32)], name, (P, P, P, P, P, qw, kw), rider=rider)
    return o, l, got


def attn_bwd(P, o, lse, do, dlse, dP, qw, kw, gi, name):
    S = P.shape[0]
    d = DILATIONS[gi]
    B = ATTN_STEPS
    E = ATTN_HEAD_DIM
    BR = B * d
    nq = ATT_ROWS // BR
    nsb = S // ATT_ROWS
    nblk = S // BR
    scale = E ** -0.5

    def body(q_ref, k_ref, v_ref, kp_ref, vp_ref, qx_ref, o_ref, l_ref, do_ref, dl_ref, ox_ref, lx_ref, dox_ref, dlx_ref,
             qw_ref, kw_ref, dp_in, dp_ref, dqw_ref, dkw_ref, stage_q, stage_k, stage_v):
        h = pl.program_id(0)
        s = pl.program_id(1)

        @pl.when(jnp.logical_and(h == 0, s == 0))
        def _():
            dqw_ref[...] = jnp.zeros_like(dqw_ref)
            dkw_ref[...] = jnp.zeros_like(dkw_ref)

        slope = _slope(gi, h) * float(d)
        qwv = qw_ref[...]
        kwv = kw_ref[...]
        qi = _iota((B, B), 0)
        kj = _iota((B, B), 1)
        rel_c = (qi - kj).astype(F32)
        rel_p = rel_c + float(B)
        band = qi <= kj
        bias_c = jnp.where(qi >= kj, -slope * rel_c, NEG)
        bias_p = jnp.where(band, -slope * rel_p, NEG)
        bias_p_first = jnp.where(jnp.logical_and(band, s > 0), -slope * rel_p, NEG)
        bias_p_last = jnp.where(jnp.logical_and(band, s < nsb - 1), -slope * rel_p, NEG)
        dqw = jnp.zeros((1, E), F32)
        dkw = jnp.zeros((1, E), F32)
        for j in range(nq):
            for r in range(d):
                rows = _strided(j * BR + r, d)
                if j == 0:
                    kpv, vpv, bB = kp_ref[_strided(r, d), :], vp_ref[_strided(r, d), :], bias_p_first
                else:
                    prow = _strided((j - 1) * BR + r, d)
                    kpv, vpv, bB = k_ref[prow, :], v_ref[prow, :], bias_p
                if j == nq - 1:
                    xr = _strided(r, d)
                    qxv, oxv, lxv, doxv, dlxv, bC = (qx_ref[xr, :], ox_ref[xr, :], lx_ref[xr, :], dox_ref[xr, :],
                                                     dlx_ref[xr, :], bias_p_last)
                else:
                    xr = _strided((j + 1) * BR + r, d)
                    qxv, oxv, lxv, doxv, dlxv, bC = (q_ref[xr, :], o_ref[xr, :], l_ref[xr, :], do_ref[xr, :],
                                                     dl_ref[xr, :], bias_p)
                qh, rq = _rmsn(q_ref[rows, :])
                kh, rk = _rmsn(k_ref[rows, :])
                kph, _ = _rmsn(kpv)
                qxh, _ = _rmsn(qxv)
                qn = (qh * qwv).astype(BF16)
                kn = (kh * kwv).astype(BF16)
                kpn = (kph * kwv).astype(BF16)
                qxn = (qxh * qwv).astype(BF16)
                v = v_ref[rows, :].astype(BF16)
                vp = vpv.astype(BF16)
                lse_c = l_ref[rows, :]
                pA = jnp.exp(_nt(qn, kn) * scale + bias_c - lse_c)
                pB = jnp.exp(_nt(qn, kpn) * scale + bB - lse_c)
                pC = jnp.exp(_nt(qxn, kn) * scale + bC - lxv)
                doc = do_ref[rows, :]
                docb = doc.astype(BF16)
                doxb = doxv.astype(BF16)
                corr_c = jnp.sum(dl_ref[rows, :] - doc * o_ref[rows, :], axis=1, keepdims=True)
                corr_n = jnp.sum(dlxv - doxv * oxv, axis=1, keepdims=True)
                dsA = (pA * (_nt(docb, v) + corr_c)).astype(BF16)
                dsB = (pB * (_nt(docb, vp) + corr_c)).astype(BF16)
                dsC = (pC * (_nt(doxb, v) + corr_n)).astype(BF16)
                dqn = (_nn(dsA, kn) + _nn(dsB, kpn)) * scale
                dkn = (_tn(dsA, qn) + _tn(dsC, qxn)) * scale
                dv = _tn(pA.astype(BF16), docb) + _tn(pC.astype(BF16), doxb)
                dqw = dqw + jnp.sum(dqn * qh, axis=0, keepdims=True)
                dkw = dkw + jnp.sum(dkn * kh, axis=0, keepdims=True)
                stage_q[rows, :] = _rms_bwd(qh, rq, qwv, dqn)
                stage_k[rows, :] = _rms_bwd(kh, rk, kwv, dkn)
                stage_v[rows, :] = dv
        dqw_ref[...] += dqw
        dkw_ref[...] += dkw
        dp_ref[:, 0:E] = stage_q[...].astype(BF16)
        dp_ref[:, E:2 * E] = stage_k[...].astype(BF16)
        dp_ref[:, 2 * E:3 * E] = stage_v[...].astype(BF16)

    def cur(i):
        return pl.BlockSpec((ATT_ROWS, E), lambda h, s: (s, (gi * 4 + h) * 3 + i))

    def prev(i):
        return pl.BlockSpec((BR, E), lambda h, s: (jnp.maximum(s * nq - 1, 0), (gi * 4 + h) * 3 + i))

    nxt_q = pl.BlockSpec((BR, E), lambda h, s: (jnp.minimum((s + 1) * nq, nblk - 1), (gi * 4 + h) * 3))
    ocur = pl.BlockSpec((ATT_ROWS, E), lambda h, s: (s, h))
    onxt = pl.BlockSpec((BR, E), lambda h, s: (jnp.minimum((s + 1) * nq, nblk - 1), h))
    wv = pl.BlockSpec((1, E), lambda h, s: (0, 0))
    return pl.pallas_call(
        body, grid=(4, nsb),
        in_specs=[cur(0), cur(1), cur(2), prev(1), prev(2), nxt_q, ocur, ocur, ocur, ocur, onxt, onxt, onxt, onxt, wv, wv,
                  _any()],
        out_specs=[pl.BlockSpec((ATT_ROWS, HEAD_W), lambda h, s: (s, gi * 4 + h)), wv, wv],
        out_shape=[_sds(dP.shape, BF16), _sds((1, E), F32), _sds((1, E), F32)],
        scratch_shapes=[pltpu.VMEM((ATT_ROWS, E), F32)] * 3,
        input_output_aliases={16: 0}, name=name,
        compiler_params=_cp(("arbitrary", "arbitrary")))(
            P, P, P, P, P, P, o, lse, do, dlse, o, lse, do, dlse, qw, kw, dP)


def attn_merge_fwd(os_, ls_, name):
    S = os_[0].shape[0]
    T = ROW_TILE

    def body(o0, o1, o2, l0, l1, l2, out):
        a, b, c = l0[...], l1[...], l2[...]
        m = jnp.maximum(jnp.maximum(a, b), c)
        ea, eb, ec = jnp.exp(a - m), jnp.exp(b - m), jnp.exp(c - m)
        inv = 1.0 / (ea + eb + ec)
        out[...] = ((ea * o0[...] + eb * o1[...] + ec * o2[...]) * inv).astype(BF16)

    return _rowcall(body, S, T, [_rows(T, 512)] * 6, _rows(T, 512), _sds((S, 512), BF16), name)(*os_, *ls_)


def attn_merge_bwd(os_, ls_, dom, name):
    S = os_[0].shape[0]
    T = ROW_TILE

    def body(o0, o1, o2, l0, l1, l2, d_ref, do0, do1, do2, dl0, dl1, dl2):
        a, b, c = l0[...], l1[...], l2[...]
        m = jnp.maximum(jnp.maximum(a, b), c)
        ea, eb, ec = jnp.exp(a - m), jnp.exp(b - m), jnp.exp(c - m)
        inv = 1.0 / (ea + eb + ec)
        wa, wb, wc = ea * inv, eb * inv, ec * inv
        dv = d_ref[...]
        do0[...] = wa * dv
        do1[...] = wb * dv
        do2[...] = wc * dv
        ga, gb, gc = dv * o0[...], dv * o1[...], dv * o2[...]
        tot = wa * ga + wb * gb + wc * gc
        dl0[...] = wa * (ga - tot)
        dl1[...] = wb * (gb - tot)
        dl2[...] = wc * (gc - tot)

    sp = _rows(T, 512)
    return _rowcall(body, S, T, [sp] * 7, [sp] * 6, [_sds((S, 512), F32)] * 6, name)(*os_, *ls_, dom)


def _counts(i, T, rows, w, offset=0):
    t = i * T + offset + _iota((rows, 1), 0)
    return jnp.minimum(t + 1, w).astype(F32)


def pool_fwd(P, wmix, scale, name):
    S = P.shape[0]
    T = ROW_TILE
    GW = 256
    uc = U_OFF // 1024
    hb = T // POOL_HALO

    def body(u_ref, h_ref, wm_ref, sc_ref, pl_ref, yp_ref):
        i = pl.program_id(0)
        u = u_ref[...]
        ext = jnp.concatenate([jnp.where(i > 0, h_ref[...], 0.0), u], axis=0)
        for g, w in enumerate(POOL_WINDOWS):
            sl = slice(g * GW, (g + 1) * GW)
            s = ext[:, sl]
            k = 1
            while k < w:
                s = s + pltpu.roll(s, k, 0)
                k *= 2
            pooled = (s[POOL_HALO:] / _counts(i, T, T, w) - u[:, sl]).astype(BF16)
            pl_ref[:, sl] = pooled
            yp_ref[:, sl] = (_nn(pooled, wm_ref[g]) * sc_ref[:, sl]).astype(BF16)

    return _rowcall(
        body, S, T,
        [_rows(T, 1024, uc), pl.BlockSpec((POOL_HALO, 1024), lambda i: (jnp.maximum(i * hb - 1, 0), uc)),
         _vec((4, GW, GW)), _vec((1, 1024))],
        [_rows(T, 1024), _rows(T, 1024)], [_sds((S, 1024), BF16), _sds((S, 1024), BF16)], name)(P, P, wmix, scale)


def pool_bwd(dyp, pooled, dP, wmix, scale, name):
    S = dyp.shape[0]
    T = ROW_TILE
    nt = S // T
    GW = 256
    uc = U_OFF // 1024
    hb = T // POOL_HALO
    TE = T + POOL_HALO

    def body(d_ref, dn_ref, p_ref, wm_ref, sc_ref, dp_in, du_ref, dwm_ref, dsc_ref):
        i = pl.program_id(0)
        _zero_at_first(i, dwm_ref, dsc_ref)
        dv = d_ref[...]
        dn = jnp.where(i < nt - 1, dn_ref[...], 0.0)
        for g, w in enumerate(POOL_WINDOWS):
            sl = slice(g * GW, (g + 1) * GW)
            pg = p_ref[:, sl]
            wm = wm_ref[g]
            dsc_ref[:, sl] += jnp.sum(dv[:, sl] * _nn(pg, wm), axis=0, keepdims=True)
            dmb = (dv[:, sl] * sc_ref[:, sl]).astype(BF16)
            dmnb = (dn[:, sl] * sc_ref[:, sl]).astype(BF16)
            dwm_ref[g] += _tn(pg, dmb)
            dpl = _nt(dmb, wm)
            dpln = _nt(dmnb, wm)
            s = jnp.concatenate([dpl / _counts(i, T, T, w), dpln / _counts(i, T, POOL_HALO, w, T)], axis=0)
            k = 1
            while k < w:
                s = s + pltpu.roll(s, TE - k, 0)
                k *= 2
            du_ref[:, sl] = (s[:T] - dpl).astype(BF16)

    return _rowcall(
        body, S, T,
        [_rows(T, 1024), pl.BlockSpec((POOL_HALO, 1024), lambda i: (jnp.minimum((i + 1) * hb, S // POOL_HALO - 1), 0)),
         _rows(T, 1024), _vec((4, GW, GW)), _vec((1, 1024)), _any()],
        [_rows(T, 1024, uc), _vec((4, GW, GW)), _vec((1, 1024))],
        [_sds(dP.shape, BF16), _sds((4, GW, GW), F32), _sds((1, 1024), F32)], name,
        aliases={5: 0})(dyp, dyp, pooled, wmix, scale, dP)


def gate_fwd(P, ys, ya, yp, name):
    S = ys.shape[0]
    T = ROW_TILE
    gc = GATE_OFF // 1024

    def body(g0, g1, g2, a, b, c, out):
        out[...] = (_sigmoid(g0[...]) * a[...] + _sigmoid(g1[...]) * b[...] + _sigmoid(g2[...]) * c[...]).astype(BF16)

    r = _rows(T, 1024)
    return _rowcall(body, S, T, [_rows(T, 1024, gc), _rows(T, 1024, gc + 1), _rows(T, 1024, gc + 2), r, r, r], r,
                    _sds((S, 1024), BF16), name)(P, P, P, ys, ya, yp)


def gate_bwd(P, ys, ya, yp, dm, dP, name):
    S = ys.shape[0]
    T = ROW_TILE
    gc = GATE_OFF // 1024

    def body(g0, g1, g2, a, b, c, d_ref, dp_in, dg_ref, da, db, dc):
        dv = d_ref[...]
        outs = []
        for gr, yr, dyr in ((g0, a, da), (g1, b, db), (g2, c, dc)):
            s = _sigmoid(gr[...])
            dyr[...] = (s * dv).astype(BF16)
            outs.append(dv * yr[...] * s * (1.0 - s))
        dg_ref[...] = jnp.concatenate(outs, axis=1).astype(BF16)

    r = _rows(T, 1024)
    return _rowcall(body, S, T, [_rows(T, 1024, gc), _rows(T, 1024, gc + 1), _rows(T, 1024, gc + 2), r, r, r, r, _any()],
                    [_rows(T, GATE_W, GATE_OFF // GATE_W), r, r, r],
                    [_sds(dP.shape, BF16)] + [_sds((S, 1024), BF16)] * 3, name,
                    aliases={7: 0})(P, P, P, ys, ya, yp, dm, dP)


def resid_bwd(dx, g, y, name):
    S, D = dx.shape
    T = ROW_TILE

    def body(d_ref, g_ref, y_ref, dy_ref, dg_ref):
        _zero_at_first(pl.program_id(0), dg_ref)
        dv = d_ref[...]
        dy_ref[...] = (g_ref[...] * dv).astype(BF16)
        dg_ref[...] += jnp.sum(dv * y_ref[...], axis=0, keepdims=True)

    r = _rows(T, D)
    return _rowcall(body, S, T, [r, _vec((1, D)), r], [r, _vec((1, D))], [_sds((S, D), BF16), _sds((1, D), F32)],
                    name)(dx, g, y)


def loss_fwd_bwd(y, target, name):
    S, D = y.shape
    T = ROW_TILE

    def body(y_ref, t_ref, acc_ref, dy_ref):
        _zero_at_first(pl.program_id(0), acc_ref)
        e = y_ref[...] - t_ref[...]
        acc_ref[...] += jnp.sum(e * e, axis=0, keepdims=True)
        dy_ref[...] = e * (1.0 / D)

    r = _rows(T, D)
    return _rowcall(body, S, T, [r, r], [_vec((1, D)), r], [_sds((1, D), F32), _sds((S, D), F32)], name)(y, target)


def silu_rows(c, name):
    def body(c_ref, o_ref):
        o_ref[...] = _silu(c_ref[...])

    return pl.pallas_call(body, out_shape=_sds(c.shape, F32), name=name)(c)


def adamw(parts, w, m, v, name, row0=0, prev=None):
    n, R, C = parts.shape
    Rt = w.shape[0]
    tr = R
    while tr * C * 4 > (1 << 20) and tr % 16 == 0:
        tr //= 2
    b0 = row0 // tr
    c1 = 1.0 / (1.0 - ADAM_B1 ** ADAM_STEP)
    c2 = 1.0 / (1.0 - ADAM_B2 ** ADAM_STEP)

    def body(p_ref, w_ref, m_ref, v_ref, *rest):
        g_ref, d_ref, nm_ref, nv_ref = rest[-4:]
        g = p_ref[0].astype(F32)
        for k in range(1, n):
            g = g + p_ref[k].astype(F32)
        nm = ADAM_B1 * m_ref[...] + (1.0 - ADAM_B1) * g
        nv = ADAM_B2 * v_ref[...] + (1.0 - ADAM_B2) * (g * g)
        g_ref[...] = g
        nm_ref[...] = nm
        nv_ref[...] = nv
        d_ref[...] = -ADAM_LR * ((nm * c1) / (jnp.sqrt(nv * c2) + ADAM_EPS) + ADAM_WD * w_ref[...])

    r = pl.BlockSpec((tr, C), lambda i: (b0 + i, 0))
    ins = [parts, w, m, v] + (list(prev) if prev is not None else [])
    in_specs = [pl.BlockSpec((n, tr, C), lambda i: (0, i, 0)), r, r, r] + ([_any()] * 4 if prev is not None else [])
    aliases = {4 + k: k for k in range(4)} if prev is not None else None
    return _rowcall(body, R, tr, in_specs, [r, r, r, r], [_sds((Rt, C), F32)] * 4, name, aliases=aliases)(*ins)


FWD_HOSTS = {"in_proj": ("w_in",), "attn_fwd0": ("w_ff1",), "attn_fwd1": ("w_ff2",),
             "attn_fwd2": ("w_ssd_out", "w_attn_out", "w_pool_mix", "w_pool_out", "w_out")}
W_IN_PARTS = 4
BWD_HOSTS = {"ssd_bwd": ("w_in_0", "w_in_1", "w_in_2"), "d_h2": ("w_in_3",),
             "d_w_in": ("w_ff1", "w_out", "w_ssd_out"), "d_h1": ("w_ff2", "w_attn_out", "w_pool_mix", "w_pool_out")}


def _cargo_rider(kind, hosts, host, cargo):
    if not cargo or hosts[host][0] not in cargo:
        return None
    return Rider(kind, [cargo[k] for k in hosts[host]])


def _ride(mmargs, kind, hosts, host, cargo, got, **kw):
    rider = _cargo_rider(kind, hosts, host, cargo) if host in hosts else None
    if rider is None:
        return mm(*mmargs, **kw)
    out, res = mm(*mmargs, rider=rider, **kw)
    got.update(zip(hosts[host], res))
    return out


def _epi_relu2(r):
    p = jnp.maximum(r, 0.0)
    return r, p * p


def _epi_relu2_bwd(r, f):
    return (2.0 * jnp.maximum(f, 0.0) * r,)


def _epi_resid(r, x, g):
    return r, x + g * r


def layer_fwd(x, mod, W, sm, tag, cargo=None):
    sh1, sc1, g1, sh2, sc2, g2 = mod
    sv, got = {}, {}
    ride = functools.partial(_ride, kind="gather", hosts=FWD_HOSTS, cargo=cargo, got=got)
    h1 = rms_mod_fwd(x, sm["norm1_w"], sc1, sh1, f"rms1_fwd_{tag}")
    P = ride((h1, W["in"], "nn", F32, f"in_proj_{tag}"), host="in_proj")
    pre, xc = conv_fwd(P, sm["conv_w"], sm["conv_b"], f"conv_fwd_{tag}")
    y, st, _ = ssd_fwd(xc, P, sm["dt_bias"], sm["a_log"], sm["d_skip"], f"ssd_fwd_{tag}")
    ysn = ssd_post_fwd(y, P, sm["ssd_norm_w"], f"ssd_post_fwd_{tag}")
    ys = mm(ysn, W["ssd_out"], "nn", F32, f"ssd_out_{tag}")
    os_, ls_ = [], []
    for gi in range(3):
        host = f"attn_fwd{gi}"
        o, l, res = attn_fwd(P, sm["q_norm_w"], sm["k_norm_w"], gi, f"attn_fwd{gi}_{tag}",
                             rider=_cargo_rider("gather", FWD_HOSTS, host, cargo))
        got.update(zip(FWD_HOSTS[host], res))
        os_.append(o)
        ls_.append(l)
    om = attn_merge_fwd(os_, ls_, f"attn_merge_fwd_{tag}")
    ya = mm(om, W["attn_out"], "nn", F32, f"attn_out_{tag}")
    pooled, ypp = pool_fwd(P, W["pool_mix"], sm["pool_scale"], f"pool_fwd_{tag}")
    yp = mm(ypp, W["pool_out"], "nn", F32, f"pool_out_{tag}")
    m = gate_fwd(P, ys, ya, yp, f"gate_fwd_{tag}")
    mo, x1 = mm(m, W["out"], "nn", F32, f"mix_out_{tag}", extra=(x, g1), epi=_epi_resid, out_dtypes=(F32, F32))
    h2 = rms_mod_fwd(x1, sm["norm2_w"], sc2, sh2, f"rms2_fwd_{tag}")
    f, a = ride((h2, W["ff1"], "nn", F32, f"ff1_{tag}"), host="ff1", epi=_epi_relu2, out_dtypes=(F32, BF16),
                col_blocks=True)
    o2, x2 = ride((a, W["ff2"], "nn", F32, f"ff2_{tag}"), host="ff2", extra=(x1, g2), epi=_epi_resid,
                  out_dtypes=(F32, F32))
    sv.update(x=x, h1=h1, P=P, pre=pre, xc=xc, y=y, st=st, ysn=ysn, ys=ys, os=os_, ls=ls_, om=om, ya=ya,
              pooled=pooled, ypp=ypp, yp=yp, m=m, mo=mo, x1=x1, h2=h2, f=f, a=a, o2=o2)
    return x2, sv, got


def layer_bwd(dx2, sv, mod, W, sm, tag, slabs, cargo=None):
    sh1, sc1, g1, sh2, sc2, g2 = mod
    S = dx2.shape[0]
    P = sv["P"]
    gw, gs, got, own, got_own = {}, {}, {}, {}, {}
    ride = functools.partial(_ride, kind="scatter", hosts=BWD_HOSTS, cargo=cargo, got=got)
    ride_own = functools.partial(_ride, kind="scatter", hosts=BWD_HOSTS, cargo=own, got=got_own)
    do2, dg2 = resid_bwd(dx2, g2, sv["o2"], f"resid2_bwd_{tag}")
    gw["ff2"] = ride((sv["a"], do2, "tn", BF16, f"d_w_ff2_{tag}"), host="d_w_ff2")
    df = ride((do2, W["ff2"], "nt", BF16, f"d_f_{tag}"), host="d_a", extra=(sv["f"],), epi=_epi_relu2_bwd)
    gw["ff1"] = ride((sv["h2"], df, "tn", BF16, f"d_w_ff1_{tag}"), host="d_w_ff1", col_blocks=True)
    dh2 = ride((df, W["ff1"], "nt", F32, f"d_h2_{tag}"), host="d_h2", col_blocks=True)
    dx1, gs["norm2_w"], dsc2, dsh2 = rms_mod_bwd(sv["x1"], dh2, dx2, sm["norm2_w"], sc2, f"rms2_bwd_{tag}")
    dmo, dg1 = resid_bwd(dx1, g1, sv["mo"], f"resid1_bwd_{tag}")
    gw["out"] = mm(sv["m"], dmo, "tn", BF16, f"d_w_out_{tag}")
    dm = mm(dmo, W["out"], "nt", F32, f"d_m_{tag}")
    dP = lax.empty((S, NP), BF16)
    dP, dys, dya, dyp = gate_bwd(P, sv["ys"], sv["ya"], sv["yp"], dm, dP, f"gate_bwd_{tag}")
    gw["ssd_out"] = mm(sv["ysn"], dys, "tn", BF16, f"d_w_ssd_out_{tag}")
    dysn = mm(dys, W["ssd_out"], "nt", F32, f"d_ysn_{tag}")
    dy, dP, gs["ssd_norm_w"] = ssd_post_bwd(sv["y"], P, dysn, dP, sm["ssd_norm_w"], f"ssd_post_bwd_{tag}")
    dxc, dP, gs["dt_bias"], gs["a_log"], dDl, res = ssd_bwd(
        sv["xc"], P, sv["st"], dy, dP, sm["dt_bias"], sm["a_log"], sm["d_skip"], f"ssd_bwd_{tag}",
        rider=_cargo_rider("scatter", BWD_HOSTS, "ssd_bwd", cargo))
    got.update(zip(BWD_HOSTS["ssd_bwd"], res))
    gs["d_skip"] = dDl.reshape(SSD_HEADS, SSD_HEAD_DIM).sum(axis=1)
    gs["dt_bias"] = gs["dt_bias"][0, :SSD_HEADS]
    gs["a_log"] = gs["a_log"][0, :SSD_HEADS]
    dP, gs["conv_w"], gs["conv_b"] = conv_bwd(dxc, sv["pre"], P, dP, sm["conv_w"], f"conv_bwd_{tag}")
    gw["attn_out"] = mm(sv["om"], dya, "tn", BF16, f"d_w_attn_out_{tag}")
    dom = mm(dya, W["attn_out"], "nt", F32, f"d_om_{tag}")
    mb = attn_merge_bwd(sv["os"], sv["ls"], dom, f"attn_merge_bwd_{tag}")
    dqw = dkw = None
    for gi in range(3):
        dP, a_, b_ = attn_bwd(P, sv["os"][gi], sv["ls"][gi], mb[gi], mb[3 + gi], dP, sm["q_norm_w"], sm["k_norm_w"], gi,
                              f"attn_bwd{gi}_{tag}")
        dqw = a_ if dqw is None else dqw + a_
        dkw = b_ if dkw is None else dkw + b_
    gs["q_norm_w"], gs["k_norm_w"] = dqw, dkw
    gw["pool_out"] = mm(sv["ypp"], dyp, "tn", BF16, f"d_w_pool_out_{tag}")
    dypp = mm(dyp, W["pool_out"], "nt", F32, f"d_ypp_{tag}")
    dP, dwm, gs["pool_scale"] = pool_bwd(dypp, sv["pooled"], dP, W["pool_mix"], sm["pool_scale"], f"pool_bwd_{tag}")
    gw["pool_mix"] = dwm.astype(BF16)
    own.update(slabs(gw))
    gw["in"] = ride_own((sv["h1"], dP, "tn", BF16, f"d_w_in_{tag}"), host="d_w_in")
    dh1 = ride_own((dP, W["in"], "nt", F32, f"d_h1_{tag}"), host="d_h1")
    dx, gs["norm1_w"], dsc1, dsh1 = rms_mod_bwd(sv["x"], dh1, dx1, sm["norm1_w"], sc1, f"rms1_bwd_{tag}")
    dmod = jnp.concatenate([dsh1, dsc1, dg1, dsh2, dsc2, dg2], axis=1)
    return dx, dmod, gs, gw, slabs({"in": gw["in"]}), got, got_own


def _w_in_to_fused(w):
    E = ATTN_HEAD_DIM
    cols = []
    for h in range(ATTN_HEADS):
        for off in (O_Q, O_K, O_V):
            cols.append(w[:, off + h * E: off + (h + 1) * E])
    cols += [w[:, O_XBC:O_DT], w[:, O_G:IN_WIDTH], w[:, O_Z:O_XBC], w[:, O_U:O_G], w[:, O_DT:O_Q],
             jnp.zeros((w.shape[0], DT_W - SSD_HEADS), w.dtype)]
    return jnp.concatenate(cols, axis=1)


def _fused_to_w_in(g):
    E = ATTN_HEAD_DIM
    q = [g[:, h * HEAD_W: h * HEAD_W + E] for h in range(ATTN_HEADS)]
    k = [g[:, h * HEAD_W + E: h * HEAD_W + 2 * E] for h in range(ATTN_HEADS)]
    v = [g[:, h * HEAD_W + 2 * E: (h + 1) * HEAD_W] for h in range(ATTN_HEADS)]
    return jnp.concatenate([g[:, Z_OFF:Z_OFF + Z_W], g[:, XBC_OFF:XBC_OFF + XBC_W], g[:, DT_OFF:DT_OFF + SSD_HEADS]]
                           + q + k + v + [g[:, U_OFF:U_OFF + U_W], g[:, GATE_OFF:GATE_OFF + GATE_W]], axis=1)


def _cols_from_shards(g):
    return jnp.moveaxis(g, 0, 1).reshape(g.shape[1], N_DEV * g.shape[2])


def _cols_to_shards(w):
    R = w.shape[0]
    return jnp.moveaxis(w.reshape(R, N_DEV, w.shape[1] // N_DEV), 1, 0)


BIG = ("w_in", "w_ssd_out", "w_attn_out", "w_pool_mix", "w_pool_out", "w_out", "w_ff1", "w_ff2")
SMALL = ("b_ada", "norm1_w", "norm2_w", "conv_b", "dt_bias", "a_log", "d_skip", "ssd_norm_w", "q_norm_w", "k_norm_w",
         "pool_scale")
WEIGHTS = ("w_ada", "b_ada", "norm1_w", "norm2_w", "w_in", "conv_w", "conv_b", "dt_bias", "a_log", "d_skip",
           "ssd_norm_w", "w_ssd_out", "q_norm_w", "k_norm_w", "w_attn_out", "w_pool_mix", "pool_scale", "w_pool_out",
           "w_out", "w_ff1", "w_ff2")


def _pack(arrs):
    flat = jnp.concatenate([a.reshape(-1).astype(F32) for a in arrs])
    pad = (-flat.shape[0]) % (8 * LANES)
    return jnp.pad(flat, (0, pad)).reshape(-1, LANES)


def _unpack(packed, shapes):
    flat = packed.reshape(-1)
    out, off = [], 0
    for s in shapes:
        n = int(np.prod(s))
        out.append(flat[off:off + n].reshape(s))
        off += n
    return out


def _pad_lanes(v):
    return jnp.pad(v, ((0, 0), (0, LANES - v.shape[1])))[:, None, :]


def _step(x, c, target, p, m_, v_):
    Ld = p["w_in"].shape[0]
    me = 4 * lax.axis_index("x") + 2 * lax.axis_index("y") + lax.axis_index("c")

    local_bf16 = {k: p[k].astype(BF16) for k in BIG}

    def local_blocks(l):
        return {k: (local_bf16[k], l) for k in BIG}

    blk = local_blocks(0)
    gathered = all_gather([blk[k] for k in BIG] + [c, p["conv_w"]], "gather_first")
    G = dict(zip(BIG, gathered[:len(BIG)]))
    c_all = gathered[len(BIG)].reshape(N_DEV, D_MODEL)
    conv_w_full = jnp.moveaxis(gathered[len(BIG) + 1], 0, 2).reshape(Ld, 4, XBC_W)

    def layer_weights(G):
        W = {}
        W["in"] = _w_in_to_fused(_cols_from_shards(G["w_in"]))
        W["ssd_out"] = G["w_ssd_out"].reshape(1024, D_MODEL)
        W["attn_out"] = _cols_from_shards(G["w_attn_out"])
        W["pool_mix"] = jnp.moveaxis(G["w_pool_mix"], 0, 1).reshape(4, 256, 256)
        W["pool_out"] = G["w_pool_out"].reshape(1024, D_MODEL)
        W["out"] = G["w_out"].reshape(D_MODEL, D_MODEL)
        W["ff1"] = G["w_ff1"]
        W["ff2"] = G["w_ff2"].reshape(4 * D_MODEL, D_MODEL)
        return W

    def slabs(gw):
        out = {}
        if "in" in gw:
            full = _cols_to_shards(_fused_to_w_in(gw["in"]))
            rows = full.shape[1] // W_IN_PARTS
            for q in range(W_IN_PARTS):
                out[f"w_in_{q}"] = full[:, q * rows:(q + 1) * rows]
            return out
        out["w_ssd_out"] = gw["ssd_out"].reshape(N_DEV, -1, D_MODEL)
        out["w_attn_out"] = _cols_to_shards(gw["attn_out"])
        out["w_pool_mix"] = jnp.moveaxis(gw["pool_mix"].reshape(4, N_DEV, 32, 256), 1, 0)
        out["w_pool_out"] = gw["pool_out"].reshape(N_DEV, -1, D_MODEL)
        out["w_out"] = gw["out"].reshape(N_DEV, -1, D_MODEL)
        out["w_ff1"] = gw["ff1"]
        out["w_ff2"] = gw["ff2"].reshape(N_DEV, -1, D_MODEL)
        return out

    def layer_small(l):
        sm = {k: p[k][l][None, :] for k in ("norm1_w", "norm2_w", "conv_b", "ssd_norm_w", "q_norm_w", "k_norm_w",
                                            "pool_scale")}
        for k in ("dt_bias", "a_log", "d_skip"):
            sm[k] = jnp.pad(p[k][l], (0, LANES - SSD_HEADS))[None, :]
        sm["conv_w"] = conv_w_full[l]
        return sm

    cond = silu_rows(c_all, "cond_silu")
    ncol = p["w_ada"].shape[2]
    b_mine = lax.dynamic_slice_in_dim(p["b_ada"], me * ncol, ncol, axis=1)
    parts = [mm(cond, p["w_ada"][l], "nn", F32, f"ada_{l}") + b_mine[l][None, :] for l in range(Ld)]
    mod_cols = jnp.stack(parts, axis=1)
    mod = all_to_all([mod_cols], "scatter_mod")[0]
    mod = jnp.moveaxis(mod, 0, 1).reshape(Ld, 6 * D_MODEL)

    def mods(l):
        return [mod[l, k * D_MODEL:(k + 1) * D_MODEL][None, :] for k in range(6)]

    saved, Ws, sms = [], [], []
    h = x
    for l in range(Ld):
        Ws.append(layer_weights(G))
        sms.append(layer_small(l))
        h, sv, G = layer_fwd(h, mods(l), Ws[l], sms[l], f"l{l}", local_blocks(l + 1) if l + 1 < Ld else None)
        saved.append(sv)
    lsum, dh = loss_fwd_bwd(h, target, "loss")
    loss = 0.5 / D_MODEL * jnp.sum(lsum)
    gss, dmods, recvs = [None] * Ld, [None] * Ld, [dict() for _ in range(Ld)]
    cargo = None
    for l in reversed(range(Ld)):
        dh, dmods[l], gss[l], _, nxt, got, got_own = layer_bwd(dh, saved[l], mods(l), Ws[l], sms[l], f"l{l}", slabs, cargo)
        recvs[l].update(got_own)
        if cargo is not None:
            recvs[l + 1].update(got)
        cargo = nxt
    dmod = jnp.concatenate(dmods, axis=0)
    dmod_cols = jnp.moveaxis(dmod.reshape(Ld, N_DEV, ncol), 1, 0)
    parts_in = [f"w_in_{q}" for q in range(W_IN_PARTS)]
    last = all_to_all([cargo[k] for k in parts_in] + [dmod_cols], "scatter_last")
    recvs[0].update(zip(parts_in, last[:W_IN_PARTS]))
    dmod_all = last[W_IN_PARTS]

    out_g, out_d, out_m, out_v = {}, {}, {}, {}

    def finish(k, res):
        shp = p[k].shape
        out_g[k], out_d[k], out_m[k], out_v[k] = [t.reshape(shp) for t in res]

    for k in BIG:
        C = p[k].shape[-1]
        rows = int(np.prod(p[k].shape[1:-1]))
        res = None
        for l in reversed(range(Ld)):
            pieces = [(f"{k}_{q}", rows // W_IN_PARTS) for q in range(W_IN_PARTS)] if k == "w_in" else [(k, rows)]
            for q, (key, nrows) in enumerate(pieces):
                res = adamw(recvs[l][key].reshape(N_DEV, nrows, C), p[k].reshape(-1, C), m_[k].reshape(-1, C),
                            v_[k].reshape(-1, C), f"adamw_{key}_l{l}", row0=l * rows + q * nrows, prev=res)
        finish(k, res)
    g_ada = jnp.stack([mm(cond, dmod_all[:, l], "tn", F32, f"d_w_ada_{l}") for l in range(Ld)], axis=0)
    finish("w_ada", adamw(g_ada.reshape(1, -1, ncol), p["w_ada"].reshape(-1, ncol), m_["w_ada"].reshape(-1, ncol),
                          v_["w_ada"].reshape(-1, ncol), "adamw_w_ada"))

    small_g = {k: jnp.stack([gss[l][k].reshape(p[k].shape[1:]) for l in range(Ld)], axis=0) for k in SMALL if k != "b_ada"}
    small_g["b_ada"] = dmod
    conv_g = jnp.stack([gss[l]["conv_w"] for l in range(Ld)], axis=0)
    packed = _pack([small_g[k] for k in SMALL] + [conv_g])
    parts_small = all_gather([packed], "gather_small_grads")[0]
    zeros_conv = jnp.zeros(conv_g.shape, F32)
    res = adamw(parts_small, _pack([p[k] for k in SMALL] + [zeros_conv]), _pack([m_[k] for k in SMALL] + [zeros_conv]),
                _pack([v_[k] for k in SMALL] + [zeros_conv]), "adamw_small")
    shapes = [p[k].shape for k in SMALL] + [conv_g.shape]
    unp = [_unpack(t, shapes) for t in res]
    for i, k in enumerate(SMALL):
        out_g[k], out_d[k], out_m[k], out_v[k] = unp[0][i], unp[1][i], unp[2][i], unp[3][i]
    cw = p["conv_w"].shape[-1]
    conv_mine = lax.dynamic_slice_in_dim(unp[0][len(SMALL)], me * cw, cw, axis=2)
    finish("conv_w", adamw(conv_mine.reshape(1, -1, cw), p["conv_w"].reshape(-1, cw), m_["conv_w"].reshape(-1, cw),
                           v_["conv_w"].reshape(-1, cw), "adamw_conv_w"))

    loss = lax.psum(loss, ("x", "y", "c"))
    return loss, dh, out_g, out_d, out_m, out_v


def kernel(x, c, w_ada, b_ada, norm1_w, norm2_w, w_in, conv_w, conv_b, dt_bias, a_log, d_skip, ssd_norm_w, w_ssd_out, q_norm_w, k_norm_w, w_attn_out, w_pool_mix, pool_scale, w_pool_out, w_out, w_ff1, w_ff2, loss_target, m_w_ada, m_b_ada, m_norm1_w, m_norm2_w, m_w_in, m_conv_w, m_conv_b, m_dt_bias, m_a_log, m_d_skip, m_ssd_norm_w, m_w_ssd_out, m_q_norm_w, m_k_norm_w, m_w_attn_out, m_w_pool_mix, m_pool_scale, m_w_pool_out, m_w_out, m_w_ff1, m_w_ff2, v_w_ada, v_b_ada, v_norm1_w, v_norm2_w, v_w_in, v_conv_w, v_conv_b, v_dt_bias, v_a_log, v_d_skip, v_ssd_norm_w, v_w_ssd_out, v_q_norm_w, v_k_norm_w, v_w_attn_out, v_w_pool_mix, v_pool_scale, v_w_pool_out, v_w_out, v_w_ff1, v_w_ff2):
    p = dict(w_ada=w_ada, b_ada=b_ada, norm1_w=norm1_w, norm2_w=norm2_w, w_in=w_in, conv_w=conv_w, conv_b=conv_b,
             dt_bias=dt_bias, a_log=a_log, d_skip=d_skip, ssd_norm_w=ssd_norm_w, w_ssd_out=w_ssd_out, q_norm_w=q_norm_w,
             k_norm_w=k_norm_w, w_attn_out=w_attn_out, w_pool_mix=w_pool_mix, pool_scale=pool_scale,
             w_pool_out=w_pool_out, w_out=w_out, w_ff1=w_ff1, w_ff2=w_ff2)
    m_ = dict(w_ada=m_w_ada, b_ada=m_b_ada, norm1_w=m_norm1_w, norm2_w=m_norm2_w, w_in=m_w_in, conv_w=m_conv_w,
              conv_b=m_conv_b, dt_bias=m_dt_bias, a_log=m_a_log, d_skip=m_d_skip, ssd_norm_w=m_ssd_norm_w,
              w_ssd_out=m_w_ssd_out, q_norm_w=m_q_norm_w, k_norm_w=m_k_norm_w, w_attn_out=m_w_attn_out,
              w_pool_mix=m_w_pool_mix, pool_scale=m_pool_scale, w_pool_out=m_w_pool_out, w_out=m_w_out, w_ff1=m_w_ff1,
              w_ff2=m_w_ff2)
    v_ = dict(w_ada=v_w_ada, b_ada=v_b_ada, norm1_w=v_norm1_w, norm2_w=v_norm2_w, w_in=v_w_in, conv_w=v_conv_w,
              conv_b=v_conv_b, dt_bias=v_dt_bias, a_log=v_a_log, d_skip=v_d_skip, ssd_norm_w=v_ssd_norm_w,
              w_ssd_out=v_w_ssd_out, q_norm_w=v_q_norm_w, k_norm_w=v_k_norm_w, w_attn_out=v_w_attn_out,
              w_pool_mix=v_w_pool_mix, pool_scale=v_pool_scale, w_pool_out=v_w_pool_out, w_out=v_w_out, w_ff1=v_w_ff1,
              w_ff2=v_w_ff2)
    loss, dx, g, d, nm, nv = _step(x[0], c, loss_target[0], p, m_, v_)
    return (loss, dx[None], *[g[k] for k in WEIGHTS], *[d[k] for k in WEIGHTS], *[nm[k] for k in WEIGHTS],
            *[nv[k] for k in WEIGHTS])
```

```python
import functools
import math

import numpy as np
import jax
import jax.numpy as jnp
from jax import lax
from jax.experimental import pallas as pl
from jax.experimental.pallas import tpu as pltpu

F32 = jnp.float32
BF16 = jnp.bfloat16
MESH = pl.DeviceIdType.MESH
HIGHEST = lax.Precision.HIGHEST

N_DEV = 8
V7X_VMEM_LIMIT_BYTES = 56 * 1024 * 1024
LANES = 128
ROW_TILE = 512

EPS = 1e-6
D_MODEL = 1024
SSD_HEADS = 16
SSD_HEAD_DIM = 64
SSD_CHUNK = 128
SSD_STATE = 128
ATTN_HEADS = 12
ATTN_HEAD_DIM = 128
ATTN_STEPS = 128
DILATIONS = (1, 4, 16)
POOL_WINDOWS = (2, 4, 8, 16)
POOL_HALO = 16
CONV_HALO = 8
NEG = -1e30

ADAM_LR = 0.001
ADAM_B1 = 0.9
ADAM_B2 = 0.999
ADAM_EPS = 1e-08
ADAM_WD = 0.01
ADAM_STEP = 10

QKV_OFF, QKV_W = 0, 4608
XBC_OFF, XBC_W = 4608, 1536
GATE_OFF, GATE_W = 6144, 3072
Z_OFF, Z_W = 9216, 1024
U_OFF, U_W = 10240, 1024
DT_OFF, DT_W = 11264, 256
NP = 11520
IN_WIDTH = 11280
O_Z, O_XBC, O_DT, O_Q, O_K, O_V, O_U, O_G = 0, 1024, 2560, 2576, 4112, 5648, 7184, 8208


def _alibi_slopes(n):
    def pow2(k):
        start = 2.0 ** (-8.0 / k)
        return [start ** (i + 1) for i in range(k)]
    if math.log2(n).is_integer():
        s = pow2(n)
    else:
        c = 2 ** math.floor(math.log2(n))
        s = pow2(c) + pow2(2 * c)[0::2][: n - c]
    return np.sort(np.asarray(s, np.float32))[::-1].copy()


SLOPES = _alibi_slopes(ATTN_HEADS).reshape(3, 4)


def _cp(sem=None):
    return pltpu.CompilerParams(dimension_semantics=sem, vmem_limit_bytes=V7X_VMEM_LIMIT_BYTES)


def _pick(dim, cap):
    if dim <= cap:
        return dim
    t = cap - cap % LANES
    while t >= LANES:
        if dim % t == 0:
            return t
        t -= LANES
    raise ValueError((dim, cap))


def _sigmoid(x):
    return 1.0 / (1.0 + jnp.exp(-x))


def _silu(x):
    return x * _sigmoid(x)


def _dsilu(x):
    s = _sigmoid(x)
    return s * (1.0 + x * (1.0 - s))


def _softplus(x):
    return jnp.maximum(x, 0.0) + jnp.log(1.0 + jnp.exp(-jnp.abs(x)))


def _dot(a, b, dims):
    return lax.dot_general(a, b, (dims, ((), ())), preferred_element_type=F32)


def _nn(a, b):
    return _dot(a, b, ((1,), (0,)))


def _nt(a, b):
    return _dot(a, b, ((1,), (1,)))


def _tn(a, b):
    return _dot(a, b, ((0,), (0,)))


def _iota(shape, axis):
    return lax.broadcasted_iota(jnp.int32, shape, axis)


def _position():
    return lax.axis_index("x"), lax.axis_index("y"), lax.axis_index("c")


def _index(px, py, pc):
    return 4 * px + 2 * py + pc


def _remote(src, dst, send_sems, recv_sems, a, k, to):
    return pltpu.make_async_remote_copy(src_ref=src, dst_ref=dst, send_sem=send_sems.at[a, k], recv_sem=recv_sems.at[a, k],
                                        device_id=to, device_id_type=MESH)


def _gather_plan(srcs, dsts, send_sems, recv_sems, local_sems):
    x, y, c = _position()
    me, sib = (x, y, c), (x, y, 1 - c)
    chips = [(1 - x, y), (x, 1 - y), (1 - x, 1 - y)]
    n = len(srcs)

    def start():
        for a in range(n):
            mine = dsts[a].at[_index(*me)]
            pltpu.make_async_copy(srcs[a], mine, local_sems.at[a]).start()
            _remote(srcs[a], mine, send_sems, recv_sems, a, 0, sib).start()
            for j, chip in enumerate(chips):
                _remote(srcs[a], mine, send_sems, recv_sems, a, 1 + j, (*chip, c)).start()

    def finish():
        for a in range(n):
            for j, chip in enumerate(chips):
                blk = dsts[a].at[_index(*chip, c)]
                _remote(blk, blk, send_sems, recv_sems, a, 1 + j, me).wait_recv()
                _remote(blk, blk, send_sems, recv_sems, a, 4 + j, sib).start()
        for a in range(n):
            mine = dsts[a].at[_index(*me)]
            blk = dsts[a].at[_index(*sib)]
            _remote(blk, blk, send_sems, recv_sems, a, 0, me).wait_recv()
            for j, chip in enumerate(chips):
                blk = dsts[a].at[_index(*chip, 1 - c)]
                _remote(blk, blk, send_sems, recv_sems, a, 4 + j, me).wait_recv()
            for k in range(N_DEV - 1):
                _remote(mine, mine, send_sems, recv_sems, a, k, me).wait_send()
            pltpu.make_async_copy(srcs[a], mine, local_sems.at[a]).wait()

    return start, finish


def _scatter_plan(srcs, dsts, send_sems, recv_sems, local_sems):
    x, y, c = _position()
    me = _index(x, y, c)
    n = len(srcs)
    copies = []
    for a in range(n):
        copies.append(pltpu.make_async_copy(srcs[a].at[me], dsts[a].at[me], local_sems.at[a]))
        for k in range(1, N_DEV):
            px, py, pc = x ^ ((k >> 2) & 1), y ^ ((k >> 1) & 1), c ^ (k & 1)
            copies.append(_remote(srcs[a].at[_index(px, py, pc)], dsts[a].at[me], send_sems, recv_sems, a, k - 1, (px, py, pc)))

    def start():
        for cp in copies:
            cp.start()

    def finish():
        for cp in copies:
            cp.wait()

    return start, finish


class Rider:
    def __init__(self, kind, arrs):
        self.kind = kind
        items = [a if isinstance(a, tuple) else (a, None) for a in arrs]
        self.arrs = [a for a, _ in items]
        self.lead = [i for _, i in items]

    def _shapes(self):
        return [a.shape if i is None else a.shape[1:] for a, i in zip(self.arrs, self.lead)]

    def out_shapes(self):
        if self.kind == "gather":
            return [pltpu.HBM((N_DEV,) + s, a.dtype) for s, a in zip(self._shapes(), self.arrs)]
        return [pltpu.HBM(s, a.dtype) for s, a in zip(self._shapes(), self.arrs)]

    def sems(self):
        n = len(self.arrs)
        return [pltpu.SemaphoreType.DMA((n, N_DEV - 1)), pltpu.SemaphoreType.DMA((n, N_DEV - 1)),
                pltpu.SemaphoreType.DMA((n,))]

    def plan(self, srcs, dsts, sems):
        srcs = [s if i is None else s.at[i] for s, i in zip(srcs, self.lead)]
        return (_gather_plan if self.kind == "gather" else _scatter_plan)(srcs, dsts, *sems)


def _pcall(body, grid, in_specs, out_specs, out_shape, name, ins, scratch=(), aliases=None, rider=None):
    in_specs, out_specs, out_shape, scratch = list(in_specs), list(out_specs), list(out_shape), list(scratch)
    sem = ("arbitrary",) * len(grid)
    if rider is None:
        res = pl.pallas_call(body, grid=grid, in_specs=in_specs, out_specs=out_specs, out_shape=out_shape,
                             scratch_shapes=scratch, name=name, input_output_aliases=aliases or {},
                             compiler_params=_cp(sem))(*ins)
        return list(res), []
    n_in, n_out, n_sc, nr = len(in_specs), len(out_shape), len(scratch), len(rider.arrs)

    def wrapped(*refs):
        own_in, srcs = refs[:n_in], refs[n_in:n_in + nr]
        own_out, dsts = refs[n_in + nr:n_in + nr + n_out], refs[n_in + nr + n_out:n_in + 2 * nr + n_out]
        own_sc, sems = refs[n_in + 2 * nr + n_out:n_in + 2 * nr + n_out + n_sc], refs[n_in + 2 * nr + n_out + n_sc:]
        ids = [pl.program_id(ax) for ax in range(len(grid))]
        first = functools.reduce(jnp.logical_and, [i == 0 for i in ids])
        last = functools.reduce(jnp.logical_and, [i == g - 1 for i, g in zip(ids, grid)])
        start, finish = rider.plan(srcs, dsts, sems)
        pl.when(first)(start)
        body(*own_in, *own_out, *own_sc)
        pl.when(last)(finish)

    hbm = pl.BlockSpec(memory_space=pltpu.HBM)
    res = pl.pallas_call(
        wrapped, grid=grid, in_specs=in_specs + [hbm] * nr, out_specs=out_specs + [hbm] * nr,
        out_shape=out_shape + rider.out_shapes(), scratch_shapes=scratch + rider.sems(), name=name,
        input_output_aliases=aliases or {},
        compiler_params=pltpu.CompilerParams(dimension_semantics=sem, vmem_limit_bytes=V7X_VMEM_LIMIT_BYTES,
                                             has_side_effects=True))(*ins, *rider.arrs)
    return list(res[:n_out]), list(res[n_out:])


def _exchange(kind, arrs, name):
    rd = Rider(kind, arrs)
    n = len(arrs)

    def body(*refs):
        start, finish = rd.plan(refs[:n], refs[n:2 * n], refs[2 * n:])
        start()
        finish()

    hbm = pl.BlockSpec(memory_space=pltpu.HBM)
    return pl.pallas_call(
        body, in_specs=[hbm] * n, out_specs=[hbm] * n, out_shape=rd.out_shapes(), scratch_shapes=rd.sems(),
        name=name, compiler_params=pltpu.CompilerParams(has_side_effects=True))(*rd.arrs)


def all_gather(arrs, name):
    return _exchange("gather", arrs, name)


def all_to_all(arrs, name):
    return _exchange("scatter", arrs, name)


def mm(a, b, mode, out_dtype, name, rider=None, extra=(), epi=None, out_dtypes=None, col_blocks=False, tm_cap=1024,
       tn_cap=1280, tk_cap=2048):
    blk = None
    if col_blocks and mode != "tn":
        nb, rows_b, blk = b.shape
        b_shape = (rows_b, nb * blk)
    else:
        b_shape = b.shape
    if mode == "nn":
        (M, K), (K2, N) = a.shape, b_shape
    elif mode == "nt":
        (M, K), (N, K2) = a.shape, b_shape
    else:
        (K, M), (K2, N) = a.shape, b_shape
        if col_blocks:
            blk = N // N_DEV
    assert K == K2, (a.shape, b.shape, mode)
    tm, tn, tk = _pick(M, tm_cap), _pick(N, tn_cap), _pick(K, tk_cap)
    if col_blocks and mode == "nt":
        tk = blk
    elif col_blocks:
        tn = blk
    nk = K // tk
    if mode == "tn":
        a_spec = pl.BlockSpec((tk, tm), lambda i, j, k: (k, i))
    else:
        a_spec = pl.BlockSpec((tm, tk), lambda i, j, k: (i, k))
    if mode == "nt":
        b_spec = (pl.BlockSpec((None, tn, tk), lambda i, j, k: (k, j, 0)) if col_blocks
                  else pl.BlockSpec((tn, tk), lambda i, j, k: (j, k)))
    else:
        b_spec = (pl.BlockSpec((None, tk, tn), lambda i, j, k: (j, k, 0)) if col_blocks and mode == "nn"
                  else pl.BlockSpec((tk, tn), lambda i, j, k: (k, j)))
    dims = {"nn": ((1,), (0,)), "nt": ((1,), (1,)), "tn": ((0,), (0,))}[mode]
    o_spec = pl.BlockSpec((tm, tn), lambda i, j, k: (i, j))
    blocked_out = col_blocks and mode == "tn"
    if blocked_out:
        assert not extra and epi is None
        o_spec = pl.BlockSpec((None, tm, tn), lambda i, j, k: (j, i, 0))
    v_spec = pl.BlockSpec((1, tn), lambda i, j, k: (0, j))
    grid = (M // tm, N // tn, nk)
    out_dtypes = list(out_dtypes) if out_dtypes is not None else [out_dtype]
    nr = len(rider.arrs) if rider is not None else 0
    ne, no = len(extra), len(out_dtypes)
    extra_specs = []
    for e in extra:
        assert e.shape in ((M, N), (1, N)), (e.shape, M, N)
        extra_specs.append(o_spec if e.shape[0] == M and M != 1 else v_spec)

    def body(*refs):
        a_ref, b_ref = refs[0], refs[1]
        e_refs = refs[2:2 + ne]
        srcs = refs[2 + ne:2 + ne + nr]
        o_refs = refs[2 + ne + nr:2 + ne + nr + no]
        dsts = refs[2 + ne + nr + no:2 + ne + 2 * nr + no]
        rest = refs[2 + ne + 2 * nr + no:]
        i, j, k = pl.program_id(0), pl.program_id(1), pl.program_id(2)
        if nr:
            start, finish = rider.plan(srcs, dsts, rest[-3:])

            @pl.when(jnp.logical_and(jnp.logical_and(i == 0, j == 0), k == 0))
            def _():
                start()

        def write(r):
            vals = epi(r, *[e[...] for e in e_refs]) if epi is not None else (r,)
            for o_ref, val, dt in zip(o_refs, vals, out_dtypes):
                o_ref[...] = val.astype(dt)

        prod = _dot(a_ref[...].astype(BF16), b_ref[...].astype(BF16), dims)
        if nk == 1:
            write(prod)
        else:
            acc = rest[0]

            @pl.when(k == 0)
            def _():
                acc[...] = prod

            @pl.when(k > 0)
            def _():
                acc[...] += prod

            @pl.when(k == nk - 1)
            def _():
                write(acc[...])

        if nr:
            @pl.when(jnp.logical_and(jnp.logical_and(i == grid[0] - 1, j == grid[1] - 1), k == nk - 1))
            def _():
                finish()

    hbm = pl.BlockSpec(memory_space=pltpu.HBM)
    out_shape = [jax.ShapeDtypeStruct((N_DEV, M, blk) if blocked_out else (M, N), dt) for dt in out_dtypes]
    scratch = [pltpu.VMEM((tm, tn), F32)] if nk > 1 else []
    if nr:
        res = pl.pallas_call(
            body, grid=grid, in_specs=[a_spec, b_spec] + extra_specs + [hbm] * nr, out_specs=[o_spec] * no + [hbm] * nr,
            out_shape=out_shape + rider.out_shapes(), scratch_shapes=scratch + rider.sems(), name=name,
            compiler_params=pltpu.CompilerParams(dimension_semantics=("arbitrary",) * 3,
                                                 vmem_limit_bytes=V7X_VMEM_LIMIT_BYTES, has_side_effects=True),
        )(a, b, *extra, *rider.arrs)
        outs, got = list(res[:no]), list(res[no:])
        return (outs[0] if no == 1 else outs), got
    res = pl.pallas_call(
        body, grid=grid, in_specs=[a_spec, b_spec] + extra_specs, out_specs=[o_spec] * no, out_shape=out_shape,
        scratch_shapes=scratch, name=name, compiler_params=_cp(("parallel", "parallel", "arbitrary")))(a, b, *extra)
    return res[0] if no == 1 else list(res)


def _rows(tile, w, cb=0):
    return pl.BlockSpec((tile, w), lambda i: (i, cb))


def _vec(shape):
    nd = len(shape)
    return pl.BlockSpec(shape, lambda i: (0,) * nd)


def _any():
    return pl.BlockSpec(memory_space=pl.ANY)


def _sds(shape, dtype):
    return jax.ShapeDtypeStruct(shape, dtype)


def _rowcall(body, n_rows, tile, in_specs, out_specs, out_shape, name, aliases=None):
    return pl.pallas_call(
        body, grid=(n_rows // tile,), in_specs=in_specs, out_specs=out_specs, out_shape=out_shape,
        name=name, input_output_aliases=aliases or {}, compiler_params=_cp(("arbitrary",)))


def _zero_at_first(step, *refs):
    @pl.when(step == 0)
    def _():
        for r in refs:
            r[...] = jnp.zeros_like(r)


def rms_mod_fwd(x, nw, sc, sh, name):
    S, D = x.shape
    T = ROW_TILE

    def body(x_ref, nw_ref, sc_ref, sh_ref, h_ref):
        xv = x_ref[...]
        r = lax.rsqrt(jnp.mean(xv * xv, axis=-1, keepdims=True) + EPS)
        y = xv * r * nw_ref[...]
        h_ref[...] = (y * (1.0 + sc_ref[...]) + sh_ref[...]).astype(BF16)

    return _rowcall(body, S, T, [_rows(T, D), _vec((1, D)), _vec((1, D)), _vec((1, D))], _rows(T, D),
                    _sds((S, D), BF16), name)(x, nw, sc, sh)


def rms_mod_bwd(x, dh, dres, nw, sc, name, gated=None):
    S, D = x.shape
    T = ROW_TILE

    def body(x_ref, dh_ref, dres_ref, nw_ref, sc_ref, *rest):
        if gated is not None:
            g_ref, y_ref, dx_ref, dnw_ref, dsc_ref, dsh_ref, dy_ref, dg_ref = rest
            _zero_at_first(pl.program_id(0), dg_ref)
        else:
            dx_ref, dnw_ref, dsc_ref, dsh_ref = rest
        _zero_at_first(pl.program_id(0), dnw_ref, dsc_ref, dsh_ref)
        xv = x_ref[...]
        d = dh_ref[...]
        r = lax.rsqrt(jnp.mean(xv * xv, axis=-1, keepdims=True) + EPS)
        xh = xv * r
        nwv = nw_ref[...]
        dsh_ref[...] += jnp.sum(d, axis=0, keepdims=True)
        dsc_ref[...] += jnp.sum(d * (xh * nwv), axis=0, keepdims=True)
        dy = d * (1.0 + sc_ref[...])
        dnw_ref[...] += jnp.sum(dy * xh, axis=0, keepdims=True)
        dxh = dy * nwv
        dx = dres_ref[...] + r * (dxh - xh * jnp.mean(dxh * xh, axis=-1, keepdims=True))
        dx_ref[...] = dx
        if gated is not None:
            dy_ref[...] = (g_ref[...] * dx).astype(BF16)
            dg_ref[...] += jnp.sum(dx * y_ref[...], axis=0, keepdims=True)

    v = _vec((1, D))
    r_ = _rows(T, D)
    ins, in_specs = [x, dh, dres, nw, sc], [r_, r_, r_, v, v]
    out_specs = [r_, v, v, v]
    out_shape = [_sds((S, D), F32), _sds((1, D), F32), _sds((1, D), F32), _sds((1, D), F32)]
    if gated is not None:
        ins, in_specs = ins + list(gated), in_specs + [v, r_]
        out_specs, out_shape = out_specs + [r_, v], out_shape + [_sds((S, D), BF16), _sds((1, D), F32)]
    return _rowcall(body, S, T, in_specs, out_specs, out_shape, name)(*ins)


def _shift_down(tile, halo, k):
    if k == 0:
        return tile
    hr = halo.shape[0]
    rolled = pltpu.roll(tile, k, 0)
    rh = pltpu.roll(halo, k, 0)
    rid = _iota(halo.shape, 0)
    first = jnp.where(rid < k, rh, rolled[:hr])
    return jnp.concatenate([first, rolled[hr:]], axis=0)


def _shift_up(tile, nxt, k):
    if k == 0:
        return tile
    T = tile.shape[0]
    hr = nxt.shape[0]
    rolled = pltpu.roll(tile, T - k, 0)
    rn = pltpu.roll(nxt, hr - k, 0)
    rid = _iota(nxt.shape, 0)
    last = jnp.where(rid >= hr - k, rn, rolled[T - hr:])
    return jnp.concatenate([rolled[:T - hr], last], axis=0)


CONV_CB = 512


def conv_fwd(P, conv_w, conv_b, name):
    S = P.shape[0]
    T = ROW_TILE
    nj = XBC_W // CONV_CB
    cb0 = XBC_OFF // CONV_CB
    hb = T // CONV_HALO

    def body(x_ref, h_ref, w_ref, b_ref, pre_ref, xc_ref):
        i = pl.program_id(0)
        xb = x_ref[...]
        xh = jnp.where(i > 0, h_ref[...], 0.0)
        acc = b_ref[...] + w_ref[3:4, :] * xb
        for idx in range(3):
            acc = acc + w_ref[idx:idx + 1, :] * _shift_down(xb, xh, 3 - idx)
        pre_ref[...] = acc
        xc_ref[...] = _silu(acc)

    blk = pl.BlockSpec((T, CONV_CB), lambda i, j: (i, j))
    return pl.pallas_call(
        body, grid=(S // T, nj),
        in_specs=[pl.BlockSpec((T, CONV_CB), lambda i, j: (i, cb0 + j)),
                  pl.BlockSpec((CONV_HALO, CONV_CB), lambda i, j: (jnp.maximum(i * hb - 1, 0), cb0 + j)),
                  pl.BlockSpec((4, CONV_CB), lambda i, j: (0, j)),
                  pl.BlockSpec((1, CONV_CB), lambda i, j: (0, j))],
        out_specs=[blk, blk], out_shape=[_sds((S, XBC_W), F32), _sds((S, XBC_W), F32)], name=name,
        compiler_params=_cp(("arbitrary", "arbitrary")))(P, P, conv_w, conv_b)


def conv_bwd(dxc, pre, P, dP, conv_w, name):
    S = P.shape[0]
    T = ROW_TILE
    nt = S // T
    nj = XBC_W // CONV_CB
    cb0 = XBC_OFF // CONV_CB
    hb = T // CONV_HALO

    def body(d_ref, dn_ref, p_ref, pn_ref, x_ref, xh_ref, w_ref, dp_in, dp_ref, dw_ref, db_ref):
        i = pl.program_id(1)
        _zero_at_first(i, dw_ref, db_ref)
        dpre = d_ref[...] * _dsilu(p_ref[...])
        dpn = jnp.where(i < nt - 1, dn_ref[...] * _dsilu(pn_ref[...]), 0.0)
        db_ref[...] += jnp.sum(dpre, axis=0, keepdims=True)
        xb = x_ref[...]
        xh = jnp.where(i > 0, xh_ref[...], 0.0)
        dx = jnp.zeros_like(dpre)
        for idx in range(4):
            k = 3 - idx
            dw_ref[idx:idx + 1, :] += jnp.sum(dpre * _shift_down(xb, xh, k), axis=0, keepdims=True)
            dx = dx + w_ref[idx:idx + 1, :] * _shift_up(dpre, dpn, k)
        dp_ref[...] = dx.astype(BF16)

    cur = pl.BlockSpec((T, CONV_CB), lambda j, i: (i, j))
    nxt = pl.BlockSpec((CONV_HALO, CONV_CB), lambda j, i: (jnp.minimum((i + 1) * hb, S // CONV_HALO - 1), j))
    return pl.pallas_call(
        body, grid=(nj, nt),
        in_specs=[cur, nxt, cur, nxt,
                  pl.BlockSpec((T, CONV_CB), lambda j, i: (i, cb0 + j)),
                  pl.BlockSpec((CONV_HALO, CONV_CB), lambda j, i: (jnp.maximum(i * hb - 1, 0), cb0 + j)),
                  pl.BlockSpec((4, CONV_CB), lambda j, i: (0, j)), _any()],
        out_specs=[pl.BlockSpec((T, CONV_CB), lambda j, i: (i, cb0 + j)),
                   pl.BlockSpec((4, CONV_CB), lambda j, i: (0, j)),
                   pl.BlockSpec((1, CONV_CB), lambda j, i: (0, j))],
        out_shape=[_sds(dP.shape, BF16), _sds((4, XBC_W), F32), _sds((1, XBC_W), F32)],
        input_output_aliases={7: 0}, name=name,
        compiler_params=_cp(("arbitrary", "arbitrary")))(dxc, dxc, pre, pre, P, P, conv_w, dP)


def _col(v, h, lane):
    return jnp.sum(jnp.where(lane == h, v, 0.0), axis=1, keepdims=True)


def _row(v, h, rowi):
    return jnp.sum(jnp.where(rowi == h, v, 0.0), axis=0, keepdims=True)


def _ssd_common(dt_ref, dtb_ref, al_ref):
    L = SSD_CHUNK
    lane = _iota((L, L), 1)
    rowi = _iota((L, L), 0)
    dtr = dt_ref[...] + dtb_ref[...]
    dt = _softplus(dtr)
    A = -jnp.exp(al_ref[...])
    a = dt * A
    tril = (rowi >= lane).astype(F32)
    ac = jnp.dot(tril, a, precision=HIGHEST, preferred_element_type=F32)
    acT = ac.T
    last = _row(ac, L - 1, rowi)
    return dict(lane=lane, rowi=rowi, dtr=dtr, dt=dt, A=A, ac=ac, acT=acT, last=last,
                eac=jnp.exp(ac), fac=jnp.exp(last - ac), cd=jnp.exp(last),
                lo=lane < SSD_HEAD_DIM, causal=rowi >= lane)


def _pair(v, h0, q):
    lane = q["lane"][:v.shape[0]]
    lo = q["lo"][:v.shape[0]]
    return jnp.where(lo, _col(v, h0, lane), _col(v, h0 + 1, lane))


def _decay(q, h):
    seg = _col(q["ac"], h, q["lane"]) - _row(q["acT"], h, q["rowi"])
    return jnp.exp(jnp.where(q["causal"], seg, NEG))


def ssd_fwd(xc, P, dtb, alog, dsk, name, rider=None):
    S = xc.shape[0]
    L = SSD_CHUNK
    nc = S // L

    def body(xs_ref, b_ref, c_ref, dt_ref, dtb_ref, al_ref, dk_ref, y_ref, st_ref, H):
        @pl.when(pl.program_id(0) == 0)
        def _():
            H[...] = jnp.zeros_like(H)

        st_ref[0] = H[...]
        q = _ssd_common(dt_ref, dtb_ref, al_ref)
        dk = dk_ref[...]
        for g in range(2):
            Bg = b_ref[:, g * L:(g + 1) * L].astype(BF16)
            Cg = c_ref[:, g * L:(g + 1) * L].astype(BF16)
            cb = _nt(Cg, Bg)
            for jj in range(4):
                j = g * 4 + jj
                h0 = 2 * j
                sl = slice(j * L, (j + 1) * L)
                xs_p = xs_ref[:, sl]
                X = xs_p * _pair(q["dt"], h0, q)
                Xb = X.astype(BF16)
                G0 = (cb * _decay(q, h0)).astype(BF16)
                G1 = (cb * _decay(q, h0 + 1)).astype(BF16)
                Yd = jnp.where(q["lo"], _nn(G0, Xb), _nn(G1, Xb))
                Hp = H[:, sl]
                Yo = _nn(Cg, Hp.astype(BF16)) * _pair(q["eac"], h0, q)
                y_ref[:, sl] = Yd + Yo + _pair(dk, h0, q) * xs_p
                Xd = (X * _pair(q["fac"], h0, q)).astype(BF16)
                H[:, sl] = _pair(q["cd"], h0, q) * Hp + _tn(Bg, Xd)

    v = _vec((1, LANES))
    (y, st), got = _pcall(
        body, (nc,),
        [pl.BlockSpec((L, 1024), lambda c: (c, 0)),
         pl.BlockSpec((L, 256), lambda c: (c, 4)),
         pl.BlockSpec((L, 256), lambda c: (c, 5)),
         pl.BlockSpec((L, LANES), lambda c: (c, DT_OFF // LANES)), v, v, v],
        [pl.BlockSpec((L, 1024), lambda c: (c, 0)),
         pl.BlockSpec((1, SSD_STATE, 1024), lambda c: (c, 0, 0))],
        [_sds((S, 1024), F32), _sds((nc, SSD_STATE, 1024), F32)], name, (xc, xc, xc, P, dtb, alog, dsk),
        scratch=[pltpu.VMEM((SSD_STATE, 1024), F32)], rider=rider)
    return y, st, got


def ssd_bwd(xc, P, st, dy, dP, dtb, alog, dsk, name, rider=None):
    S = xc.shape[0]
    L = SSD_CHUNK
    nc = S // L

    def body(xs_ref, b_ref, c_ref, dt_ref, st_ref, dy_ref, dtb_ref, al_ref, dk_ref, dp_in,
             dxc_ref, dp_ref, ddtb_ref, dal_ref, dD_ref, dH):
        step = pl.program_id(0)

        @pl.when(step == 0)
        def _():
            dH[...] = jnp.zeros_like(dH)

        _zero_at_first(step, ddtb_ref, dal_ref, dD_ref)
        q = _ssd_common(dt_ref, dtb_ref, al_ref)
        lane, rowi, lo = q["lane"], q["rowi"], q["lo"]
        lane1 = lane[:1]
        lo1 = lo[:1]
        dk = dk_ref[...]
        dac = jnp.zeros((L, L), F32)
        dacT = jnp.zeros((L, L), F32)
        dlast = jnp.zeros((1, L), F32)
        ddt = jnp.zeros((L, L), F32)

        def put(h0, s0, s1):
            return jnp.where(lane == h0, s0, 0.0) + jnp.where(lane == h0 + 1, s1, 0.0)

        def put1(h0, s0, s1):
            return jnp.where(lane1 == h0, s0, 0.0) + jnp.where(lane1 == h0 + 1, s1, 0.0)

        def hsum(v):
            return (jnp.sum(jnp.where(lo, v, 0.0), axis=1, keepdims=True),
                    jnp.sum(jnp.where(lo, 0.0, v), axis=1, keepdims=True))

        for g in range(2):
            Bg = b_ref[:, g * L:(g + 1) * L].astype(BF16)
            Cg = c_ref[:, g * L:(g + 1) * L].astype(BF16)
            cb = _nt(Cg, Bg)
            dcb = jnp.zeros((L, L), F32)
            dCg = jnp.zeros((L, L), F32)
            dBg = jnp.zeros((L, L), F32)
            for jj in range(4):
                j = g * 4 + jj
                h0 = 2 * j
                sl = slice(j * L, (j + 1) * L)
                xs_p = xs_ref[:, sl]
                dY = dy_ref[:, sl]
                dYb = dY.astype(BF16)
                dtp = _pair(q["dt"], h0, q)
                X = xs_p * dtp
                e_p = _pair(q["eac"], h0, q)
                f_p = _pair(q["fac"], h0, q)
                cd_p = _pair(q["cd"], h0, q)
                Hp = st_ref[0, :, sl]
                Hpb = Hp.astype(BF16)
                dHn = dH[:, sl]
                dHnb = dHn.astype(BF16)
                dD_ref[:, sl] += jnp.sum(dY * xs_p, axis=0, keepdims=True)
                dxs = _pair(dk, h0, q) * dY
                Yo = _nn(Cg, Hpb) * e_p
                dZb = (dY * e_p).astype(BF16)
                s0, s1 = hsum(dY * Yo)
                dac = dac + put(h0, s0, s1)
                dCg = dCg + _nt(dZb, Hpb)
                dHp = _tn(Cg, dZb) + cd_p * dHn
                t = jnp.sum(dHn * Hp, axis=0, keepdims=True)
                d0 = jnp.sum(jnp.where(lo1, t, 0.0), axis=1, keepdims=True)
                d1 = jnp.sum(jnp.where(lo1, 0.0, t), axis=1, keepdims=True)
                dlast = dlast + put1(h0, d0, d1) * q["cd"]
                Xd = X * f_p
                dXd = _nn(Bg, dHnb)
                dBg = dBg + _nt(Xd.astype(BF16), dHnb)
                dX = dXd * f_p
                s0, s1 = hsum(dXd * Xd)
                dac = dac - put(h0, s0, s1)
                dlast = dlast + put1(h0, jnp.sum(s0, axis=0, keepdims=True), jnp.sum(s1, axis=0, keepdims=True))
                for hh, mask in ((h0, lo), (h0 + 1, jnp.logical_not(lo))):
                    dec = _decay(q, hh)
                    Gm = cb * dec
                    Xh = jnp.where(mask, X, 0.0).astype(BF16)
                    dG = _nt(dYb, Xh)
                    dcb = dcb + dG * dec
                    Q = dG * Gm
                    dac = dac + jnp.where(lane == hh, jnp.sum(Q, axis=1, keepdims=True), 0.0)
                    dacT = dacT - jnp.where(rowi == hh, jnp.sum(Q, axis=0, keepdims=True), 0.0)
                    dX = dX + jnp.where(mask, _tn(Gm.astype(BF16), dYb), 0.0)
                dxs = dxs + dX * dtp
                s0, s1 = hsum(dX * xs_p)
                ddt = ddt + put(h0, s0, s1)
                dxc_ref[:, sl] = dxs
                dH[:, sl] = dHp
            dcbb = dcb.astype(BF16)
            dxc_ref[:, 1024 + g * L:1024 + (g + 1) * L] = dBg + _tn(dcbb, Cg)
            dxc_ref[:, 1280 + g * L:1280 + (g + 1) * L] = dCg + _nn(dcbb, Bg)

        dac = dac + dacT.T + jnp.where(rowi == L - 1, dlast, 0.0)
        triu = (rowi <= lane).astype(F32)
        da = jnp.dot(triu, dac, precision=HIGHEST, preferred_element_type=F32)
        ddt = ddt + da * q["A"]
        dal_ref[...] += jnp.sum(da * q["dt"], axis=0, keepdims=True) * q["A"]
        ddtr = jnp.where(lane < SSD_HEADS, ddt * _sigmoid(q["dtr"]), 0.0)
        ddtb_ref[...] += jnp.sum(ddtr, axis=0, keepdims=True)
        dp_ref[...] = jnp.concatenate([ddtr, jnp.zeros_like(ddtr)], axis=1).astype(BF16)

    v = _vec((1, LANES))
    rev = lambda c: nc - 1 - c
    res, got = _pcall(
        body, (nc,),
        [pl.BlockSpec((L, 1024), lambda c: (rev(c), 0)),
         pl.BlockSpec((L, 256), lambda c: (rev(c), 4)),
         pl.BlockSpec((L, 256), lambda c: (rev(c), 5)),
         pl.BlockSpec((L, LANES), lambda c: (rev(c), DT_OFF // LANES)),
         pl.BlockSpec((1, SSD_STATE, 1024), lambda c: (rev(c), 0, 0)),
         pl.BlockSpec((L, 1024), lambda c: (rev(c), 0)), v, v, v, _any()],
        [pl.BlockSpec((L, XBC_W), lambda c: (rev(c), 0)),
         pl.BlockSpec((L, DT_W), lambda c: (rev(c), DT_OFF // DT_W)),
         v, v, _vec((1, 1024))],
        [_sds((S, XBC_W), F32), _sds(dP.shape, BF16), _sds((1, LANES), F32), _sds((1, LANES), F32),
         _sds((1, 1024), F32)], name, (xc, xc, xc, P, st, dy, dtb, alog, dsk, dP),
        scratch=[pltpu.VMEM((SSD_STATE, 1024), F32)], aliases={9: 1}, rider=rider)
    return (*res, got)


def ssd_post_fwd(y, P, nw, name):
    S = y.shape[0]
    T = ROW_TILE
    HW = 512

    def body(y_ref, z_ref, w_ref, o_ref):
        g = y_ref[...] * _silu(z_ref[...])
        for k in range(2):
            gk = g[:, k * HW:(k + 1) * HW]
            r = lax.rsqrt(jnp.mean(gk * gk, axis=-1, keepdims=True) + EPS)
            o_ref[:, k * HW:(k + 1) * HW] = (gk * r * w_ref[:, k * HW:(k + 1) * HW]).astype(BF16)

    return _rowcall(body, S, T, [_rows(T, 1024), _rows(T, 1024, Z_OFF // 1024), _vec((1, 1024))], _rows(T, 1024),
                    _sds((S, 1024), BF16), name)(y, P, nw)


def ssd_post_bwd(y, P, dout, dP, nw, name):
    S = y.shape[0]
    T = ROW_TILE
    HW = 512

    def body(y_ref, z_ref, d_ref, w_ref, dp_in, dy_ref, dz_ref, dw_ref):
        _zero_at_first(pl.program_id(0), dw_ref)
        yv = y_ref[...]
        zv = z_ref[...]
        sz = _silu(zv)
        g = yv * sz
        dv = d_ref[...]
        dgs = []
        for k in range(2):
            sl = slice(k * HW, (k + 1) * HW)
            gk = g[:, sl]
            r = lax.rsqrt(jnp.mean(gk * gk, axis=-1, keepdims=True) + EPS)
            gh = gk * r
            dk = dv[:, sl]
            dw_ref[:, sl] += jnp.sum(dk * gh, axis=0, keepdims=True)
            dgn = dk * w_ref[:, sl]
            dgs.append(r * (dgn - gh * jnp.mean(dgn * gh, axis=-1, keepdims=True)))
        dg = jnp.concatenate(dgs, axis=1)
        dy_ref[...] = dg * sz
        dz_ref[...] = (dg * yv * _dsilu(zv)).astype(BF16)

    zc = Z_OFF // 1024
    return _rowcall(body, S, T, [_rows(T, 1024), _rows(T, 1024, zc), _rows(T, 1024), _vec((1, 1024)), _any()],
                    [_rows(T, 1024), _rows(T, 1024, zc), _vec((1, 1024))],
                    [_sds((S, 1024), F32), _sds(dP.shape, BF16), _sds((1, 1024), F32)], name,
                    aliases={4: 1})(y, P, dout, nw, dP)


HEAD_W = 3 * ATTN_HEAD_DIM
HEAD_BLOCKS = NP // HEAD_W


def _slope(gi, h):
    c = [float(v) for v in SLOPES[gi]]
    return jnp.where(h == 0, c[0], jnp.where(h == 1, c[1], jnp.where(h == 2, c[2], c[3])))


def _rmsn(t):
    r = lax.rsqrt(jnp.mean(t * t, axis=-1, keepdims=True) + EPS)
    return t * r, r


def _rms_bwd(th, r, w, dn):
    dth = dn * w
    return r * (dth - th * jnp.mean(dth * th, axis=-1, keepdims=True))


ATT_ROWS = 2048


def _strided(r0, d):
    return pl.ds(r0, ATTN_STEPS, stride=d) if d > 1 else pl.ds(r0, ATTN_STEPS)


def attn_fwd(P, qw, kw, gi, name, rider=None):
    S = P.shape[0]
    d = DILATIONS[gi]
    B = ATTN_STEPS
    E = ATTN_HEAD_DIM
    BR = B * d
    nq = ATT_ROWS // BR
    nsb = S // ATT_ROWS
    scale = E ** -0.5

    def body(q_ref, k_ref, v_ref, kp_ref, vp_ref, qw_ref, kw_ref, o_ref, l_ref):
        h = pl.program_id(0)
        s = pl.program_id(1)
        slope = _slope(gi, h) * float(d)
        qwv = qw_ref[...]
        kwv = kw_ref[...]
        qi = _iota((B, B), 0)
        kj = _iota((B, B), 1)
        rel_c = (qi - kj).astype(F32)
        rel_p = rel_c + float(B)
        bias_c = jnp.where(qi >= kj, -slope * rel_c, NEG)
        bias_p0 = jnp.where(jnp.logical_and(qi <= kj, s > 0), -slope * rel_p, NEG)
        bias_p = jnp.where(qi <= kj, -slope * rel_p, NEG)
        for j in range(nq):
            for r in range(d):
                rows = _strided(j * BR + r, d)
                if j == 0:
                    kpv, vpv, bp = kp_ref[_strided(r, d), :], vp_ref[_strided(r, d), :], bias_p0
                else:
                    prow = _strided((j - 1) * BR + r, d)
                    kpv, vpv, bp = k_ref[prow, :], v_ref[prow, :], bias_p
                qh, _ = _rmsn(q_ref[rows, :])
                kh, _ = _rmsn(k_ref[rows, :])
                kph, _ = _rmsn(kpv)
                qn = (qh * qwv).astype(BF16)
                kn = (kh * kwv).astype(BF16)
                kpn = (kph * kwv).astype(BF16)
                s_c = _nt(qn, kn) * scale + bias_c
                s_p = _nt(qn, kpn) * scale + bp
                m = jnp.maximum(jnp.max(s_c, axis=1, keepdims=True), jnp.max(s_p, axis=1, keepdims=True))
                e_c = jnp.exp(s_c - m)
                e_p = jnp.exp(s_p - m)
                l = jnp.sum(e_c, axis=1, keepdims=True) + jnp.sum(e_p, axis=1, keepdims=True)
                lse = m + jnp.log(l)
                inv = 1.0 / l
                p_c = (e_c * inv).astype(BF16)
                p_p = (e_p * inv).astype(BF16)
                o_ref[rows, :] = _nn(p_c, v_ref[rows, :].astype(BF16)) + _nn(p_p, vpv.astype(BF16))
                l_ref[rows, :] = jnp.broadcast_to(lse, (B, E))

    def cur(i):
        return pl.BlockSpec((ATT_ROWS, E), lambda h, s: (s, (gi * 4 + h) * 3 + i))

    def prev(i):
        return pl.BlockSpec((BR, E), lambda h, s: (jnp.maximum(s * nq - 1, 0), (gi * 4 + h) * 3 + i))

    wv = pl.BlockSpec((1, E), lambda h, s: (0, 0))
    ob = pl.BlockSpec((ATT_ROWS, E), lambda h, s: (s, h))
    (o, l), got = _pcall(body, (4, nsb), [cur(0), cur(1), cur(2), prev(1), prev(2), wv, wv], [ob, ob],
                         [_sds((S, 512), F32), _sds((S, 512), F32)], name, (P, P, P, P, P, qw, kw), rider=rider)
    return o, l, got


def attn_bwd(P, o, lse, do, dlse, dP, qw, kw, gi, name):
    S = P.shape[0]
    d = DILATIONS[gi]
    B = ATTN_STEPS
    E = ATTN_HEAD_DIM
    BR = B * d
    nq = ATT_ROWS // BR
    nsb = S // ATT_ROWS
    nblk = S // BR
    scale = E ** -0.5

    def body(q_ref, k_ref, v_ref, kp_ref, vp_ref, qx_ref, o_ref, l_ref, do_ref, dl_ref, ox_ref, lx_ref, dox_ref, dlx_ref,
             qw_ref, kw_ref, dp_in, dp_ref, dqw_ref, dkw_ref, stage_q, stage_k, stage_v):
        h = pl.program_id(0)
        s = pl.program_id(1)

        @pl.when(jnp.logical_and(h == 0, s == 0))
        def _():
            dqw_ref[...] = jnp.zeros_like(dqw_ref)
            dkw_ref[...] = jnp.zeros_like(dkw_ref)

        slope = _slope(gi, h) * float(d)
        qwv = qw_ref[...]
        kwv = kw_ref[...]
        qi = _iota((B, B), 0)
        kj = _iota((B, B), 1)
        rel_c = (qi - kj).astype(F32)
        rel_p = rel_c + float(B)
        band = qi <= kj
        bias_c = jnp.where(qi >= kj, -slope * rel_c, NEG)
        bias_p = jnp.where(band, -slope * rel_p, NEG)
        bias_p_first = jnp.where(jnp.logical_and(band, s > 0), -slope * rel_p, NEG)
        bias_p_last = jnp.where(jnp.logical_and(band, s < nsb - 1), -slope * rel_p, NEG)
        dqw = jnp.zeros((1, E), F32)
        dkw = jnp.zeros((1, E), F32)
        for j in range(nq):
            for r in range(d):
                rows = _strided(j * BR + r, d)
                if j == 0:
                    kpv, vpv, bB = kp_ref[_strided(r, d), :], vp_ref[_strided(r, d), :], bias_p_first
                else:
                    prow = _strided((j - 1) * BR + r, d)
                    kpv, vpv, bB = k_ref[prow, :], v_ref[prow, :], bias_p
                if j == nq - 1:
                    xr = _strided(r, d)
                    qxv, oxv, lxv, doxv, dlxv, bC = (qx_ref[xr, :], ox_ref[xr, :], lx_ref[xr, :], dox_ref[xr, :],
                                                     dlx_ref[xr, :], bias_p_last)
                else:
                    xr = _strided((j + 1) * BR + r, d)
                    qxv, oxv, lxv, doxv, dlxv, bC = (q_ref[xr, :], o_ref[xr, :], l_ref[xr, :], do_ref[xr, :],
                                                     dl_ref[xr, :], bias_p)
                qh, rq = _rmsn(q_ref[rows, :])
                kh, rk = _rmsn(k_ref[rows, :])
                kph, _ = _rmsn(kpv)
                qxh, _ = _rmsn(qxv)
                qn = (qh * qwv).astype(BF16)
                kn = (kh * kwv).astype(BF16)
                kpn = (kph * kwv).astype(BF16)
                qxn = (qxh * qwv).astype(BF16)
                v = v_ref[rows, :].astype(BF16)
                vp = vpv.astype(BF16)
                lse_c = l_ref[rows, :]
                pA = jnp.exp(_nt(qn, kn) * scale + bias_c - lse_c)
                pB = jnp.exp(_nt(qn, kpn) * scale + bB - lse_c)
                pC = jnp.exp(_nt(qxn, kn) * scale + bC - lxv)
                doc = do_ref[rows, :]
                docb = doc.astype(BF16)
                doxb = doxv.astype(BF16)
                corr_c = jnp.sum(dl_ref[rows, :] - doc * o_ref[rows, :], axis=1, keepdims=True)
                corr_n = jnp.sum(dlxv - doxv * oxv, axis=1, keepdims=True)
                dsA = (pA * (_nt(docb, v) + corr_c)).astype(BF16)
                dsB = (pB * (_nt(docb, vp) + corr_c)).astype(BF16)
                dsC = (pC * (_nt(doxb, v) + corr_n)).astype(BF16)
                dqn = (_nn(dsA, kn) + _nn(dsB, kpn)) * scale
                dkn = (_tn(dsA, qn) + _tn(dsC, qxn)) * scale
                dv = _tn(pA.astype(BF16), docb) + _tn(pC.astype(BF16), doxb)
                dqw = dqw + jnp.sum(dqn * qh, axis=0, keepdims=True)
                dkw = dkw + jnp.sum(dkn * kh, axis=0, keepdims=True)
                stage_q[rows, :] = _rms_bwd(qh, rq, qwv, dqn)
                stage_k[rows, :] = _rms_bwd(kh, rk, kwv, dkn)
                stage_v[rows, :] = dv
        dqw_ref[...] += dqw
        dkw_ref[...] += dkw
        dp_ref[:, 0:E] = stage_q[...].astype(BF16)
        dp_ref[:, E:2 * E] = stage_k[...].astype(BF16)
        dp_ref[:, 2 * E:3 * E] = stage_v[...].astype(BF16)

    def cur(i):
        return pl.BlockSpec((ATT_ROWS, E), lambda h, s: (s, (gi * 4 + h) * 3 + i))

    def prev(i):
        return pl.BlockSpec((BR, E), lambda h, s: (jnp.maximum(s * nq - 1, 0), (gi * 4 + h) * 3 + i))

    nxt_q = pl.BlockSpec((BR, E), lambda h, s: (jnp.minimum((s + 1) * nq, nblk - 1), (gi * 4 + h) * 3))
    ocur = pl.BlockSpec((ATT_ROWS, E), lambda h, s: (s, h))
    onxt = pl.BlockSpec((BR, E), lambda h, s: (jnp.minimum((s + 1) * nq, nblk - 1), h))
    wv = pl.BlockSpec((1, E), lambda h, s: (0, 0))
    return pl.pallas_call(
        body, grid=(4, nsb),
        in_specs=[cur(0), cur(1), cur(2), prev(1), prev(2), nxt_q, ocur, ocur, ocur, ocur, onxt, onxt, onxt, onxt, wv, wv,
                  _any()],
        out_specs=[pl.BlockSpec((ATT_ROWS, HEAD_W), lambda h, s: (s, gi * 4 + h)), wv, wv],
        out_shape=[_sds(dP.shape, BF16), _sds((1, E), F32), _sds((1, E), F32)],
        scratch_shapes=[pltpu.VMEM((ATT_ROWS, E), F32)] * 3,
        input_output_aliases={16: 0}, name=name,
        compiler_params=_cp(("arbitrary", "arbitrary")))(
            P, P, P, P, P, P, o, lse, do, dlse, o, lse, do, dlse, qw, kw, dP)


def attn_merge_fwd(os_, ls_, name):
    S = os_[0].shape[0]
    T = ROW_TILE

    def body(o0, o1, o2, l0, l1, l2, out):
        a, b, c = l0[...], l1[...], l2[...]
        m = jnp.maximum(jnp.maximum(a, b), c)
        ea, eb, ec = jnp.exp(a - m), jnp.exp(b - m), jnp.exp(c - m)
        inv = 1.0 / (ea + eb + ec)
        out[...] = ((ea * o0[...] + eb * o1[...] + ec * o2[...]) * inv).astype(BF16)

    return _rowcall(body, S, T, [_rows(T, 512)] * 6, _rows(T, 512), _sds((S, 512), BF16), name)(*os_, *ls_)


def attn_merge_bwd(os_, ls_, dom, name):
    S = os_[0].shape[0]
    T = ROW_TILE

    def body(o0, o1, o2, l0, l1, l2, d_ref, do0, do1, do2, dl0, dl1, dl2):
        a, b, c = l0[...], l1[...], l2[...]
        m = jnp.maximum(jnp.maximum(a, b), c)
        ea, eb, ec = jnp.exp(a - m), jnp.exp(b - m), jnp.exp(c - m)
        inv = 1.0 / (ea + eb + ec)
        wa, wb, wc = ea * inv, eb * inv, ec * inv
        dv = d_ref[...]
        do0[...] = wa * dv
        do1[...] = wb * dv
        do2[...] = wc * dv
        ga, gb, gc = dv * o0[...], dv * o1[...], dv * o2[...]
        tot = wa * ga + wb * gb + wc * gc
        dl0[...] = wa * (ga - tot)
        dl1[...] = wb * (gb - tot)
        dl2[...] = wc * (gc - tot)

    sp = _rows(T, 512)
    return _rowcall(body, S, T, [sp] * 7, [sp] * 6, [_sds((S, 512), F32)] * 6, name)(*os_, *ls_, dom)


def _counts(i, T, rows, w, offset=0):
    t = i * T + offset + _iota((rows, 1), 0)
    return jnp.minimum(t + 1, w).astype(F32)


def pool_fwd(P, wmix, scale, name):
    S = P.shape[0]
    T = ROW_TILE
    GW = 256
    uc = U_OFF // 1024
    hb = T // POOL_HALO

    def body(u_ref, h_ref, wm_ref, sc_ref, pl_ref, yp_ref):
        i = pl.program_id(0)
        u = u_ref[...]
        ext = jnp.concatenate([jnp.where(i > 0, h_ref[...], 0.0), u], axis=0)
        for g, w in enumerate(POOL_WINDOWS):
            sl = slice(g * GW, (g + 1) * GW)
            s = ext[:, sl]
            k = 1
            while k < w:
                s = s + pltpu.roll(s, k, 0)
                k *= 2
            pooled = (s[POOL_HALO:] / _counts(i, T, T, w) - u[:, sl]).astype(BF16)
            pl_ref[:, sl] = pooled
            yp_ref[:, sl] = (_nn(pooled, wm_ref[g]) * sc_ref[:, sl]).astype(BF16)

    return _rowcall(
        body, S, T,
        [_rows(T, 1024, uc), pl.BlockSpec((POOL_HALO, 1024), lambda i: (jnp.maximum(i * hb - 1, 0), uc)),
         _vec((4, GW, GW)), _vec((1, 1024))],
        [_rows(T, 1024), _rows(T, 1024)], [_sds((S, 1024), BF16), _sds((S, 1024), BF16)], name)(P, P, wmix, scale)


def pool_bwd(dyp, pooled, dP, wmix, scale, name):
    S = dyp.shape[0]
    T = ROW_TILE
    nt = S // T
    GW = 256
    uc = U_OFF // 1024
    hb = T // POOL_HALO
    TE = T + POOL_HALO

    def body(d_ref, dn_ref, p_ref, wm_ref, sc_ref, dp_in, du_ref, dwm_ref, dsc_ref):
        i = pl.program_id(0)
        _zero_at_first(i, dwm_ref, dsc_ref)
        dv = d_ref[...]
        dn = jnp.where(i < nt - 1, dn_ref[...], 0.0)
        for g, w in enumerate(POOL_WINDOWS):
            sl = slice(g * GW, (g + 1) * GW)
            pg = p_ref[:, sl]
            wm = wm_ref[g]
            dsc_ref[:, sl] += jnp.sum(dv[:, sl] * _nn(pg, wm), axis=0, keepdims=True)
            dmb = (dv[:, sl] * sc_ref[:, sl]).astype(BF16)
            dmnb = (dn[:, sl] * sc_ref[:, sl]).astype(BF16)
            dwm_ref[g] += _tn(pg, dmb)
            dpl = _nt(dmb, wm)
            dpln = _nt(dmnb, wm)
            s = jnp.concatenate([dpl / _counts(i, T, T, w), dpln / _counts(i, T, POOL_HALO, w, T)], axis=0)
            k = 1
            while k < w:
                s = s + pltpu.roll(s, TE - k, 0)
                k *= 2
            du_ref[:, sl] = (s[:T] - dpl).astype(BF16)

    return _rowcall(
        body, S, T,
        [_rows(T, 1024), pl.BlockSpec((POOL_HALO, 1024), lambda i: (jnp.minimum((i + 1) * hb, S // POOL_HALO - 1), 0)),
         _rows(T, 1024), _vec((4, GW, GW)), _vec((1, 1024)), _any()],
        [_rows(T, 1024, uc), _vec((4, GW, GW)), _vec((1, 1024))],
        [_sds(dP.shape, BF16), _sds((4, GW, GW), F32), _sds((1, 1024), F32)], name,
        aliases={5: 0})(dyp, dyp, pooled, wmix, scale, dP)


def gate_fwd(P, ys, ya, yp, name):
    S = ys.shape[0]
    T = ROW_TILE
    gc = GATE_OFF // 1024

    def body(g0, g1, g2, a, b, c, out):
        out[...] = (_sigmoid(g0[...]) * a[...] + _sigmoid(g1[...]) * b[...] + _sigmoid(g2[...]) * c[...]).astype(BF16)

    r = _rows(T, 1024)
    return _rowcall(body, S, T, [_rows(T, 1024, gc), _rows(T, 1024, gc + 1), _rows(T, 1024, gc + 2), r, r, r], r,
                    _sds((S, 1024), BF16), name)(P, P, P, ys, ya, yp)


def gate_bwd(P, ys, ya, yp, dm, dP, name):
    S = ys.shape[0]
    T = ROW_TILE
    gc = GATE_OFF // 1024

    def body(g0, g1, g2, a, b, c, d_ref, dp_in, dg_ref, da, db, dc):
        dv = d_ref[...]
        outs = []
        for gr, yr, dyr in ((g0, a, da), (g1, b, db), (g2, c, dc)):
            s = _sigmoid(gr[...])
            dyr[...] = (s * dv).astype(BF16)
            outs.append(dv * yr[...] * s * (1.0 - s))
        dg_ref[...] = jnp.concatenate(outs, axis=1).astype(BF16)

    r = _rows(T, 1024)
    return _rowcall(body, S, T, [_rows(T, 1024, gc), _rows(T, 1024, gc + 1), _rows(T, 1024, gc + 2), r, r, r, r, _any()],
                    [_rows(T, GATE_W, GATE_OFF // GATE_W), r, r, r],
                    [_sds(dP.shape, BF16)] + [_sds((S, 1024), BF16)] * 3, name,
                    aliases={7: 0})(P, P, P, ys, ya, yp, dm, dP)


def resid_bwd(dx, g, y, name):
    S, D = dx.shape
    T = ROW_TILE

    def body(d_ref, g_ref, y_ref, dy_ref, dg_ref):
        _zero_at_first(pl.program_id(0), dg_ref)
        dv = d_ref[...]
        dy_ref[...] = (g_ref[...] * dv).astype(BF16)
        dg_ref[...] += jnp.sum(dv * y_ref[...], axis=0, keepdims=True)

    r = _rows(T, D)
    return _rowcall(body, S, T, [r, _vec((1, D)), r], [r, _vec((1, D))], [_sds((S, D), BF16), _sds((1, D), F32)],
                    name)(dx, g, y)


def loss_fwd_bwd(y, target, name):
    S, D = y.shape
    T = ROW_TILE

    def body(y_ref, t_ref, acc_ref, dy_ref):
        _zero_at_first(pl.program_id(0), acc_ref)
        e = y_ref[...] - t_ref[...]
        acc_ref[...] += jnp.sum(e * e, axis=0, keepdims=True)
        dy_ref[...] = e * (1.0 / D)

    r = _rows(T, D)
    return _rowcall(body, S, T, [r, r], [_vec((1, D)), r], [_sds((1, D), F32), _sds((S, D), F32)], name)(y, target)


def silu_rows(c, name):
    def body(c_ref, o_ref):
        o_ref[...] = _silu(c_ref[...])

    return pl.pallas_call(body, out_shape=_sds(c.shape, F32), name=name)(c)


def adamw(parts, w, m, v, name, row0=0, prev=None):
    n, R, C = parts.shape
    Rt = w.shape[0]
    tr = R
    while tr * C * 4 > (1 << 20) and tr % 16 == 0:
        tr //= 2
    b0 = row0 // tr
    c1 = 1.0 / (1.0 - ADAM_B1 ** ADAM_STEP)
    c2 = 1.0 / (1.0 - ADAM_B2 ** ADAM_STEP)

    def body(p_ref, w_ref, m_ref, v_ref, *rest):
        g_ref, d_ref, nm_ref, nv_ref = rest[-4:]
        g = p_ref[0].astype(F32)
        for k in range(1, n):
            g = g + p_ref[k].astype(F32)
        nm = ADAM_B1 * m_ref[...] + (1.0 - ADAM_B1) * g
        nv = ADAM_B2 * v_ref[...] + (1.0 - ADAM_B2) * (g * g)
        g_ref[...] = g
        nm_ref[...] = nm
        nv_ref[...] = nv
        d_ref[...] = -ADAM_LR * ((nm * c1) / (jnp.sqrt(nv * c2) + ADAM_EPS) + ADAM_WD * w_ref[...])

    r = pl.BlockSpec((tr, C), lambda i: (b0 + i, 0))
    ins = [parts, w, m, v] + (list(prev) if prev is not None else [])
    in_specs = [pl.BlockSpec((n, tr, C), lambda i: (0, i, 0)), r, r, r] + ([_any()] * 4 if prev is not None else [])
    aliases = {4 + k: k for k in range(4)} if prev is not None else None
    return _rowcall(body, R, tr, in_specs, [r, r, r, r], [_sds((Rt, C), F32)] * 4, name, aliases=aliases)(*ins)


FWD_HOSTS = {"in_proj": ("w_in",), "attn_fwd0": ("w_ff1",), "attn_fwd1": ("w_ff2",),
             "attn_fwd2": ("w_ssd_out", "w_attn_out", "w_pool_mix", "w_pool_out", "w_out")}
W_IN_PARTS = 4
BWD_HOSTS = {"ssd_bwd": ("w_in_0", "w_in_1", "w_in_2"), "d_h2": ("w_in_3",),
             "d_w_in": ("w_ff1", "w_out", "w_ssd_out"), "d_h1": ("w_ff2", "w_attn_out", "w_pool_mix", "w_pool_out")}


def _cargo_rider(kind, hosts, host, cargo):
    if not cargo or hosts[host][0] not in cargo:
        return None
    return Rider(kind, [cargo[k] for k in hosts[host]])


def _ride(mmargs, kind, hosts, host, cargo, got, **kw):
    rider = _cargo_rider(kind, hosts, host, cargo) if host in hosts else None
    if rider is None:
        return mm(*mmargs, **kw)
    out, res = mm(*mmargs, rider=rider, **kw)
    got.update(zip(hosts[host], res))
    return out


def _epi_relu2(r):
    p = jnp.maximum(r, 0.0)
    return r, p * p


def _epi_relu2_bwd(r, f):
    return (2.0 * jnp.maximum(f, 0.0) * r,)


def _epi_resid(r, x, g):
    return r, x + g * r


def layer_fwd(x, mod, W, sm, tag, cargo=None):
    sh1, sc1, g1, sh2, sc2, g2 = mod
    sv, got = {}, {}
    ride = functools.partial(_ride, kind="gather", hosts=FWD_HOSTS, cargo=cargo, got=got)
    h1 = rms_mod_fwd(x, sm["norm1_w"], sc1, sh1, f"rms1_fwd_{tag}")
    P = ride((h1, W["in"], "nn", F32, f"in_proj_{tag}"), host="in_proj")
    pre, xc = conv_fwd(P, sm["conv_w"], sm["conv_b"], f"conv_fwd_{tag}")
    y, st, _ = ssd_fwd(xc, P, sm["dt_bias"], sm["a_log"], sm["d_skip"], f"ssd_fwd_{tag}")
    ysn = ssd_post_fwd(y, P, sm["ssd_norm_w"], f"ssd_post_fwd_{tag}")
    ys = mm(ysn, W["ssd_out"], "nn", F32, f"ssd_out_{tag}")
    os_, ls_ = [], []
    for gi in range(3):
        host = f"attn_fwd{gi}"
        o, l, res = attn_fwd(P, sm["q_norm_w"], sm["k_norm_w"], gi, f"attn_fwd{gi}_{tag}",
                             rider=_cargo_rider("gather", FWD_HOSTS, host, cargo))
        got.update(zip(FWD_HOSTS[host], res))
        os_.append(o)
        ls_.append(l)
    om = attn_merge_fwd(os_, ls_, f"attn_merge_fwd_{tag}")
    ya = mm(om, W["attn_out"], "nn", F32, f"attn_out_{tag}")
    pooled, ypp = pool_fwd(P, W["pool_mix"], sm["pool_scale"], f"pool_fwd_{tag}")
    yp = mm(ypp, W["pool_out"], "nn", F32, f"pool_out_{tag}")
    m = gate_fwd(P, ys, ya, yp, f"gate_fwd_{tag}")
    mo, x1 = mm(m, W["out"], "nn", F32, f"mix_out_{tag}", extra=(x, g1), epi=_epi_resid, out_dtypes=(F32, F32))
    h2 = rms_mod_fwd(x1, sm["norm2_w"], sc2, sh2, f"rms2_fwd_{tag}")
    f, a = ride((h2, W["ff1"], "nn", F32, f"ff1_{tag}"), host="ff1", epi=_epi_relu2, out_dtypes=(F32, BF16),
                col_blocks=True)
    o2, x2 = ride((a, W["ff2"], "nn", F32, f"ff2_{tag}"), host="ff2", extra=(x1, g2), epi=_epi_resid,
                  out_dtypes=(F32, F32))
    sv.update(x=x, h1=h1, P=P, pre=pre, xc=xc, y=y, st=st, ysn=ysn, ys=ys, os=os_, ls=ls_, om=om, ya=ya,
              pooled=pooled, ypp=ypp, yp=yp, m=m, mo=mo, x1=x1, h2=h2, f=f, a=a, o2=o2)
    return x2, sv, got


def layer_bwd(dx2, sv, mod, W, sm, tag, slabs, cargo=None):
    sh1, sc1, g1, sh2, sc2, g2 = mod
    S = dx2.shape[0]
    P = sv["P"]
    gw, gs, got, own, got_own = {}, {}, {}, {}, {}
    ride = functools.partial(_ride, kind="scatter", hosts=BWD_HOSTS, cargo=cargo, got=got)
    ride_own = functools.partial(_ride, kind="scatter", hosts=BWD_HOSTS, cargo=own, got=got_own)
    do2, dg2 = resid_bwd(dx2, g2, sv["o2"], f"resid2_bwd_{tag}")
    gw["ff2"] = ride((sv["a"], do2, "tn", BF16, f"d_w_ff2_{tag}"), host="d_w_ff2")
    df = ride((do2, W["ff2"], "nt", BF16, f"d_f_{tag}"), host="d_a", extra=(sv["f"],), epi=_epi_relu2_bwd)
    gw["ff1"] = ride((sv["h2"], df, "tn", BF16, f"d_w_ff1_{tag}"), host="d_w_ff1", col_blocks=True)
    dh2 = ride((df, W["ff1"], "nt", F32, f"d_h2_{tag}"), host="d_h2", col_blocks=True)
    dx1, gs["norm2_w"], dsc2, dsh2, dmo, dg1 = rms_mod_bwd(sv["x1"], dh2, dx2, sm["norm2_w"], sc2, f"rms2_bwd_{tag}",
                                                          gated=(g1, sv["mo"]))
    gw["out"] = mm(sv["m"], dmo, "tn", BF16, f"d_w_out_{tag}")
    dm = mm(dmo, W["out"], "nt", F32, f"d_m_{tag}")
    dP = lax.empty((S, NP), BF16)
    dP, dys, dya, dyp = gate_bwd(P, sv["ys"], sv["ya"], sv["yp"], dm, dP, f"gate_bwd_{tag}")
    gw["ssd_out"] = mm(sv["ysn"], dys, "tn", BF16, f"d_w_ssd_out_{tag}")
    dysn = mm(dys, W["ssd_out"], "nt", F32, f"d_ysn_{tag}")
    dy, dP, gs["ssd_norm_w"] = ssd_post_bwd(sv["y"], P, dysn, dP, sm["ssd_norm_w"], f"ssd_post_bwd_{tag}")
    dxc, dP, gs["dt_bias"], gs["a_log"], dDl, res = ssd_bwd(
        sv["xc"], P, sv["st"], dy, dP, sm["dt_bias"], sm["a_log"], sm["d_skip"], f"ssd_bwd_{tag}",
        rider=_cargo_rider("scatter", BWD_HOSTS, "ssd_bwd", cargo))
    got.update(zip(BWD_HOSTS["ssd_bwd"], res))
    gs["d_skip"] = dDl.reshape(SSD_HEADS, SSD_HEAD_DIM).sum(axis=1)
    gs["dt_bias"] = gs["dt_bias"][0, :SSD_HEADS]
    gs["a_log"] = gs["a_log"][0, :SSD_HEADS]
    dP, gs["conv_w"], gs["conv_b"] = conv_bwd(dxc, sv["pre"], P, dP, sm["conv_w"], f"conv_bwd_{tag}")
    gw["attn_out"] = mm(sv["om"], dya, "tn", BF16, f"d_w_attn_out_{tag}")
    dom = mm(dya, W["attn_out"], "nt", F32, f"d_om_{tag}")
    mb = attn_merge_bwd(sv["os"], sv["ls"], dom, f"attn_merge_bwd_{tag}")
    dqw = dkw = None
    for gi in range(3):
        dP, a_, b_ = attn_bwd(P, sv["os"][gi], sv["ls"][gi], mb[gi], mb[3 + gi], dP, sm["q_norm_w"], sm["k_norm_w"], gi,
                              f"attn_bwd{gi}_{tag}")
        dqw = a_ if dqw is None else dqw + a_
        dkw = b_ if dkw is None else dkw + b_
    gs["q_norm_w"], gs["k_norm_w"] = dqw, dkw
    gw["pool_out"] = mm(sv["ypp"], dyp, "tn", BF16, f"d_w_pool_out_{tag}")
    dypp = mm(dyp, W["pool_out"], "nt", F32, f"d_ypp_{tag}")
    dP, dwm, gs["pool_scale"] = pool_bwd(dypp, sv["pooled"], dP, W["pool_mix"], sm["pool_scale"], f"pool_bwd_{tag}")
    gw["pool_mix"] = dwm.astype(BF16)
    own.update(slabs(gw))
    gw["in"] = ride_own((sv["h1"], dP, "tn", BF16, f"d_w_in_{tag}"), host="d_w_in")
    dh1 = ride_own((dP, W["in"], "nt", F32, f"d_h1_{tag}"), host="d_h1")
    dx, gs["norm1_w"], dsc1, dsh1 = rms_mod_bwd(sv["x"], dh1, dx1, sm["norm1_w"], sc1, f"rms1_bwd_{tag}")
    dmod = jnp.concatenate([dsh1, dsc1, dg1, dsh2, dsc2, dg2], axis=1)
    return dx, dmod, gs, gw, slabs({"in": gw["in"]}), got, got_own


def _w_in_to_fused(w):
    E = ATTN_HEAD_DIM
    cols = []
    for h in range(ATTN_HEADS):
        for off in (O_Q, O_K, O_V):
            cols.append(w[:, off + h * E: off + (h + 1) * E])
    cols += [w[:, O_XBC:O_DT], w[:, O_G:IN_WIDTH], w[:, O_Z:O_XBC], w[:, O_U:O_G], w[:, O_DT:O_Q],
             jnp.zeros((w.shape[0], DT_W - SSD_HEADS), w.dtype)]
    return jnp.concatenate(cols, axis=1)


def _fused_to_w_in(g):
    E = ATTN_HEAD_DIM
    q = [g[:, h * HEAD_W: h * HEAD_W + E] for h in range(ATTN_HEADS)]
    k = [g[:, h * HEAD_W + E: h * HEAD_W + 2 * E] for h in range(ATTN_HEADS)]
    v = [g[:, h * HEAD_W + 2 * E: (h + 1) * HEAD_W] for h in range(ATTN_HEADS)]
    return jnp.concatenate([g[:, Z_OFF:Z_OFF + Z_W], g[:, XBC_OFF:XBC_OFF + XBC_W], g[:, DT_OFF:DT_OFF + SSD_HEADS]]
                           + q + k + v + [g[:, U_OFF:U_OFF + U_W], g[:, GATE_OFF:GATE_OFF + GATE_W]], axis=1)


def _cols_from_shards(g):
    return jnp.moveaxis(g, 0, 1).reshape(g.shape[1], N_DEV * g.shape[2])


def _cols_to_shards(w):
    R = w.shape[0]
    return jnp.moveaxis(w.reshape(R, N_DEV, w.shape[1] // N_DEV), 1, 0)


BIG = ("w_in", "w_ssd_out", "w_attn_out", "w_pool_mix", "w_pool_out", "w_out", "w_ff1", "w_ff2")
SMALL = ("b_ada", "norm1_w", "norm2_w", "conv_b", "dt_bias", "a_log", "d_skip", "ssd_norm_w", "q_norm_w", "k_norm_w",
         "pool_scale")
WEIGHTS = ("w_ada", "b_ada", "norm1_w", "norm2_w", "w_in", "conv_w", "conv_b", "dt_bias", "a_log", "d_skip",
           "ssd_norm_w", "w_ssd_out", "q_norm_w", "k_norm_w", "w_attn_out", "w_pool_mix", "pool_scale", "w_pool_out",
           "w_out", "w_ff1", "w_ff2")


def _pack(arrs):
    flat = jnp.concatenate([a.reshape(-1).astype(F32) for a in arrs])
    pad = (-flat.shape[0]) % (8 * LANES)
    return jnp.pad(flat, (0, pad)).reshape(-1, LANES)


def _unpack(packed, shapes):
    flat = packed.reshape(-1)
    out, off = [], 0
    for s in shapes:
        n = int(np.prod(s))
        out.append(flat[off:off + n].reshape(s))
        off += n
    return out


def _pad_lanes(v):
    return jnp.pad(v, ((0, 0), (0, LANES - v.shape[1])))[:, None, :]


def _step(x, c, target, p, m_, v_):
    Ld = p["w_in"].shape[0]
    me = 4 * lax.axis_index("x") + 2 * lax.axis_index("y") + lax.axis_index("c")

    local_bf16 = {k: p[k].astype(BF16) for k in BIG}

    def local_blocks(l):
        return {k: (local_bf16[k], l) for k in BIG}

    blk = local_blocks(0)
    gathered = all_gather([blk[k] for k in BIG] + [c, p["conv_w"]], "gather_first")
    G = dict(zip(BIG, gathered[:len(BIG)]))
    c_all = gathered[len(BIG)].reshape(N_DEV, D_MODEL)
    conv_w_full = jnp.moveaxis(gathered[len(BIG) + 1], 0, 2).reshape(Ld, 4, XBC_W)

    def layer_weights(G):
        W = {}
        W["in"] = _w_in_to_fused(_cols_from_shards(G["w_in"]))
        W["ssd_out"] = G["w_ssd_out"].reshape(1024, D_MODEL)
        W["attn_out"] = _cols_from_shards(G["w_attn_out"])
        W["pool_mix"] = jnp.moveaxis(G["w_pool_mix"], 0, 1).reshape(4, 256, 256)
        W["pool_out"] = G["w_pool_out"].reshape(1024, D_MODEL)
        W["out"] = G["w_out"].reshape(D_MODEL, D_MODEL)
        W["ff1"] = G["w_ff1"]
        W["ff2"] = G["w_ff2"].reshape(4 * D_MODEL, D_MODEL)
        return W

    def slabs(gw):
        out = {}
        if "in" in gw:
            full = _cols_to_shards(_fused_to_w_in(gw["in"]))
            rows = full.shape[1] // W_IN_PARTS
            for q in range(W_IN_PARTS):
                out[f"w_in_{q}"] = full[:, q * rows:(q + 1) * rows]
            return out
        out["w_ssd_out"] = gw["ssd_out"].reshape(N_DEV, -1, D_MODEL)
        out["w_attn_out"] = _cols_to_shards(gw["attn_out"])
        out["w_pool_mix"] = jnp.moveaxis(gw["pool_mix"].reshape(4, N_DEV, 32, 256), 1, 0)
        out["w_pool_out"] = gw["pool_out"].reshape(N_DEV, -1, D_MODEL)
        out["w_out"] = gw["out"].reshape(N_DEV, -1, D_MODEL)
        out["w_ff1"] = gw["ff1"]
        out["w_ff2"] = gw["ff2"].reshape(N_DEV, -1, D_MODEL)
        return out

    def layer_small(l):
        sm = {k: p[k][l][None, :] for k in ("norm1_w", "norm2_w", "conv_b", "ssd_norm_w", "q_norm_w", "k_norm_w",
                                            "pool_scale")}
        for k in ("dt_bias", "a_log", "d_skip"):
            sm[k] = jnp.pad(p[k][l], (0, LANES - SSD_HEADS))[None, :]
        sm["conv_w"] = conv_w_full[l]
        return sm

    cond = silu_rows(c_all, "cond_silu")
    ncol = p["w_ada"].shape[2]
    b_mine = lax.dynamic_slice_in_dim(p["b_ada"], me * ncol, ncol, axis=1)
    parts = [mm(cond, p["w_ada"][l], "nn", F32, f"ada_{l}") + b_mine[l][None, :] for l in range(Ld)]
    mod_cols = jnp.stack(parts, axis=1)
    mod = all_to_all([mod_cols], "scatter_mod")[0]
    mod = jnp.moveaxis(mod, 0, 1).reshape(Ld, 6 * D_MODEL)

    def mods(l):
        return [mod[l, k * D_MODEL:(k + 1) * D_MODEL][None, :] for k in range(6)]

    saved, Ws, sms = [], [], []
    h = x
    for l in range(Ld):
        Ws.append(layer_weights(G))
        sms.append(layer_small(l))
        h, sv, G = layer_fwd(h, mods(l), Ws[l], sms[l], f"l{l}", local_blocks(l + 1) if l + 1 < Ld else None)
        saved.append(sv)
    lsum, dh = loss_fwd_bwd(h, target, "loss")
    loss = 0.5 / D_MODEL * jnp.sum(lsum)
    gss, dmods, recvs = [None] * Ld, [None] * Ld, [dict() for _ in range(Ld)]
    cargo = None
    for l in reversed(range(Ld)):
        dh, dmods[l], gss[l], _, nxt, got, got_own = layer_bwd(dh, saved[l], mods(l), Ws[l], sms[l], f"l{l}", slabs, cargo)
        recvs[l].update(got_own)
        if cargo is not None:
            recvs[l + 1].update(got)
        cargo = nxt
    dmod = jnp.concatenate(dmods, axis=0)
    dmod_cols = jnp.moveaxis(dmod.reshape(Ld, N_DEV, ncol), 1, 0)
    parts_in = [f"w_in_{q}" for q in range(W_IN_PARTS)]
    last = all_to_all([cargo[k] for k in parts_in] + [dmod_cols], "scatter_last")
    recvs[0].update(zip(parts_in, last[:W_IN_PARTS]))
    dmod_all = last[W_IN_PARTS]

    out_g, out_d, out_m, out_v = {}, {}, {}, {}

    def finish(k, res):
        shp = p[k].shape
        out_g[k], out_d[k], out_m[k], out_v[k] = [t.reshape(shp) for t in res]

    for k in BIG:
        C = p[k].shape[-1]
        rows = int(np.prod(p[k].shape[1:-1]))
        res = None
        for l in reversed(range(Ld)):
            pieces = [(f"{k}_{q}", rows // W_IN_PARTS) for q in range(W_IN_PARTS)] if k == "w_in" else [(k, rows)]
            for q, (key, nrows) in enumerate(pieces):
                res = adamw(recvs[l][key].reshape(N_DEV, nrows, C), p[k].reshape(-1, C), m_[k].reshape(-1, C),
                            v_[k].reshape(-1, C), f"adamw_{key}_l{l}", row0=l * rows + q * nrows, prev=res)
        finish(k, res)
    g_ada = jnp.stack([mm(cond, dmod_all[:, l], "tn", F32, f"d_w_ada_{l}") for l in range(Ld)], axis=0)
    finish("w_ada", adamw(g_ada.reshape(1, -1, ncol), p["w_ada"].reshape(-1, ncol), m_["w_ada"].reshape(-1, ncol),
                          v_["w_ada"].reshape(-1, ncol), "adamw_w_ada"))

    small_g = {k: jnp.stack([gss[l][k].reshape(p[k].shape[1:]) for l in range(Ld)], axis=0) for k in SMALL if k != "b_ada"}
    small_g["b_ada"] = dmod
    conv_g = jnp.stack([gss[l]["conv_w"] for l in range(Ld)], axis=0)
    packed = _pack([small_g[k] for k in SMALL] + [conv_g])
    parts_small = all_gather([packed], "gather_small_grads")[0]
    zeros_conv = jnp.zeros(conv_g.shape, F32)
    res = adamw(parts_small, _pack([p[k] for k in SMALL] + [zeros_conv]), _pack([m_[k] for k in SMALL] + [zeros_conv]),
                _pack([v_[k] for k in SMALL] + [zeros_conv]), "adamw_small")
    shapes = [p[k].shape for k in SMALL] + [conv_g.shape]
    unp = [_unpack(t, shapes) for t in res]
    for i, k in enumerate(SMALL):
        out_g[k], out_d[k], out_m[k], out_v[k] = unp[0][i], unp[1][i], unp[2][i], unp[3][i]
    cw = p["conv_w"].shape[-1]
    conv_mine = lax.dynamic_slice_in_dim(unp[0][len(SMALL)], me * cw, cw, axis=2)
    finish("conv_w", adamw(conv_mine.reshape(1, -1, cw), p["conv_w"].reshape(-1, cw), m_["conv_w"].reshape(-1, cw),
                           v_["conv_w"].reshape(-1, cw), "adamw_conv_w"))

    loss = lax.psum(loss, ("x", "y", "c"))
    return loss, dh, out_g, out_d, out_m, out_v


def kernel(x, c, w_ada, b_ada, norm1_w, norm2_w, w_in, conv_w, conv_b, dt_bias, a_log, d_skip, ssd_norm_w, w_ssd_out, q_norm_w, k_norm_w, w_attn_out, w_pool_mix, pool_scale, w_pool_out, w_out, w_ff1, w_ff2, loss_target, m_w_ada, m_b_ada, m_norm1_w, m_norm2_w, m_w_in, m_conv_w, m_conv_b, m_dt_bias, m_a_log, m_d_skip, m_ssd_norm_w, m_w_ssd_out, m_q_norm_w, m_k_norm_w, m_w_attn_out, m_w_pool_mix, m_pool_scale, m_w_pool_out, m_w_out, m_w_ff1, m_w_ff2, v_w_ada, v_b_ada, v_norm1_w, v_norm2_w, v_w_in, v_conv_w, v_conv_b, v_dt_bias, v_a_log, v_d_skip, v_ssd_norm_w, v_w_ssd_out, v_q_norm_w, v_k_norm_w, v_w_attn_out, v_w_pool_mix, v_pool_scale, v_w_pool_out, v_w_out, v_w_ff1, v_w_ff2):
    p = dict(w_ada=w_ada, b_ada=b_ada, norm1_w=norm1_w, norm2_w=norm2_w, w_in=w_in, conv_w=conv_w, conv_b=conv_b,
             dt_bias=dt_bias, a_log=a_log, d_skip=d_skip, ssd_norm_w=ssd_norm_w, w_ssd_out=w_ssd_out, q_norm_w=q_norm_w,
             k_norm_w=k_norm_w, w_attn_out=w_attn_out, w_pool_mix=w_pool_mix, pool_scale=pool_scale,
             w_pool_out=w_pool_out, w_out=w_out, w_ff1=w_ff1, w_ff2=w_ff2)
    m_ = dict(w_ada=m_w_ada, b_ada=m_b_ada, norm1_w=m_norm1_w, norm2_w=m_norm2_w, w_in=m_w_in, conv_w=m_conv_w,
              conv_b=m_conv_b, dt_bias=m_dt_bias, a_log=m_a_log, d_skip=m_d_skip, ssd_norm_w=m_ssd_norm_w,
              w_ssd_out=m_w_ssd_out, q_norm_w=m_q_norm_w, k_norm_w=m_k_norm_w, w_attn_out=m_w_attn_out,
              w_pool_mix=m_w_pool_mix, pool_scale=m_pool_scale, w_pool_out=m_w_pool_out, w_out=m_w_out, w_ff1=m_w_ff1,
              w_ff2=m_w_ff2)
    v_ = dict(w_ada=v_w_ada, b_ada=v_b_ada, norm1_w=v_norm1_w, norm2_w=v_norm2_w, w_in=v_w_in, conv_w=v_conv_w,
              conv_b=v_conv_b, dt_bias=v_dt_bias, a_log=v_a_log, d_skip=v_d_skip, ssd_norm_w=v_ssd_norm_w,
              w_ssd_out=v_w_ssd_out, q_norm_w=v_q_norm_w, k_norm_w=v_k_norm_w, w_attn_out=v_w_attn_out,
              w_pool_mix=v_w_pool_mix, pool_scale=v_pool_scale, w_pool_out=v_w_pool_out, w_out=v_w_out, w_ff1=v_w_ff1,
              w_ff2=v_w_ff2)
    loss, dx, g, d, nm, nv = _step(x[0], c, loss_target[0], p, m_, v_)
    return (loss, dx[None], *[g[k] for k in WEIGHTS], *[d[k] for k in WEIGHTS], *[nm[k] for k in WEIGHTS],
            *[nv[k] for k in WEIGHTS])
```
